```python
import jax, jax.numpy as jnp
from jax import lax
import numpy as np

D_MODEL = 1024
BATCH = 8
SEQ = 2048
DEPTH = 1
DEC_BATCH = 32
DEC_SEQ = 1
PAST_LEN = 8192
PAGE_SIZE = 128

ATT_HEADS = 8
ATT_HD = D_MODEL // ATT_HEADS
ATT_W = ATT_HEADS * ATT_HD
ROT_DIM = ATT_HD // 4
ROPE_THETA = 500000.0
MOBA_BLOCK = 256
MOBA_TOPK = 3
Q_BLOCK = 64
ML_HEADS = 4
ML_HD = D_MODEL // ML_HEADS
ML_W = ML_HEADS * ML_HD
ML_CHUNK = 64
EPS = 1e-6
SPLIT_SIZES = (ATT_W, ATT_W, ATT_W, ATT_W, ML_W, ML_W, ML_W, ML_W, ML_W, D_MODEL, D_MODEL, ML_HEADS, ML_HEADS)
PROJ_W = 4 * ATT_W + 5 * ML_W + 2 * D_MODEL + 2 * ML_HEADS

kernel_name = 'moba_mlstm_parallel_hybrid_step'


def rms_norm(x, w):
    xf = x.astype(jnp.float32)
    y = xf * lax.rsqrt(jnp.mean(xf * xf, axis=-1, keepdims=True) + EPS)
    return (y * w.astype(jnp.float32)).astype(x.dtype)


def partial_rope(x, pos):
    half = ROT_DIM // 2
    inv = ROPE_THETA ** (-(jnp.arange(half, dtype=jnp.float32) * 2.0) / ROT_DIM)
    ang = pos.astype(jnp.float32)[:, None] * inv[None, :]
    cos, sin = jnp.cos(ang)[:, None, :], jnp.sin(ang)[:, None, :]
    xf = x.astype(jnp.float32)
    x1, x2 = xf[..., :half], xf[..., half:ROT_DIM]
    out = jnp.concatenate([x1 * cos - x2 * sin, x2 * cos + x1 * sin, xf[..., ROT_DIM:]], axis=-1)
    return out.astype(x.dtype)


def branch_inputs(x, pos, norm_w, w_in, b_gates, q_norm_w, k_norm_w):
    B, S, _ = x.shape
    xn = rms_norm(x, norm_w)
    points, acc = [], 0
    for s in SPLIT_SIZES[:-1]:
        acc += s
        points.append(acc)
    aq, ak, av, az, mq, mk, mv, mo, mz, ga, gm, mi, mf = jnp.split(xn @ w_in, points, axis=-1)
    aq = partial_rope(rms_norm(aq.reshape(B, S, ATT_HEADS, ATT_HD), q_norm_w), pos)
    ak = partial_rope(rms_norm(ak.reshape(B, S, ATT_HEADS, ATT_HD), k_norm_w), pos)
    av = av.reshape(B, S, ATT_HEADS, ATT_HD)
    mq = mq.reshape(B, S, ML_HEADS, ML_HD).transpose(0, 2, 1, 3)
    mk = mk.reshape(B, S, ML_HEADS, ML_HD).transpose(0, 2, 1, 3) * (ML_HD ** -0.5)
    mv = mv.reshape(B, S, ML_HEADS, ML_HD).transpose(0, 2, 1, 3)
    gates = jnp.concatenate([mi, mf], axis=-1).astype(jnp.float32) + b_gates.astype(jnp.float32)
    ig = gates[..., :ML_HEADS].transpose(0, 2, 1)
    lf = jax.nn.log_sigmoid(gates[..., ML_HEADS:]).transpose(0, 2, 1)
    return aq, ak, av, az, mq, mk, mv, ig, lf, mo, mz, ga, gm


def moba_prompt(q, k, v):
    B, S, H, hd = q.shape
    nb = -(-S // MOBA_BLOCK)
    pad = ((0, 0), (0, nb * MOBA_BLOCK - S), (0, 0), (0, 0))
    k_blk = jnp.pad(k.astype(jnp.float32), pad).reshape(B, nb, MOBA_BLOCK, H, hd)
    v_blk = jnp.pad(v.astype(jnp.float32), pad).reshape(B, nb, MOBA_BLOCK, H, hd)
    qf = q.astype(jnp.float32)
    k_mean = jnp.mean(k_blk, axis=2)
    gate = jnp.einsum('bshd,bnhd->bhsn', qf, k_mean)
    q_block = jnp.arange(S) // MOBA_BLOCK
    gate = jnp.where(jnp.arange(nb)[None, :] < q_block[:, None], gate, -jnp.inf)
    top = min(MOBA_TOPK, nb)
    _, sel = lax.top_k(gate, top)
    valid = sel < q_block[None, None, :, None]
    k_bh = k_blk.transpose(0, 3, 1, 2, 4)
    v_bh = v_blk.transpose(0, 3, 1, 2, 4)
    nqb = S // Q_BLOCK

    def to_items(t):
        t = jnp.moveaxis(t.reshape(B, H, nqb, Q_BLOCK, *t.shape[3:]), 2, 1)
        return t.reshape(B * nqb, H, Q_BLOCK, *t.shape[4:])

    items = (to_items(qf.transpose(0, 2, 1, 3)), to_items(sel), to_items(valid),
             jnp.repeat(jnp.arange(B, dtype=jnp.int32), nqb), jnp.tile(jnp.arange(nqb, dtype=jnp.int32), B))
    h_ix = jnp.arange(H)[:, None, None]
    k_local = jnp.arange(MOBA_BLOCK)
    q_local = jnp.arange(Q_BLOCK)
    scale = hd ** -0.5
    n_sel = top * MOBA_BLOCK

    def attend(item):
        qi, si, vi, b, qb = item
        kb, vb = k_bh[b], v_bh[b]
        k_sel, v_sel = kb[h_ix, si], vb[h_ix, si]
        s_sel = jnp.einsum('hqd,hqjkd->hqjk', qi, k_sel) * scale
        s_sel = jnp.where(vi[..., None], s_sel, -jnp.inf).reshape(H, Q_BLOCK, n_sel)
        own = (qb * Q_BLOCK) // MOBA_BLOCK
        k_own, v_own = kb[:, own], vb[:, own]
        s_own = jnp.einsum('hqd,hkd->hqk', qi, k_own) * scale
        causal = own * MOBA_BLOCK + k_local[None, :] <= qb * Q_BLOCK + q_local[:, None]
        s_own = jnp.where(causal, s_own, -jnp.inf)
        p = jax.nn.softmax(jnp.concatenate([s_sel, s_own], axis=-1), axis=-1)
        p_sel = p[..., :n_sel].reshape(H, Q_BLOCK, top, MOBA_BLOCK)
        return jnp.einsum('hqjk,hqjkd->qhd', p_sel, v_sel) + jnp.einsum('hqk,hkd->qhd', p[..., n_sel:], v_own)

    out = lax.map(attend, items)
    return out.reshape(B, S, H, hd).astype(q.dtype)


def moba_sample(q, k_new, v_new, cache_k, cache_v, page_table):
    DB, DS, H, hd = q.shape
    n_pages = page_table.shape[1]
    ppb = MOBA_BLOCK // PAGE_SIZE
    n_full = (n_pages * PAGE_SIZE) // MOBA_BLOCK
    own_page0 = n_full * ppb
    scale = hd ** -0.5
    qf = q.astype(jnp.float32)
    parts = []
    if n_full > 0:
        k_full = cache_k[page_table[:, :own_page0]].astype(jnp.float32)
        k_mean = jnp.mean(k_full.reshape(DB, n_full, MOBA_BLOCK, H, hd), axis=2)
        gate = jnp.einsum('bqhd,bnhd->bhqn', qf, k_mean)
        top = min(MOBA_TOPK, n_full)
        _, sel = lax.top_k(gate, top)
        logical = sel[..., None] * ppb + jnp.arange(ppb)
        phys = page_table[jnp.arange(DB)[:, None, None, None, None], logical]
        h_ix = jnp.arange(H)[None, :, None, None, None]
        k_sel = cache_k[phys, :, h_ix, :].reshape(DB, H, DS, top * MOBA_BLOCK, hd)
        v_sel = cache_v[phys, :, h_ix, :].reshape(DB, H, DS, top * MOBA_BLOCK, hd)
        parts.append((jnp.einsum('bqhd,bhqkd->bhqk', qf, k_sel.astype(jnp.float32)) * scale, v_sel, True))
    if n_pages > own_page0:
        k_own = cache_k[page_table[:, own_page0:]].reshape(DB, -1, H, hd)
        v_own = cache_v[page_table[:, own_page0:]].reshape(DB, -1, H, hd)
        parts.append((jnp.einsum('bqhd,bkhd->bhqk', qf, k_own.astype(jnp.float32)) * scale, v_own, False))
    s_new = jnp.einsum('bqhd,bkhd->bhqk', qf, k_new.astype(jnp.float32)) * scale
    s_new = jnp.where(jnp.tril(jnp.ones((DS, DS), dtype=bool)), s_new, -jnp.inf)
    parts.append((s_new, v_new, False))
    p = jax.nn.softmax(jnp.concatenate([s for s, _, _ in parts], axis=-1), axis=-1)
    out = jnp.zeros((DB, DS, H, hd), jnp.float32)
    off = 0
    for s, vals, per_query in parts:
        w = s.shape[-1]
        pb = p[..., off:off + w]
        off += w
        if per_query:
            out = out + jnp.einsum('bhqk,bhqkd->bqhd', pb, vals.astype(jnp.float32))
        else:
            out = out + jnp.einsum('bhqk,bkhd->bqhd', pb, vals.astype(jnp.float32))
    return out.astype(q.dtype)


def mlstm_chunk(carry, chunk):
    c_prev, n_prev, m_prev = (t.astype(jnp.float32) for t in carry)
    q, k, v, ig, lf = (t.astype(jnp.float32) for t in chunk)
    L = q.shape[2]
    b = jnp.cumsum(lf, axis=-1)
    causal = jnp.tril(jnp.ones((L, L), dtype=bool))
    log_w = jnp.where(causal, b[..., :, None] - b[..., None, :] + ig[..., None, :], -jnp.inf)
    log_inter = b + m_prev[..., None]
    m_t = jnp.maximum(log_inter, jnp.max(log_w, axis=-1))
    w_intra = jnp.exp(log_w - m_t[..., None])
    w_inter = jnp.exp(log_inter - m_t)
    s = jnp.einsum('bhtd,bhsd->bhts', q, k) * w_intra
    num = w_inter[..., None] * jnp.einsum('bhtd,bhde->bhte', q, c_prev) + jnp.einsum('bhts,bhse->bhte', s, v)
    den = w_inter * jnp.einsum('bhtd,bhd->bht', q, n_prev) + jnp.sum(s, axis=-1)
    h = num / jnp.maximum(jnp.abs(den), jnp.exp(-m_t))[..., None]
    m_new = m_t[..., -1]
    a_prev = jnp.exp(b[..., -1] + m_prev - m_new)
    a_s = jnp.exp(b[..., -1:] - b + ig - m_new[..., None])
    c_new = a_prev[..., None, None] * c_prev + jnp.einsum('bhs,bhsd,bhse->bhde', a_s, k, v)
    n_new = a_prev[..., None] * n_prev + jnp.einsum('bhs,bhsd->bhd', a_s, k)
    return (c_new, n_new, m_new), h


def mlstm_prompt(q, k, v, ig, lf):
    B, H, S, dk = q.shape
    dv = v.shape[-1]
    nc = S // ML_CHUNK

    def chunks(t):
        return jnp.moveaxis(t.reshape(B, H, nc, ML_CHUNK, *t.shape[3:]), 2, 0)

    init = (jnp.zeros((B, H, dk, dv), jnp.float32), jnp.zeros((B, H, dk), jnp.float32), jnp.zeros((B, H), jnp.float32))
    state, h = lax.scan(mlstm_chunk, init, (chunks(q), chunks(k), chunks(v), chunks(ig), chunks(lf)))
    h = jnp.moveaxis(h, 0, 2).reshape(B, H, S, dv).transpose(0, 2, 1, 3)
    return h, state


def merge_branches(x, attn, az, h, mo, mz, ga, gm, mh_norm_w, w_proj_attn, w_proj_mlstm, w_out):
    B, S, _ = x.shape
    h = h.astype(x.dtype).reshape(B, S, ML_W) * jax.nn.sigmoid(mo)
    h = rms_norm(h.reshape(B, S, ML_HEADS, ML_HD), mh_norm_w).reshape(B, S, ML_W)
    y_att = (attn.reshape(B, S, ATT_W) * jax.nn.silu(az)) @ w_proj_attn
    y_ml = (h * jax.nn.silu(mz)) @ w_proj_mlstm
    return x + (jax.nn.sigmoid(ga) * y_att + jax.nn.sigmoid(gm) * y_ml) @ w_out


def setup_inputs(seed: int = 0) -> dict:
    key = jax.random.key(seed)
    ks = jax.random.split(key, 20)
    n_pages = PAST_LEN // PAGE_SIZE
    n_used = DEC_BATCH * n_pages
    n_pool = (n_used * 5) // 4
    nrm = jax.random.normal
    page_table = jax.random.permutation(ks[0], n_pool)[:n_used].reshape(DEC_BATCH, n_pages).astype(jnp.int32)
    b_gates = jnp.concatenate([0.1 * nrm(ks[1], (DEPTH, ML_HEADS)), 3.0 + 0.5 * nrm(ks[2], (DEPTH, ML_HEADS))], axis=-1)
    return {
        'x_prompt': nrm(ks[3], (BATCH, SEQ, D_MODEL), jnp.float32),
        'x_sample': nrm(ks[4], (DEC_BATCH, DEC_SEQ, D_MODEL), jnp.float32),
        'cache_k': nrm(ks[5], (DEPTH, n_pool, PAGE_SIZE, ATT_HEADS, ATT_HD), jnp.float32),
        'cache_v': nrm(ks[6], (DEPTH, n_pool, PAGE_SIZE, ATT_HEADS, ATT_HD), jnp.float32),
        'page_table': page_table,
        'state_mlstm_C': 0.05 * nrm(ks[7], (DEPTH, DEC_BATCH, ML_HEADS, ML_HD, ML_HD), jnp.float32),
        'state_mlstm_n': 0.1 * nrm(ks[8], (DEPTH, DEC_BATCH, ML_HEADS, ML_HD), jnp.float32),
        'state_mlstm_m': 0.5 * nrm(ks[9], (DEPTH, DEC_BATCH, ML_HEADS), jnp.float32),
        'norm_w': 1.0 + 0.02 * nrm(ks[10], (DEPTH, D_MODEL), jnp.float32),
        'w_in': nrm(ks[11], (DEPTH, D_MODEL, PROJ_W), jnp.float32) * D_MODEL ** -0.5,
        'b_gates': b_gates,
        'q_norm_w': 1.0 + 0.02 * nrm(ks[12], (DEPTH, ATT_HD), jnp.float32),
        'k_norm_w': 1.0 + 0.02 * nrm(ks[13], (DEPTH, ATT_HD), jnp.float32),
        'mh_norm_w': 1.0 + 0.02 * nrm(ks[14], (DEPTH, ML_HD), jnp.float32),
        'w_proj_attn': nrm(ks[15], (DEPTH, ATT_W, D_MODEL), jnp.float32) * ATT_W ** -0.5,
        'w_proj_mlstm': nrm(ks[16], (DEPTH, ML_W, D_MODEL), jnp.float32) * ML_W ** -0.5,
        'w_out': nrm(ks[17], (DEPTH, D_MODEL, D_MODEL), jnp.float32) * D_MODEL ** -0.5,
    }


def reference(x_prompt, x_sample, cache_k, cache_v, page_table, state_mlstm_C, state_mlstm_n, state_mlstm_m,
              norm_w, w_in, b_gates, q_norm_w, k_norm_w, mh_norm_w, w_proj_attn, w_proj_mlstm, w_out):
    S = x_prompt.shape[1]
    DS = x_sample.shape[1]
    past = page_table.shape[1] * PAGE_SIZE
    pos_p = jnp.arange(S, dtype=jnp.int32)
    pos_s = past + jnp.arange(DS, dtype=jnp.int32)
    st_dtype = state_mlstm_C.dtype
    y_p, y_s = x_prompt, x_sample
    k_p, v_p, c_p, n_p, m_p = [], [], [], [], []
    k_s, v_s, c_s, n_s, m_s = [], [], [], [], []
    for l in range(DEPTH):
        aq, ak, av, az, mq, mk, mv, ig, lf, mo, mz, ga, gm = branch_inputs(
            y_p, pos_p, norm_w[l], w_in[l], b_gates[l], q_norm_w[l], k_norm_w[l])
        attn = moba_prompt(aq, ak, av)
        h, (c_new, n_new, m_new) = mlstm_prompt(mq, mk, mv, ig, lf)
        y_p = merge_branches(y_p, attn, az, h, mo, mz, ga, gm, mh_norm_w[l], w_proj_attn[l], w_proj_mlstm[l], w_out[l])
        k_p.append(ak); v_p.append(av); c_p.append(c_new); n_p.append(n_new); m_p.append(m_new)
        aq, ak, av, az, mq, mk, mv, ig, lf, mo, mz, ga, gm = branch_inputs(
            y_s, pos_s, norm_w[l], w_in[l], b_gates[l], q_norm_w[l], k_norm_w[l])
        attn = moba_sample(aq, ak, av, cache_k[l], cache_v[l], page_table)
        (c_new, n_new, m_new), h = mlstm_chunk((state_mlstm_C[l], state_mlstm_n[l], state_mlstm_m[l]), (mq, mk, mv, ig, lf))
        h = h.transpose(0, 2, 1, 3)
        y_s = merge_branches(y_s, attn, az, h, mo, mz, ga, gm, mh_norm_w[l], w_proj_attn[l], w_proj_mlstm[l], w_out[l])
        k_s.append(ak); v_s.append(av); c_s.append(c_new); n_s.append(n_new); m_s.append(m_new)
    return (y_p, y_s,
            jnp.stack(k_p), jnp.stack(v_p), jnp.stack(c_p).astype(st_dtype), jnp.stack(n_p).astype(st_dtype), jnp.stack(m_p).astype(st_dtype),
            jnp.stack(k_s), jnp.stack(v_s), jnp.stack(c_s).astype(st_dtype), jnp.stack(n_s).astype(st_dtype), jnp.stack(m_s).astype(st_dtype))
```

```python
import functools

import jax
import jax.numpy as jnp
from jax import lax
from jax.experimental import pallas as pl
from jax.experimental.pallas import tpu as pltpu

F32 = jnp.float32
BF16 = jnp.bfloat16
HIGHEST = lax.Precision.HIGHEST
NEG_INF = float("-inf")

ATT_HEADS = 8
ATT_HD = 128
ROT_DIM = ATT_HD // 4
ROPE_THETA = 500000.0
MOBA_BLOCK = 256
MOBA_TOPK = 3
ML_HEADS = 4
ML_HD = 256
PAGE_SIZE = 128
EPS = 1e-6
N_WIDE = 11
N_BF16_SECTIONS = 8
SEC_AZ, SEC_MQ, SEC_MK, SEC_MV, SEC_MO, SEC_MZ, SEC_GA, SEC_GM = range(8)
GATE_LANES = 128

VMEM_LIMIT_BYTES = 56 * 1024 * 1024

NT_DIMS = (((1,), (1,)), ((), ()))
TN_DIMS = (((0,), (0,)), ((), ()))


def _sigmoid(x):
    return 1.0 / (1.0 + jnp.exp(-x))


def _silu(x):
    return x * _sigmoid(x)


def _log_sigmoid(x):
    return jnp.minimum(x, 0.0) - jnp.log1p(jnp.exp(-jnp.abs(x)))


def _params(*semantics):
    return pltpu.CompilerParams(dimension_semantics=semantics, vmem_limit_bytes=VMEM_LIMIT_BYTES)


def _head_norm_rope(t, w, rc, rs1, rs2):
    half = ROT_DIM // 2
    outs = []
    for h in range(ATT_HEADS):
        th = t[:, h * ATT_HD:(h + 1) * ATT_HD]
        y = th * lax.rsqrt(jnp.mean(th * th, axis=-1, keepdims=True) + EPS) * w
        up = pltpu.roll(y, ATT_HD - half, axis=1)
        down = pltpu.roll(y, half, axis=1)
        outs.append(y * rc + up * rs1 + down * rs2)
    return outs


def _inproj_kernel(x_ref, nw_ref, w_ref, wg_ref, wgt_ref, rc_ref, rs1_ref, rs2_ref, qnw_ref, knw_ref,
                   q_ref, k_ref, v_ref, r_ref, g_ref, gt_ref):
    d = x_ref.shape[1]
    x = x_ref[...]
    xn = x * lax.rsqrt(jnp.mean(x * x, axis=-1, keepdims=True) + EPS) * nw_ref[...]
    xb = xn.astype(BF16)

    def section(i):
        return jnp.dot(xb, w_ref[:, i * d:(i + 1) * d], preferred_element_type=F32)

    rc, rs1, rs2 = rc_ref[...], rs1_ref[...], rs2_ref[...]
    for h, t in enumerate(_head_norm_rope(section(0), qnw_ref[...], rc, rs1, rs2)):
        q_ref[:, h * ATT_HD:(h + 1) * ATT_HD] = t
    for h, t in enumerate(_head_norm_rope(section(1), knw_ref[...], rc, rs1, rs2)):
        k_ref[:, h * ATT_HD:(h + 1) * ATT_HD] = t
    v_ref[...] = section(2)
    for j in range(N_BF16_SECTIONS):
        t = section(3 + j)
        if j == SEC_MK:
            t = t * (ML_HD ** -0.5)
        r_ref[:, j * d:(j + 1) * d] = t.astype(BF16)
    g_ref[...] = jnp.dot(xn, wg_ref[...], preferred_element_type=F32, precision=HIGHEST)
    gt_ref[...] = lax.dot_general(wgt_ref[...], xn, NT_DIMS, preferred_element_type=F32, precision=HIGHEST)


def _inproj(x, norm_w, w_wide, w_gate, w_gate_t, rope, q_norm_w, k_norm_w, tm, n_pos_blocks):
    t_rows, d = x.shape
    rc, rs1, rs2 = rope
    row = lambda i: (i, 0)
    fixed = lambda i: (0, 0)
    pos = lambda i: (i % n_pos_blocks, 0)
    once = pl.Buffered(1)
    return pl.pallas_call(
        _inproj_kernel,
        grid=(t_rows // tm,),
        in_specs=[
            pl.BlockSpec((tm, d), row),
            pl.BlockSpec((1, d), fixed),
            pl.BlockSpec((d, N_WIDE * d), fixed, pipeline_mode=once),
            pl.BlockSpec((d, GATE_LANES), fixed, pipeline_mode=once),
            pl.BlockSpec((GATE_LANES, d), fixed, pipeline_mode=once),
            pl.BlockSpec((tm, ATT_HD), pos),
            pl.BlockSpec((tm, ATT_HD), pos),
            pl.BlockSpec((tm, ATT_HD), pos),
            pl.BlockSpec((1, ATT_HD), fixed),
            pl.BlockSpec((1, ATT_HD), fixed),
        ],
        out_specs=[
            pl.BlockSpec((tm, d), row),
            pl.BlockSpec((tm, d), row),
            pl.BlockSpec((tm, d), row),
            pl.BlockSpec((tm, N_BF16_SECTIONS * d), row),
            pl.BlockSpec((tm, GATE_LANES), row),
            pl.BlockSpec((GATE_LANES, tm), lambda i: (0, i)),
        ],
        out_shape=[
            jax.ShapeDtypeStruct((t_rows, d), F32),
            jax.ShapeDtypeStruct((t_rows, d), F32),
            jax.ShapeDtypeStruct((t_rows, d), F32),
            jax.ShapeDtypeStruct((t_rows, N_BF16_SECTIONS * d), BF16),
            jax.ShapeDtypeStruct((t_rows, GATE_LANES), F32),
            jax.ShapeDtypeStruct((GATE_LANES, t_rows), F32),
        ],
        compiler_params=_params("arbitrary"),
        name="inproj",
    )(x, norm_w, w_wide, w_gate, w_gate_t, rc, rs1, rs2, q_norm_w, k_norm_w)


def _rope_tables(pos):
    half = ROT_DIM // 2
    inv = ROPE_THETA ** (-(jnp.arange(half, dtype=F32) * 2.0) / ROT_DIM)
    ang = pos.astype(F32)[:, None] * inv[None, :]
    cos, sin = jnp.cos(ang), jnp.sin(ang)
    n = pos.shape[0]
    zeros = jnp.zeros((n, half), F32)
    tail0 = jnp.zeros((n, ATT_HD - ROT_DIM), F32)
    rc = jnp.concatenate([cos, cos, jnp.ones((n, ATT_HD - ROT_DIM), F32)], axis=-1)
    rs1 = jnp.concatenate([-sin, zeros, tail0], axis=-1)
    rs2 = jnp.concatenate([zeros, sin, tail0], axis=-1)
    return rc, rs1, rs2


def _moba_kernel(q_ref, k_ref, v_ref, az_ref, o_ref, kmean_sc, kb_sc, vt_sc, bias_sc):
    i = pl.program_id(2)
    nb = kb_sc.shape[0]
    blk = MOBA_BLOCK

    @pl.when(i == 0)
    def _():
        for j in range(nb):
            kj = k_ref[j * blk:(j + 1) * blk, :]
            kmean_sc[j:j + 1, :] = jnp.sum(kj, axis=0, keepdims=True) * (1.0 / blk)
            kb_sc[j] = kj.astype(BF16)
            vt_sc[j] = v_ref[j * blk:(j + 1) * blk, :].T.astype(BF16)

    q = q_ref[...]
    gate = lax.dot_general(kmean_sc[...], q, NT_DIMS, preferred_element_type=F32, precision=HIGHEST)
    blk_id = lax.broadcasted_iota(jnp.int32, gate.shape, 0)
    past = blk_id < i
    gate = jnp.where(past, gate, NEG_INF)
    beaten = jnp.zeros(gate.shape, F32)
    for m in range(nb):
        gm = gate[m:m + 1, :]
        wins = jnp.where(gm > gate, 1.0, jnp.where(gm == gate, jnp.where(blk_id > m, 1.0, 0.0), 0.0))
        beaten = beaten + wins
    chosen = jnp.where(past, jnp.where(beaten < MOBA_TOPK, 0.0, NEG_INF), NEG_INF)
    bias_sc[...] = chosen

    qs = (q * (ATT_HD ** -0.5)).astype(BF16)

    def scores(j):
        return lax.dot_general(kb_sc[j], qs, NT_DIMS, preferred_element_type=F32)

    key_ix = lax.broadcasted_iota(jnp.int32, (blk, blk), 0)
    qry_ix = lax.broadcasted_iota(jnp.int32, (blk, blk), 1)
    s = jnp.where(key_ix <= qry_ix, scores(i), NEG_INF)
    m0 = jnp.max(s, axis=0, keepdims=True)
    p = jnp.exp(s - m0)
    l0 = jnp.sum(p, axis=0, keepdims=True)
    acc0 = jnp.dot(vt_sc[i], p.astype(BF16), preferred_element_type=F32)

    def body(j, carry):
        m_old, l_old, acc = carry
        sj = scores(j) + bias_sc[pl.ds(j, 1), :]
        m_new = jnp.maximum(m_old, jnp.max(sj, axis=0, keepdims=True))
        alpha = jnp.exp(m_old - m_new)
        pj = jnp.exp(sj - m_new)
        l_new = alpha * l_old + jnp.sum(pj, axis=0, keepdims=True)
        acc = alpha * acc + jnp.dot(vt_sc[j], pj.astype(BF16), preferred_element_type=F32)
        return m_new, l_new, acc

    _, l_fin, acc = lax.fori_loop(0, i, body, (m0, l0, acc0))
    out = (acc / l_fin).T
    o_ref[...] = (out * _silu(az_ref[...].astype(F32))).astype(BF16)


def _moba(q, k, v, r, batch, seq):
    t_rows, d = q.shape
    blk = MOBA_BLOCK
    nb = seq // blk
    return pl.pallas_call(
        _moba_kernel,
        grid=(batch, ATT_HEADS, nb),
        in_specs=[
            pl.BlockSpec((blk, ATT_HD), lambda b, h, i: (b * nb + i, h)),
            pl.BlockSpec((seq, ATT_HD), lambda b, h, i: (b, h)),
            pl.BlockSpec((seq, ATT_HD), lambda b, h, i: (b, h)),
            pl.BlockSpec((blk, ATT_HD), lambda b, h, i: (b * nb + i, SEC_AZ * ATT_HEADS + h)),
        ],
        out_specs=pl.BlockSpec((blk, ATT_HD), lambda b, h, i: (b * nb + i, h)),
        out_shape=jax.ShapeDtypeStruct((t_rows, d), BF16),
        scratch_shapes=[
            pltpu.VMEM((nb, ATT_HD), F32),
            pltpu.VMEM((nb, blk, ATT_HD), BF16),
            pltpu.VMEM((nb, ATT_HD, blk), BF16),
            pltpu.VMEM((nb, blk), F32),
        ],
        compiler_params=_params("arbitrary", "arbitrary", "arbitrary"),
        name="moba_prompt",
    )(q, k, v, r)


def _mlstm_post(h, mo, mz, nw):
    h = h * _sigmoid(mo)
    h = h * lax.rsqrt(jnp.mean(h * h, axis=-1, keepdims=True) + EPS) * nw
    return h * _silu(mz)


def _mlstm_kernel(q_ref, k_ref, v_ref, mo_ref, mz_ref, g_ref, gt_ref, bgr_ref, bgc_ref, nw_ref,
                  hm_ref, c_out, n_out, m_out, c_sc, n_sc, m_sc):
    c = pl.program_id(1)
    chunk = q_ref.shape[0]

    @pl.when(c == 0)
    def _():
        c_sc[...] = jnp.zeros(c_sc.shape, F32)
        n_sc[...] = jnp.zeros(n_sc.shape, F32)
        m_sc[...] = jnp.zeros(m_sc.shape, F32)

    gc = g_ref[...] + bgr_ref[...]
    lane = lax.broadcasted_iota(jnp.int32, gc.shape, 1)
    gc = jnp.where(lane >= ML_HEADS, _log_sigmoid(gc), gc)
    gr = gt_ref[0:2 * ML_HEADS, :] + bgc_ref[0:2 * ML_HEADS, :]
    sub = lax.broadcasted_iota(jnp.int32, gr.shape, 0)
    gr = jnp.where(sub >= ML_HEADS, _log_sigmoid(gr), gr)

    t_ix = lax.broadcasted_iota(jnp.int32, (chunk, chunk), 0)
    s_ix = lax.broadcasted_iota(jnp.int32, (chunk, chunk), 1)
    causal = s_ix <= t_ix
    tril = jnp.where(causal, 1.0, 0.0)
    bc = jnp.dot(tril, gc, preferred_element_type=F32, precision=HIGHEST)
    br = lax.dot_general(gr, tril, NT_DIMS, preferred_element_type=F32, precision=HIGHEST)

    for h in range(ML_HEADS):
        cols = slice(h * ML_HD, (h + 1) * ML_HD)
        q, k, v = q_ref[:, cols], k_ref[:, cols], v_ref[:, cols]
        ig_r, b_r = gr[h:h + 1, :], br[ML_HEADS + h:ML_HEADS + h + 1, :]
        ig_c, b_c = gc[:, h:h + 1], bc[:, ML_HEADS + h:ML_HEADS + h + 1]
        m_prev = m_sc[h:h + 1, 0:1]
        c_prev = c_sc[h]
        n_prev = n_sc[h:h + 1, :]

        log_w = jnp.where(causal, b_c - b_r + ig_r, NEG_INF)
        log_inter = b_c + m_prev
        m_t = jnp.maximum(log_inter, jnp.max(log_w, axis=-1, keepdims=True))
        w_intra = jnp.exp(log_w - m_t)
        w_inter = jnp.exp(log_inter - m_t)
        s = lax.dot_general(q, k, NT_DIMS, preferred_element_type=F32) * w_intra
        num = (w_inter * jnp.dot(q, c_prev.astype(BF16), preferred_element_type=F32)
               + jnp.dot(s.astype(BF16), v, preferred_element_type=F32))
        qn = jnp.sum(q.astype(F32) * n_prev, axis=-1, keepdims=True)
        den = w_inter * qn + jnp.sum(s, axis=-1, keepdims=True)
        hh = num / jnp.maximum(jnp.abs(den), jnp.exp(-m_t))

        m_new = m_t[chunk - 1:chunk, :]
        b_last = b_c[chunk - 1:chunk, :]
        a_prev = jnp.exp(b_last + m_prev - m_new)
        a_c = jnp.exp(b_last - b_c + ig_c - m_new)
        a_r = jnp.exp(b_last - b_r + ig_r - m_new)
        kf = k.astype(F32)
        ka = (kf * a_c).astype(BF16)
        c_sc[h] = a_prev * c_prev + lax.dot_general(ka, v, TN_DIMS, preferred_element_type=F32)
        a_rows = jnp.broadcast_to(a_r, (8, chunk))
        n_sc[h:h + 1, :] = a_prev * n_prev + jnp.dot(
            a_rows, kf, preferred_element_type=F32, precision=HIGHEST)[0:1, :]
        m_sc[h:h + 1, :] = jnp.broadcast_to(m_new, (1, m_sc.shape[1]))

        hm_ref[:, cols] = _mlstm_post(
            hh, mo_ref[:, cols].astype(F32), mz_ref[:, cols].astype(F32), nw_ref[...]).astype(BF16)

    @pl.when(c == pl.num_programs(1) - 1)
    def _():
        c_out[0] = c_sc[...]
        n_out[0] = n_sc[...]
        m_out[0] = m_sc[...]


def _mlstm(r, g, gt, bg_row, bg_col, mh_norm_w, batch, seq, chunk):
    t_rows = r.shape[0]
    d = ML_HEADS * ML_HD
    nc = seq // chunk
    sec = lambda j: (lambda b, c: (b * nc + c, j))
    fixed = lambda b, c: (0, 0)
    state = lambda b, c: (b, 0, 0)
    return pl.pallas_call(
        _mlstm_kernel,
        grid=(batch, nc),
        in_specs=[
            pl.BlockSpec((chunk, d), sec(SEC_MQ)),
            pl.BlockSpec((chunk, d), sec(SEC_MK)),
            pl.BlockSpec((chunk, d), sec(SEC_MV)),
            pl.BlockSpec((chunk, d), sec(SEC_MO)),
            pl.BlockSpec((chunk, d), sec(SEC_MZ)),
            pl.BlockSpec((chunk, GATE_LANES), lambda b, c: (b * nc + c, 0)),
            pl.BlockSpec((GATE_LANES, chunk), lambda b, c: (0, b * nc + c)),
            pl.BlockSpec((1, GATE_LANES), fixed),
            pl.BlockSpec((GATE_LANES, 1), fixed),
            pl.BlockSpec((1, ML_HD), fixed),
        ],
        out_specs=[
            pl.BlockSpec((chunk, d), lambda b, c: (b * nc + c, 0)),
            pl.BlockSpec((1, ML_HEADS, ML_HD, ML_HD), lambda b, c: (b, 0, 0, 0)),
            pl.BlockSpec((1, ML_HEADS, ML_HD), state),
            pl.BlockSpec((1, ML_HEADS, GATE_LANES), state),
        ],
        out_shape=[
            jax.ShapeDtypeStruct((t_rows, d), BF16),
            jax.ShapeDtypeStruct((batch, ML_HEADS, ML_HD, ML_HD), F32),
            jax.ShapeDtypeStruct((batch, ML_HEADS, ML_HD), F32),
            jax.ShapeDtypeStruct((batch, ML_HEADS, GATE_LANES), F32),
        ],
        scratch_shapes=[
            pltpu.VMEM((ML_HEADS, ML_HD, ML_HD), F32),
            pltpu.VMEM((ML_HEADS, ML_HD), F32),
            pltpu.VMEM((ML_HEADS, GATE_LANES), F32),
        ],
        compiler_params=_params("arbitrary", "arbitrary"),
        name="mlstm_prompt",
    )(r, r, r, r, r, g, gt, bg_row, bg_col, mh_norm_w)


def _sstep_kernel(q_ref, k_ref, v_ref, mo_ref, mz_ref, g_ref, bgr_ref, nw_ref, c_ref, n_ref, m_ref,
                  hm_ref, c_out, n_out, m_out):
    gates = g_ref[0] + bgr_ref[...]
    for h in range(ML_HEADS):
        cols = slice(h * ML_HD, (h + 1) * ML_HD)
        q = q_ref[0, :, cols].astype(F32)
        k = k_ref[0, :, cols].astype(F32)
        v = v_ref[0, :, cols].astype(F32)
        ig = gates[:, h:h + 1]
        lf = _log_sigmoid(gates[:, ML_HEADS + h:ML_HEADS + h + 1])
        m_prev = m_ref[0, :, h:h + 1]
        c_prev = c_ref[0, h]
        n_prev = n_ref[0, h:h + 1, :]

        log_inter = lf + m_prev
        m_t = jnp.maximum(log_inter, ig)
        w_intra = jnp.exp(ig - m_t)
        w_inter = jnp.exp(log_inter - m_t)
        s = jnp.sum(q * k, axis=-1, keepdims=True) * w_intra
        q8 = jnp.broadcast_to(q, (8, ML_HD))
        qc = jnp.dot(q8, c_prev, preferred_element_type=F32, precision=HIGHEST)[0:1, :]
        num = w_inter * qc + s * v
        den = w_inter * jnp.sum(q * n_prev, axis=-1, keepdims=True) + s
        hh = num / jnp.maximum(jnp.abs(den), jnp.exp(-m_t))

        sub = lax.broadcasted_iota(jnp.int32, (8, ML_HD), 0)
        k8 = jnp.where(sub == 0, jnp.broadcast_to(k, (8, ML_HD)), 0.0)
        v8 = jnp.broadcast_to(v, (8, ML_HD))
        kv = lax.dot_general(k8, v8, TN_DIMS, preferred_element_type=F32, precision=HIGHEST)
        c_out[0, h] = w_inter * c_prev + w_intra * kv
        n_out[0, h:h + 1, :] = w_inter * n_prev + w_intra * k
        m_out[0, :, h:h + 1] = m_t

        hm_ref[0, :, cols] = _mlstm_post(
            hh, mo_ref[0, :, cols].astype(F32), mz_ref[0, :, cols].astype(F32), nw_ref[...]).astype(BF16)


def _sstep(r3, g3, bg_row, mh_norm_w, c_state, n_state, m_state3):
    nb = r3.shape[0]
    d = ML_HEADS * ML_HD
    sec = lambda j: (lambda b: (b, 0, j))
    fixed = lambda b: (0, 0)
    row3 = lambda b: (b, 0, 0)
    return pl.pallas_call(
        _sstep_kernel,
        grid=(nb,),
        in_specs=[
            pl.BlockSpec((1, 1, d), sec(SEC_MQ)),
            pl.BlockSpec((1, 1, d), sec(SEC_MK)),
            pl.BlockSpec((1, 1, d), sec(SEC_MV)),
            pl.BlockSpec((1, 1, d), sec(SEC_MO)),
            pl.BlockSpec((1, 1, d), sec(SEC_MZ)),
            pl.BlockSpec((1, 1, GATE_LANES), row3),
            pl.BlockSpec((1, GATE_LANES), fixed),
            pl.BlockSpec((1, ML_HD), fixed),
            pl.BlockSpec((1, ML_HEADS, ML_HD, ML_HD), lambda b: (b, 0, 0, 0)),
            pl.BlockSpec((1, ML_HEADS, ML_HD), row3),
            pl.BlockSpec((1, 1, ML_HEADS), row3),
        ],
        out_specs=[
            pl.BlockSpec((1, 1, d), row3),
            pl.BlockSpec((1, ML_HEADS, ML_HD, ML_HD), lambda b: (b, 0, 0, 0)),
            pl.BlockSpec((1, ML_HEADS, ML_HD), row3),
            pl.BlockSpec((1, 1, ML_HEADS), row3),
        ],
        out_shape=[
            jax.ShapeDtypeStruct((nb, 1, d), BF16),
            jax.ShapeDtypeStruct(c_state.shape, F32),
            jax.ShapeDtypeStruct(n_state.shape, F32),
            jax.ShapeDtypeStruct(m_state3.shape, F32),
        ],
        compiler_params=_params("arbitrary"),
        name="mlstm_sample",
    )(r3, r3, r3, r3, r3, g3, bg_row, mh_norm_w, c_state, n_state, m_state3)


def _sgate_kernel(pt_ref, q_ref, *refs, pages_per_step):
    del pt_ref
    page_refs = refs[:pages_per_step]
    sel_ref, kmean_sc = refs[pages_per_step], refs[pages_per_step + 1]
    s = pl.program_id(1)
    ppb = MOBA_BLOCK // PAGE_SIZE
    blocks_per_step = pages_per_step // ppb
    for j in range(blocks_per_step):
        tot = jnp.sum(page_refs[ppb * j][0], axis=0)
        for p in range(1, ppb):
            tot = tot + jnp.sum(page_refs[ppb * j + p][0], axis=0)
        kmean_sc[s * blocks_per_step + j] = tot * (1.0 / MOBA_BLOCK)

    @pl.when(s == pl.num_programs(1) - 1)
    def _():
        n_blocks = kmean_sc.shape[0]
        gate = jnp.sum(kmean_sc[...] * q_ref[...], axis=-1, keepdims=True)
        row = lax.broadcasted_iota(jnp.int32, gate.shape, 0).astype(F32)
        lane = lax.broadcasted_iota(jnp.int32, (ATT_HEADS, GATE_LANES), 1)
        picked = jnp.zeros((ATT_HEADS, GATE_LANES), F32)
        for j in range(MOBA_TOPK):
            best = jnp.max(gate, axis=0, keepdims=True)
            idx = jnp.min(jnp.where(gate == best, row, float(n_blocks)), axis=0, keepdims=True)
            picked = jnp.where(lane == j, idx[0], picked)
            gate = jnp.where(row == idx, NEG_INF, gate)
        sel_ref[0] = picked.astype(jnp.int32)


def _sgate(page_table, q_heads, cache_k, pages_per_step):
    nb, n_pages = page_table.shape
    n_steps = n_pages // pages_per_step
    n_blocks = n_pages * PAGE_SIZE // MOBA_BLOCK

    def page_spec(p):
        return pl.BlockSpec((1, PAGE_SIZE, ATT_HEADS, ATT_HD),
                            lambda b, s, pt: (pt[b, s * pages_per_step + p], 0, 0, 0))

    return pl.pallas_call(
        functools.partial(_sgate_kernel, pages_per_step=pages_per_step),
        grid_spec=pltpu.PrefetchScalarGridSpec(
            num_scalar_prefetch=1,
            grid=(nb, n_steps),
            in_specs=[pl.BlockSpec((1, ATT_HEADS, ATT_HD), lambda b, s, pt: (b, 0, 0))]
            + [page_spec(p) for p in range(pages_per_step)],
            out_specs=pl.BlockSpec((1, ATT_HEADS, GATE_LANES), lambda b, s, pt: (b, 0, 0)),
            scratch_shapes=[pltpu.VMEM((n_blocks, ATT_HEADS, ATT_HD), F32)],
        ),
        out_shape=jax.ShapeDtypeStruct((nb, ATT_HEADS, GATE_LANES), jnp.int32),
        compiler_params=_params("arbitrary", "arbitrary"),
        name="moba_sample_gate",
    )(page_table, q_heads, *([cache_k] * pages_per_step))


def _sattn_copies(pt_ref, sel_ref, ck_ref, cv_ref, kbuf, vbuf, sems, b, slot):
    ppb = MOBA_BLOCK // PAGE_SIZE
    copies = []
    for h in range(ATT_HEADS):
        for j in range(MOBA_TOPK):
            block = sel_ref[b, h * MOBA_TOPK + j]
            for p in range(ppb):
                page = pt_ref[b, block * ppb + p]
                rows = pl.ds((j * ppb + p) * PAGE_SIZE, PAGE_SIZE)
                copies.append(pltpu.make_async_copy(ck_ref.at[page, :, h, :], kbuf.at[slot, h, rows, :], sems.at[0, slot]))
                copies.append(pltpu.make_async_copy(cv_ref.at[page, :, h, :], vbuf.at[slot, h, rows, :], sems.at[1, slot]))
    return copies


def _sattn_kernel(pt_ref, sel_ref, q_ref, kn_ref, vn_ref, az_ref, ck_ref, cv_ref, o_ref, kbuf, vbuf, sems):
    b = pl.program_id(0)
    slot = b % 2
    fetch = functools.partial(_sattn_copies, pt_ref, sel_ref, ck_ref, cv_ref, kbuf, vbuf, sems)

    @pl.when(b == 0)
    def _():
        for cp in fetch(0, 0):
            cp.start()

    @pl.when(b + 1 < pl.num_programs(0))
    def _():
        for cp in fetch(b + 1, 1 - slot):
            cp.start()

    for cp in fetch(b, slot):
        cp.wait()

    qs = q_ref[0] * (ATT_HD ** -0.5)
    qb = qs.astype(BF16)
    s_new = jnp.sum(qs * kn_ref[0], axis=-1, keepdims=True)
    vn = vn_ref[0]
    sub = lax.broadcasted_iota(jnp.int32, (ATT_HEADS, ATT_HD), 0)
    out = jnp.zeros((ATT_HEADS, ATT_HD), F32)
    for h in range(ATT_HEADS):
        s = lax.dot_general(qb, kbuf[slot, h].astype(BF16), NT_DIMS, preferred_element_type=F32)[h:h + 1, :]
        sn = s_new[h:h + 1, :]
        m = jnp.maximum(sn, jnp.max(s, axis=-1, keepdims=True))
        p = jnp.exp(s - m)
        pn = jnp.exp(sn - m)
        l = pn + jnp.sum(p, axis=-1, keepdims=True)
        p8 = jnp.broadcast_to(p, (8, p.shape[1])).astype(BF16)
        acc = pn * vn[h:h + 1, :] + jnp.dot(p8, vbuf[slot, h].astype(BF16), preferred_element_type=F32)[0:1, :]
        out = jnp.where(sub == h, acc / l, out)
    o_ref[0] = (out * _silu(az_ref[0].astype(F32))).astype(BF16)


def _sattn(page_table, sel, q_heads, k_heads, v_heads, az_heads, cache_k, cache_v):
    nb = page_table.shape[0]
    n_keys = MOBA_TOPK * MOBA_BLOCK
    heads = lambda b, pt, sl: (b, 0, 0)
    head_spec = pl.BlockSpec((1, ATT_HEADS, ATT_HD), heads)
    return pl.pallas_call(
        _sattn_kernel,
        grid_spec=pltpu.PrefetchScalarGridSpec(
            num_scalar_prefetch=2,
            grid=(nb,),
            in_specs=[head_spec, head_spec, head_spec, head_spec,
                      pl.BlockSpec(memory_space=pl.ANY), pl.BlockSpec(memory_space=pl.ANY)],
            out_specs=head_spec,
            scratch_shapes=[
                pltpu.VMEM((2, ATT_HEADS, n_keys, ATT_HD), F32),
                pltpu.VMEM((2, ATT_HEADS, n_keys, ATT_HD), F32),
                pltpu.SemaphoreType.DMA((2, 2)),
            ],
        ),
        out_shape=jax.ShapeDtypeStruct((nb, ATT_HEADS, ATT_HD), BF16),
        compiler_params=_params("arbitrary"),
        name="moba_sample_attend",
    )(page_table, sel, q_heads, k_heads, v_heads, az_heads, cache_k, cache_v)


def _merge_kernel(x_ref, am_ref, hm_ref, ga_ref, gm_ref, wa_ref, wm_ref, wo_ref, y_ref):
    y_att = jnp.dot(am_ref[...], wa_ref[...], preferred_element_type=F32)
    y_ml = jnp.dot(hm_ref[...], wm_ref[...], preferred_element_type=F32)
    mix = _sigmoid(ga_ref[...].astype(F32)) * y_att + _sigmoid(gm_ref[...].astype(F32)) * y_ml
    y_ref[...] = x_ref[...] + jnp.dot(mix.astype(BF16), wo_ref[...], preferred_element_type=F32)


def _merge(x, am, hm, r, wa, wm, wo, tm):
    t_rows, d = x.shape
    row = lambda i: (i, 0)
    fixed = lambda i: (0, 0)
    return pl.pallas_call(
        _merge_kernel,
        grid=(t_rows // tm,),
        in_specs=[
            pl.BlockSpec((tm, d), row),
            pl.BlockSpec((tm, d), row),
            pl.BlockSpec((tm, d), row),
            pl.BlockSpec((tm, d), lambda i: (i, SEC_GA)),
            pl.BlockSpec((tm, d), lambda i: (i, SEC_GM)),
            pl.BlockSpec((d, d), fixed),
            pl.BlockSpec((d, d), fixed),
            pl.BlockSpec((d, d), fixed),
        ],
        out_specs=pl.BlockSpec((tm, d), row),
        out_shape=jax.ShapeDtypeStruct((t_rows, d), F32),
        compiler_params=_params("arbitrary"),
        name="merge",
    )(x, am, hm, r, r, wa, wm, wo)


def _row_tile(rows, target):
    tm = min(rows, target)
    while rows % tm:
        tm //= 2
    return tm


def kernel(x_prompt, x_sample, cache_k, cache_v, page_table, state_mlstm_C, state_mlstm_n, state_mlstm_m,
           norm_w, w_in, b_gates, q_norm_w, k_norm_w, mh_norm_w, w_proj_attn, w_proj_mlstm, w_out):
    batch, seq, d = x_prompt.shape
    dec_batch, dec_seq, _ = x_sample.shape
    assert dec_seq == 1 and d == ATT_HEADS * ATT_HD == ML_HEADS * ML_HD
    depth = w_in.shape[0]
    n_pages = page_table.shape[1]
    past = n_pages * PAGE_SIZE
    assert past % MOBA_BLOCK == 0 and seq % MOBA_BLOCK == 0
    t_p, t_s = batch * seq, dec_batch * dec_seq

    tm_proj = _row_tile(seq, 256)
    rope_p = _rope_tables(jnp.arange(seq, dtype=jnp.int32))
    rope_s = _rope_tables(jnp.tile(past + jnp.arange(dec_seq, dtype=jnp.int32), dec_batch))

    y_p = x_prompt.reshape(t_p, d)
    y_s = x_sample.reshape(t_s, d)
    outs = [[] for _ in range(10)]
    for l in range(depth):
        w_wide = w_in[l][:, :N_WIDE * d].astype(BF16)
        w_gate = jnp.pad(w_in[l][:, N_WIDE * d:], ((0, 0), (0, GATE_LANES - 2 * ML_HEADS)))
        w_gate_t = w_gate.T
        bg_row = jnp.pad(b_gates[l].astype(F32), (0, GATE_LANES - 2 * ML_HEADS)).reshape(1, GATE_LANES)
        bg_col = bg_row.reshape(GATE_LANES, 1)
        nw = norm_w[l].reshape(1, d)
        qnw, knw = q_norm_w[l].reshape(1, ATT_HD), k_norm_w[l].reshape(1, ATT_HD)
        mhw = mh_norm_w[l].reshape(1, ML_HD)
        wa, wm, wo = w_proj_attn[l].astype(BF16), w_proj_mlstm[l].astype(BF16), w_out[l].astype(BF16)

        q, k, v, r, g, gt = _inproj(y_p, nw, w_wide, w_gate, w_gate_t, rope_p, qnw, knw, tm_proj, seq // tm_proj)
        am = _moba(q, k, v, r, batch, seq)
        hm, c_new, n_new, m_new = _mlstm(r, g, gt, bg_row, bg_col, mhw, batch, seq, _row_tile(seq, 256))
        y_p = _merge(y_p, am, hm, r, wa, wm, wo, _row_tile(t_p, 512))
        for dst, val in zip(outs[:5], (k.reshape(batch, seq, ATT_HEADS, ATT_HD), v.reshape(batch, seq, ATT_HEADS, ATT_HD),
                                       c_new, n_new, m_new[:, :, 0])):
            dst.append(val)

        q, k, v, r, g, _ = _inproj(y_s, nw, w_wide, w_gate, w_gate_t, rope_s, qnw, knw, t_s, 1)
        r3 = r.reshape(t_s, 1, r.shape[-1])
        by_head = lambda t: t.reshape(t_s, ATT_HEADS, ATT_HD)
        sel = _sgate(page_table, by_head(q), cache_k[l], 8)
        sel = sel[:, :, :MOBA_TOPK].reshape(t_s, ATT_HEADS * MOBA_TOPK)
        am = _sattn(page_table, sel, by_head(q), by_head(k), by_head(v),
                    by_head(r[:, SEC_AZ * d:(SEC_AZ + 1) * d]), cache_k[l], cache_v[l])
        hm, c_new, n_new, m_new = _sstep(
            r3, g.reshape(t_s, 1, GATE_LANES), bg_row, mhw,
            state_mlstm_C[l], state_mlstm_n[l], state_mlstm_m[l].reshape(dec_batch, 1, ML_HEADS))
        y_s = _merge(y_s, am.reshape(t_s, d), hm.reshape(t_s, d), r, wa, wm, wo, t_s)
        for dst, val in zip(outs[5:], (k.reshape(dec_batch, dec_seq, ATT_HEADS, ATT_HD),
                                       v.reshape(dec_batch, dec_seq, ATT_HEADS, ATT_HD),
                                       c_new, n_new, m_new.reshape(dec_batch, ML_HEADS))):
            dst.append(val)

    st = state_mlstm_C.dtype
    k_p, v_p, c_p, n_p, m_p, k_s, v_s, c_s, n_s, m_s = (jnp.stack(o) for o in outs)
    return (y_p.reshape(batch, seq, d), y_s.reshape(dec_batch, dec_seq, d),
            k_p, v_p, c_p.astype(st), n_p.astype(st), m_p.astype(st),
            k_s, v_s, c_s.astype(st), n_s.astype(st), m_s.astype(st))
```

```python
import functools

import jax
import jax.numpy as jnp
from jax import lax
from jax.experimental import pallas as pl
from jax.experimental.pallas import tpu as pltpu

F32 = jnp.float32
BF16 = jnp.bfloat16
HIGHEST = lax.Precision.HIGHEST
NEG_INF = float("-inf")

ATT_HEADS = 8
ATT_HD = 128
ROT_DIM = ATT_HD // 4
ROPE_THETA = 500000.0
MOBA_BLOCK = 256
MOBA_TOPK = 3
ML_HEADS = 4
ML_HD = 256
PAGE_SIZE = 128
EPS = 1e-6
N_WIDE = 11
N_BF16_SECTIONS = 8
SEC_AZ, SEC_MQ, SEC_MK, SEC_MV, SEC_MO, SEC_MZ, SEC_GA, SEC_GM = range(8)
GATE_LANES = 128

VMEM_LIMIT_BYTES = 56 * 1024 * 1024

NT_DIMS = (((1,), (1,)), ((), ()))
TN_DIMS = (((0,), (0,)), ((), ()))


def _sigmoid(x):
    return 1.0 / (1.0 + jnp.exp(-x))


def _silu(x):
    return x * _sigmoid(x)


def _log_sigmoid(x):
    return jnp.minimum(x, 0.0) - jnp.log1p(jnp.exp(-jnp.abs(x)))


def _params(*semantics):
    return pltpu.CompilerParams(dimension_semantics=semantics, vmem_limit_bytes=VMEM_LIMIT_BYTES)


def _head_norm_rope(t, w, rc, rs1, rs2):
    half = ROT_DIM // 2
    outs = []
    for h in range(ATT_HEADS):
        th = t[:, h * ATT_HD:(h + 1) * ATT_HD]
        y = th * lax.rsqrt(jnp.mean(th * th, axis=-1, keepdims=True) + EPS) * w
        up = pltpu.roll(y, ATT_HD - half, axis=1)
        down = pltpu.roll(y, half, axis=1)
        outs.append(y * rc + up * rs1 + down * rs2)
    return outs


def _inproj_kernel(x_ref, nw_ref, w_ref, wg_ref, wgt_ref, rc_ref, rs1_ref, rs2_ref, qnw_ref, knw_ref,
                   q_ref, k_ref, v_ref, r_ref, g_ref, gt_ref):
    d = x_ref.shape[1]
    x = x_ref[...]
    xn = x * lax.rsqrt(jnp.mean(x * x, axis=-1, keepdims=True) + EPS) * nw_ref[...]
    xb = xn.astype(BF16)

    def section(i):
        return jnp.dot(xb, w_ref[:, i * d:(i + 1) * d], preferred_element_type=F32)

    rc, rs1, rs2 = rc_ref[...], rs1_ref[...], rs2_ref[...]
    for h, t in enumerate(_head_norm_rope(section(0), qnw_ref[...], rc, rs1, rs2)):
        q_ref[:, h * ATT_HD:(h + 1) * ATT_HD] = t
    for h, t in enumerate(_head_norm_rope(section(1), knw_ref[...], rc, rs1, rs2)):
        k_ref[:, h * ATT_HD:(h + 1) * ATT_HD] = t
    v_ref[...] = section(2)
    for j in range(N_BF16_SECTIONS):
        t = section(3 + j)
        if j == SEC_MK:
            t = t * (ML_HD ** -0.5)
        r_ref[:, j * d:(j + 1) * d] = t.astype(BF16)
    g_ref[...] = jnp.dot(xn, wg_ref[...], preferred_element_type=F32, precision=HIGHEST)
    gt_ref[...] = lax.dot_general(wgt_ref[...], xn, NT_DIMS, preferred_element_type=F32, precision=HIGHEST)


def _inproj(x, norm_w, w_wide, w_gate, w_gate_t, rope, q_norm_w, k_norm_w, tm, n_pos_blocks):
    t_rows, d = x.shape
    rc, rs1, rs2 = rope
    row = lambda i: (i, 0)
    fixed = lambda i: (0, 0)
    pos = lambda i: (i % n_pos_blocks, 0)
    once = pl.Buffered(1)
    return pl.pallas_call(
        _inproj_kernel,
        grid=(t_rows // tm,),
        in_specs=[
            pl.BlockSpec((tm, d), row),
            pl.BlockSpec((1, d), fixed),
            pl.BlockSpec((d, N_WIDE * d), fixed, pipeline_mode=once),
            pl.BlockSpec((d, GATE_LANES), fixed, pipeline_mode=once),
            pl.BlockSpec((GATE_LANES, d), fixed, pipeline_mode=once),
            pl.BlockSpec((tm, ATT_HD), pos),
            pl.BlockSpec((tm, ATT_HD), pos),
            pl.BlockSpec((tm, ATT_HD), pos),
            pl.BlockSpec((1, ATT_HD), fixed),
            pl.BlockSpec((1, ATT_HD), fixed),
        ],
        out_specs=[
            pl.BlockSpec((tm, d), row),
            pl.BlockSpec((tm, d), row),
            pl.BlockSpec((tm, d), row),
            pl.BlockSpec((tm, N_BF16_SECTIONS * d), row),
            pl.BlockSpec((tm, GATE_LANES), row),
            pl.BlockSpec((GATE_LANES, tm), lambda i: (0, i)),
        ],
        out_shape=[
            jax.ShapeDtypeStruct((t_rows, d), F32),
            jax.ShapeDtypeStruct((t_rows, d), F32),
            jax.ShapeDtypeStruct((t_rows, d), F32),
            jax.ShapeDtypeStruct((t_rows, N_BF16_SECTIONS * d), BF16),
            jax.ShapeDtypeStruct((t_rows, GATE_LANES), F32),
            jax.ShapeDtypeStruct((GATE_LANES, t_rows), F32),
        ],
        compiler_params=_params("arbitrary"),
        name="inproj",
    )(x, norm_w, w_wide, w_gate, w_gate_t, rc, rs1, rs2, q_norm_w, k_norm_w)


def _rope_tables(pos):
    half = ROT_DIM // 2
    inv = ROPE_THETA ** (-(jnp.arange(half, dtype=F32) * 2.0) / ROT_DIM)
    ang = pos.astype(F32)[:, None] * inv[None, :]
    cos, sin = jnp.cos(ang), jnp.sin(ang)
    n = pos.shape[0]
    zeros = jnp.zeros((n, half), F32)
    tail0 = jnp.zeros((n, ATT_HD - ROT_DIM), F32)
    rc = jnp.concatenate([cos, cos, jnp.ones((n, ATT_HD - ROT_DIM), F32)], axis=-1)
    rs1 = jnp.concatenate([-sin, zeros, tail0], axis=-1)
    rs2 = jnp.concatenate([zeros, sin, tail0], axis=-1)
    return rc, rs1, rs2


def _moba_kernel(q_ref, k_ref, v_ref, az_ref, o_ref, kmean_sc, kb_sc, vt_sc, s_sc, p_sc):
    blk = MOBA_BLOCK
    nb = k_ref.shape[0] // blk
    rows = lambda j: slice(j * blk, (j + 1) * blk)

    for j in range(nb):
        kj = k_ref[rows(j), :]
        kmean_sc[j:j + 1, :] = jnp.sum(kj, axis=0, keepdims=True) * (1.0 / blk)
        kb_sc[rows(j), :] = kj.astype(BF16)
        vt_sc[:, rows(j)] = v_ref[rows(j), :].T.astype(BF16)
    kmean = kmean_sc[...]

    blk_id = lax.broadcasted_iota(jnp.int32, (nb, blk), 0)
    key_ix = lax.broadcasted_iota(jnp.int32, (blk, blk), 0)
    qry_ix = lax.broadcasted_iota(jnp.int32, (blk, blk), 1)
    causal = key_ix <= qry_ix

    for c in range(nb):
        q = q_ref[rows(c), :]
        gate = lax.dot_general(kmean, q, NT_DIMS, preferred_element_type=F32, precision=HIGHEST)
        past = blk_id < c
        gate = jnp.where(past, gate, NEG_INF)
        beaten = jnp.zeros(gate.shape, F32)
        for m in range(c):
            gm = gate[m:m + 1, :]
            wins = jnp.where(gm > gate, 1.0, jnp.where(gm == gate, jnp.where(blk_id > m, 1.0, 0.0), 0.0))
            beaten = beaten + wins
        bias = jnp.where(past, jnp.where(beaten < MOBA_TOPK, 0.0, NEG_INF), NEG_INF)

        qs = (q * (ATT_HD ** -0.5)).astype(BF16)
        buf = c % 2
        m_col = None
        for j in range(c + 1):
            s = lax.dot_general(kb_sc[rows(j), :], qs, NT_DIMS, preferred_element_type=F32)
            s = jnp.where(causal, s, NEG_INF) if j == c else s + bias[j:j + 1, :]
            s_sc[buf, rows(j), :] = s
            s_max = jnp.max(s, axis=0, keepdims=True)
            m_col = s_max if m_col is None else jnp.maximum(m_col, s_max)
        l_col = jnp.zeros((1, blk), F32)
        for j in range(c + 1):
            p = jnp.exp(s_sc[buf, rows(j), :] - m_col)
            l_col = l_col + jnp.sum(p, axis=0, keepdims=True)
            p_sc[buf, rows(j), :] = p.astype(BF16)
        n_keys = (c + 1) * blk
        acc = jnp.dot(vt_sc[:, :n_keys], p_sc[buf, :n_keys, :], preferred_element_type=F32)
        out = (acc / l_col).T
        o_ref[rows(c), :] = (out * _silu(az_ref[rows(c), :].astype(F32))).astype(BF16)


def _moba(q, k, v, r, batch, seq):
    t_rows, d = q.shape
    blk = MOBA_BLOCK
    nb = seq // blk
    head = lambda b, h: (b, h)
    return pl.pallas_call(
        _moba_kernel,
        grid=(batch, ATT_HEADS),
        in_specs=[
            pl.BlockSpec((seq, ATT_HD), head),
            pl.BlockSpec((seq, ATT_HD), head),
            pl.BlockSpec((seq, ATT_HD), head),
            pl.BlockSpec((seq, ATT_HD), lambda b, h: (b, SEC_AZ * ATT_HEADS + h)),
        ],
        out_specs=pl.BlockSpec((seq, ATT_HD), head),
        out_shape=jax.ShapeDtypeStruct((t_rows, d), BF16),
        scratch_shapes=[
            pltpu.VMEM((nb, ATT_HD), F32),
            pltpu.VMEM((seq, ATT_HD), BF16),
            pltpu.VMEM((ATT_HD, seq), BF16),
            pltpu.VMEM((2, seq, blk), F32),
            pltpu.VMEM((2, seq, blk), BF16),
        ],
        compiler_params=_params("arbitrary", "arbitrary"),
        name="moba_prompt",
    )(q, k, v, r)


def _mlstm_post(h, mo, mz, nw):
    h = h * _sigmoid(mo)
    h = h * lax.rsqrt(jnp.mean(h * h, axis=-1, keepdims=True) + EPS) * nw
    return h * _silu(mz)


def _mlstm_kernel(q_ref, k_ref, v_ref, mo_ref, mz_ref, g_ref, gt_ref, bgr_ref, bgc_ref, nw_ref,
                  hm_ref, c_out, n_out, m_out, c_sc, n_sc, m_sc):
    c = pl.program_id(1)
    chunk = q_ref.shape[0]

    @pl.when(c == 0)
    def _():
        c_sc[...] = jnp.zeros(c_sc.shape, F32)
        n_sc[...] = jnp.zeros(n_sc.shape, F32)
        m_sc[...] = jnp.zeros(m_sc.shape, F32)

    gc = g_ref[...] + bgr_ref[...]
    lane = lax.broadcasted_iota(jnp.int32, gc.shape, 1)
    gc = jnp.where(lane >= ML_HEADS, _log_sigmoid(gc), gc)
    gr = gt_ref[0:2 * ML_HEADS, :] + bgc_ref[0:2 * ML_HEADS, :]
    sub = lax.broadcasted_iota(jnp.int32, gr.shape, 0)
    gr = jnp.where(sub >= ML_HEADS, _log_sigmoid(gr), gr)

    t_ix = lax.broadcasted_iota(jnp.int32, (chunk, chunk), 0)
    s_ix = lax.broadcasted_iota(jnp.int32, (chunk, chunk), 1)
    causal = s_ix <= t_ix
    tril = jnp.where(causal, 1.0, 0.0)
    bc = jnp.dot(tril, gc, preferred_element_type=F32, precision=HIGHEST)
    br = lax.dot_general(gr, tril, NT_DIMS, preferred_element_type=F32, precision=HIGHEST)

    for h in range(ML_HEADS):
        cols = slice(h * ML_HD, (h + 1) * ML_HD)
        q, k, v = q_ref[:, cols], k_ref[:, cols], v_ref[:, cols]
        ig_r, b_r = gr[h:h + 1, :], br[ML_HEADS + h:ML_HEADS + h + 1, :]
        ig_c, b_c = gc[:, h:h + 1], bc[:, ML_HEADS + h:ML_HEADS + h + 1]
        m_prev = m_sc[h:h + 1, 0:1]
        c_prev = c_sc[h]
        n_prev = n_sc[h:h + 1, :]

        log_w = jnp.where(causal, b_c - b_r + ig_r, NEG_INF)
        log_inter = b_c + m_prev
        m_t = jnp.maximum(log_inter, jnp.max(log_w, axis=-1, keepdims=True))
        w_intra = jnp.exp(log_w - m_t)
        w_inter = jnp.exp(log_inter - m_t)
        s = lax.dot_general(q, k, NT_DIMS, preferred_element_type=F32) * w_intra
        num = (w_inter * jnp.dot(q, c_prev.astype(BF16), preferred_element_type=F32)
               + jnp.dot(s.astype(BF16), v, preferred_element_type=F32))
        qn = jnp.sum(q.astype(F32) * n_prev, axis=-1, keepdims=True)
        den = w_inter * qn + jnp.sum(s, axis=-1, keepdims=True)
        hh = num / jnp.maximum(jnp.abs(den), jnp.exp(-m_t))

        m_new = m_t[chunk - 1:chunk, :]
        b_last = b_c[chunk - 1:chunk, :]
        a_prev = jnp.exp(b_last + m_prev - m_new)
        a_c = jnp.exp(b_last - b_c + ig_c - m_new)
        a_r = jnp.exp(b_last - b_r + ig_r - m_new)
        kf = k.astype(F32)
        ka = (kf * a_c).astype(BF16)
        c_sc[h] = a_prev * c_prev + lax.dot_general(ka, v, TN_DIMS, preferred_element_type=F32)
        a_rows = jnp.broadcast_to(a_r, (8, chunk))
        n_sc[h:h + 1, :] = a_prev * n_prev + jnp.dot(
            a_rows, kf, preferred_element_type=F32, precision=HIGHEST)[0:1, :]
        m_sc[h:h + 1, :] = jnp.broadcast_to(m_new, (1, m_sc.shape[1]))

        hm_ref[:, cols] = _mlstm_post(
            hh, mo_ref[:, cols].astype(F32), mz_ref[:, cols].astype(F32), nw_ref[...]).astype(BF16)

    @pl.when(c == pl.num_programs(1) - 1)
    def _():
        c_out[0] = c_sc[...]
        n_out[0] = n_sc[...]
        m_out[0] = m_sc[...]


def _mlstm(r, g, gt, bg_row, bg_col, mh_norm_w, batch, seq, chunk):
    t_rows = r.shape[0]
    d = ML_HEADS * ML_HD
    nc = seq // chunk
    sec = lambda j: (lambda b, c: (b * nc + c, j))
    fixed = lambda b, c: (0, 0)
    state = lambda b, c: (b, 0, 0)
    return pl.pallas_call(
        _mlstm_kernel,
        grid=(batch, nc),
        in_specs=[
            pl.BlockSpec((chunk, d), sec(SEC_MQ)),
            pl.BlockSpec((chunk, d), sec(SEC_MK)),
            pl.BlockSpec((chunk, d), sec(SEC_MV)),
            pl.BlockSpec((chunk, d), sec(SEC_MO)),
            pl.BlockSpec((chunk, d), sec(SEC_MZ)),
            pl.BlockSpec((chunk, GATE_LANES), lambda b, c: (b * nc + c, 0)),
            pl.BlockSpec((GATE_LANES, chunk), lambda b, c: (0, b * nc + c)),
            pl.BlockSpec((1, GATE_LANES), fixed),
            pl.BlockSpec((GATE_LANES, 1), fixed),
            pl.BlockSpec((1, ML_HD), fixed),
        ],
        out_specs=[
            pl.BlockSpec((chunk, d), lambda b, c: (b * nc + c, 0)),
            pl.BlockSpec((1, ML_HEADS, ML_HD, ML_HD), lambda b, c: (b, 0, 0, 0)),
            pl.BlockSpec((1, ML_HEADS, ML_HD), state),
            pl.BlockSpec((1, ML_HEADS, GATE_LANES), state),
        ],
        out_shape=[
            jax.ShapeDtypeStruct((t_rows, d), BF16),
            jax.ShapeDtypeStruct((batch, ML_HEADS, ML_HD, ML_HD), F32),
            jax.ShapeDtypeStruct((batch, ML_HEADS, ML_HD), F32),
            jax.ShapeDtypeStruct((batch, ML_HEADS, GATE_LANES), F32),
        ],
        scratch_shapes=[
            pltpu.VMEM((ML_HEADS, ML_HD, ML_HD), F32),
            pltpu.VMEM((ML_HEADS, ML_HD), F32),
            pltpu.VMEM((ML_HEADS, GATE_LANES), F32),
        ],
        compiler_params=_params("arbitrary", "arbitrary"),
        name="mlstm_prompt",
    )(r, r, r, r, r, g, gt, bg_row, bg_col, mh_norm_w)


def _sstep_kernel(q_ref, k_ref, v_ref, mo_ref, mz_ref, g_ref, bgr_ref, nw_ref, c_ref, n_ref, m_ref,
                  hm_ref, c_out, n_out, m_out):
    gates = g_ref[0] + bgr_ref[...]
    for h in range(ML_HEADS):
        cols = slice(h * ML_HD, (h + 1) * ML_HD)
        q = q_ref[0, :, cols].astype(F32)
        k = k_ref[0, :, cols].astype(F32)
        v = v_ref[0, :, cols].astype(F32)
        ig = gates[:, h:h + 1]
        lf = _log_sigmoid(gates[:, ML_HEADS + h:ML_HEADS + h + 1])
        m_prev = m_ref[0, :, h:h + 1]
        c_prev = c_ref[0, h]
        n_prev = n_ref[0, h:h + 1, :]

        log_inter = lf + m_prev
        m_t = jnp.maximum(log_inter, ig)
        w_intra = jnp.exp(ig - m_t)
        w_inter = jnp.exp(log_inter - m_t)
        s = jnp.sum(q * k, axis=-1, keepdims=True) * w_intra
        q8 = jnp.broadcast_to(q, (8, ML_HD))
        qc = jnp.dot(q8, c_prev, preferred_element_type=F32, precision=HIGHEST)[0:1, :]
        num = w_inter * qc + s * v
        den = w_inter * jnp.sum(q * n_prev, axis=-1, keepdims=True) + s
        hh = num / jnp.maximum(jnp.abs(den), jnp.exp(-m_t))

        sub = lax.broadcasted_iota(jnp.int32, (8, ML_HD), 0)
        k8 = jnp.where(sub == 0, jnp.broadcast_to(k, (8, ML_HD)), 0.0)
        v8 = jnp.broadcast_to(v, (8, ML_HD))
        kv = lax.dot_general(k8, v8, TN_DIMS, preferred_element_type=F32, precision=HIGHEST)
        c_out[0, h] = w_inter * c_prev + w_intra * kv
        n_out[0, h:h + 1, :] = w_inter * n_prev + w_intra * k
        m_out[0, :, h:h + 1] = m_t

        hm_ref[0, :, cols] = _mlstm_post(
            hh, mo_ref[0, :, cols].astype(F32), mz_ref[0, :, cols].astype(F32), nw_ref[...]).astype(BF16)


def _sstep(r3, g3, bg_row, mh_norm_w, c_state, n_state, m_state3):
    nb = r3.shape[0]
    d = ML_HEADS * ML_HD
    sec = lambda j: (lambda b: (b, 0, j))
    fixed = lambda b: (0, 0)
    row3 = lambda b: (b, 0, 0)
    return pl.pallas_call(
        _sstep_kernel,
        grid=(nb,),
        in_specs=[
            pl.BlockSpec((1, 1, d), sec(SEC_MQ)),
            pl.BlockSpec((1, 1, d), sec(SEC_MK)),
            pl.BlockSpec((1, 1, d), sec(SEC_MV)),
            pl.BlockSpec((1, 1, d), sec(SEC_MO)),
            pl.BlockSpec((1, 1, d), sec(SEC_MZ)),
            pl.BlockSpec((1, 1, GATE_LANES), row3),
            pl.BlockSpec((1, GATE_LANES), fixed),
            pl.BlockSpec((1, ML_HD), fixed),
            pl.BlockSpec((1, ML_HEADS, ML_HD, ML_HD), lambda b: (b, 0, 0, 0)),
            pl.BlockSpec((1, ML_HEADS, ML_HD), row3),
            pl.BlockSpec((1, 1, ML_HEADS), row3),
        ],
        out_specs=[
            pl.BlockSpec((1, 1, d), row3),
            pl.BlockSpec((1, ML_HEADS, ML_HD, ML_HD), lambda b: (b, 0, 0, 0)),
            pl.BlockSpec((1, ML_HEADS, ML_HD), row3),
            pl.BlockSpec((1, 1, ML_HEADS), row3),
        ],
        out_shape=[
            jax.ShapeDtypeStruct((nb, 1, d), BF16),
            jax.ShapeDtypeStruct(c_state.shape, F32),
            jax.ShapeDtypeStruct(n_state.shape, F32),
            jax.ShapeDtypeStruct(m_state3.shape, F32),
        ],
        compiler_params=_params("arbitrary"),
        name="mlstm_sample",
    )(r3, r3, r3, r3, r3, g3, bg_row, mh_norm_w, c_state, n_state, m_state3)


def _sgate_kernel(pt_ref, q_ref, *refs, pages_per_step):
    del pt_ref
    page_refs = refs[:pages_per_step]
    sel_ref, kmean_sc = refs[pages_per_step], refs[pages_per_step + 1]
    s = pl.program_id(1)
    ppb = MOBA_BLOCK // PAGE_SIZE
    blocks_per_step = pages_per_step // ppb
    for j in range(blocks_per_step):
        tot = jnp.sum(page_refs[ppb * j][0], axis=0)
        for p in range(1, ppb):
            tot = tot + jnp.sum(page_refs[ppb * j + p][0], axis=0)
        kmean_sc[s * blocks_per_step + j] = tot * (1.0 / MOBA_BLOCK)

    @pl.when(s == pl.num_programs(1) - 1)
    def _():
        n_blocks = kmean_sc.shape[0]
        gate = jnp.sum(kmean_sc[...] * q_ref[...], axis=-1, keepdims=True)
        row = lax.broadcasted_iota(jnp.int32, gate.shape, 0).astype(F32)
        lane = lax.broadcasted_iota(jnp.int32, (ATT_HEADS, GATE_LANES), 1)
        picked = jnp.zeros((ATT_HEADS, GATE_LANES), F32)
        for j in range(MOBA_TOPK):
            best = jnp.max(gate, axis=0, keepdims=True)
            idx = jnp.min(jnp.where(gate == best, row, float(n_blocks)), axis=0, keepdims=True)
            picked = jnp.where(lane == j, idx[0], picked)
            gate = jnp.where(row == idx, NEG_INF, gate)
        sel_ref[0] = picked.astype(jnp.int32)


def _sgate(page_table, q_heads, cache_k, pages_per_step):
    nb, n_pages = page_table.shape
    n_steps = n_pages // pages_per_step
    n_blocks = n_pages * PAGE_SIZE // MOBA_BLOCK

    def page_spec(p):
        return pl.BlockSpec((1, PAGE_SIZE, ATT_HEADS, ATT_HD),
                            lambda b, s, pt: (pt[b, s * pages_per_step + p], 0, 0, 0))

    return pl.pallas_call(
        functools.partial(_sgate_kernel, pages_per_step=pages_per_step),
        grid_spec=pltpu.PrefetchScalarGridSpec(
            num_scalar_prefetch=1,
            grid=(nb, n_steps),
            in_specs=[pl.BlockSpec((1, ATT_HEADS, ATT_HD), lambda b, s, pt: (b, 0, 0))]
            + [page_spec(p) for p in range(pages_per_step)],
            out_specs=pl.BlockSpec((1, ATT_HEADS, GATE_LANES), lambda b, s, pt: (b, 0, 0)),
            scratch_shapes=[pltpu.VMEM((n_blocks, ATT_HEADS, ATT_HD), F32)],
        ),
        out_shape=jax.ShapeDtypeStruct((nb, ATT_HEADS, GATE_LANES), jnp.int32),
        compiler_params=_params("arbitrary", "arbitrary"),
        name="moba_sample_gate",
    )(page_table, q_heads, *([cache_k] * pages_per_step))


def _sattn_copies(pt_ref, sel_ref, ck_ref, cv_ref, kbuf, vbuf, sems, b, slot):
    ppb = MOBA_BLOCK // PAGE_SIZE
    copies = []
    for h in range(ATT_HEADS):
        for j in range(MOBA_TOPK):
            block = sel_ref[b, h * MOBA_TOPK + j]
            for p in range(ppb):
                page = pt_ref[b, block * ppb + p]
                rows = pl.ds((j * ppb + p) * PAGE_SIZE, PAGE_SIZE)
                copies.append(pltpu.make_async_copy(ck_ref.at[page, :, h, :], kbuf.at[slot, h, rows, :], sems.at[0, slot]))
                copies.append(pltpu.make_async_copy(cv_ref.at[page, :, h, :], vbuf.at[slot, h, rows, :], sems.at[1, slot]))
    return copies


def _sattn_kernel(pt_ref, sel_ref, q_ref, kn_ref, vn_ref, az_ref, ck_ref, cv_ref, o_ref, kbuf, vbuf, sems):
    b = pl.program_id(0)
    slot = b % 2
    fetch = functools.partial(_sattn_copies, pt_ref, sel_ref, ck_ref, cv_ref, kbuf, vbuf, sems)

    @pl.when(b == 0)
    def _():
        for cp in fetch(0, 0):
            cp.start()

    @pl.when(b + 1 < pl.num_programs(0))
    def _():
        for cp in fetch(b + 1, 1 - slot):
            cp.start()

    for cp in fetch(b, slot):
        cp.wait()

    qs = q_ref[0] * (ATT_HD ** -0.5)
    qb = qs.astype(BF16)
    s_new = jnp.sum(qs * kn_ref[0], axis=-1, keepdims=True)
    vn = vn_ref[0]
    sub = lax.broadcasted_iota(jnp.int32, (ATT_HEADS, ATT_HD), 0)
    out = jnp.zeros((ATT_HEADS, ATT_HD), F32)
    for h in range(ATT_HEADS):
        s = lax.dot_general(qb, kbuf[slot, h].astype(BF16), NT_DIMS, preferred_element_type=F32)[h:h + 1, :]
        sn = s_new[h:h + 1, :]
        m = jnp.maximum(sn, jnp.max(s, axis=-1, keepdims=True))
        p = jnp.exp(s - m)
        pn = jnp.exp(sn - m)
        l = pn + jnp.sum(p, axis=-1, keepdims=True)
        p8 = jnp.broadcast_to(p, (8, p.shape[1])).astype(BF16)
        acc = pn * vn[h:h + 1, :] + jnp.dot(p8, vbuf[slot, h].astype(BF16), preferred_element_type=F32)[0:1, :]
        out = jnp.where(sub == h, acc / l, out)
    o_ref[0] = (out * _silu(az_ref[0].astype(F32))).astype(BF16)


def _sattn(page_table, sel, q_heads, k_heads, v_heads, az_heads, cache_k, cache_v):
    nb = page_table.shape[0]
    n_keys = MOBA_TOPK * MOBA_BLOCK
    heads = lambda b, pt, sl: (b, 0, 0)
    head_spec = pl.BlockSpec((1, ATT_HEADS, ATT_HD), heads)
    return pl.pallas_call(
        _sattn_kernel,
        grid_spec=pltpu.PrefetchScalarGridSpec(
            num_scalar_prefetch=2,
            grid=(nb,),
            in_specs=[head_spec, head_spec, head_spec, head_spec,
                      pl.BlockSpec(memory_space=pl.ANY), pl.BlockSpec(memory_space=pl.ANY)],
            out_specs=head_spec,
            scratch_shapes=[
                pltpu.VMEM((2, ATT_HEADS, n_keys, ATT_HD), F32),
                pltpu.VMEM((2, ATT_HEADS, n_keys, ATT_HD), F32),
                pltpu.SemaphoreType.DMA((2, 2)),
            ],
        ),
        out_shape=jax.ShapeDtypeStruct((nb, ATT_HEADS, ATT_HD), BF16),
        compiler_params=_params("arbitrary"),
        name="moba_sample_attend",
    )(page_table, sel, q_heads, k_heads, v_heads, az_heads, cache_k, cache_v)


def _merge_kernel(x_ref, am_ref, hm_ref, ga_ref, gm_ref, wa_ref, wm_ref, wo_ref, y_ref):
    y_att = jnp.dot(am_ref[...], wa_ref[...], preferred_element_type=F32)
    y_ml = jnp.dot(hm_ref[...], wm_ref[...], preferred_element_type=F32)
    mix = _sigmoid(ga_ref[...].astype(F32)) * y_att + _sigmoid(gm_ref[...].astype(F32)) * y_ml
    y_ref[...] = x_ref[...] + jnp.dot(mix.astype(BF16), wo_ref[...], preferred_element_type=F32)


def _merge(x, am, hm, r, wa, wm, wo, tm):
    t_rows, d = x.shape
    row = lambda i: (i, 0)
    fixed = lambda i: (0, 0)
    return pl.pallas_call(
        _merge_kernel,
        grid=(t_rows // tm,),
        in_specs=[
            pl.BlockSpec((tm, d), row),
            pl.BlockSpec((tm, d), row),
            pl.BlockSpec((tm, d), row),
            pl.BlockSpec((tm, d), lambda i: (i, SEC_GA)),
            pl.BlockSpec((tm, d), lambda i: (i, SEC_GM)),
            pl.BlockSpec((d, d), fixed),
            pl.BlockSpec((d, d), fixed),
            pl.BlockSpec((d, d), fixed),
        ],
        out_specs=pl.BlockSpec((tm, d), row),
        out_shape=jax.ShapeDtypeStruct((t_rows, d), F32),
        compiler_params=_params("arbitrary"),
        name="merge",
    )(x, am, hm, r, r, wa, wm, wo)


def _row_tile(rows, target):
    tm = min(rows, target)
    while rows % tm:
        tm //= 2
    return tm


def kernel(x_prompt, x_sample, cache_k, cache_v, page_table, state_mlstm_C, state_mlstm_n, state_mlstm_m,
           norm_w, w_in, b_gates, q_norm_w, k_norm_w, mh_norm_w, w_proj_attn, w_proj_mlstm, w_out):
    batch, seq, d = x_prompt.shape
    dec_batch, dec_seq, _ = x_sample.shape
    assert dec_seq == 1 and d == ATT_HEADS * ATT_HD == ML_HEADS * ML_HD
    depth = w_in.shape[0]
    n_pages = page_table.shape[1]
    past = n_pages * PAGE_SIZE
    assert past % MOBA_BLOCK == 0 and seq % MOBA_BLOCK == 0
    t_p, t_s = batch * seq, dec_batch * dec_seq

    tm_proj = _row_tile(seq, 256)
    rope_p = _rope_tables(jnp.arange(seq, dtype=jnp.int32))
    rope_s = _rope_tables(jnp.tile(past + jnp.arange(dec_seq, dtype=jnp.int32), dec_batch))

    y_p = x_prompt.reshape(t_p, d)
    y_s = x_sample.reshape(t_s, d)
    outs = [[] for _ in range(10)]
    for l in range(depth):
        w_wide = w_in[l][:, :N_WIDE * d].astype(BF16)
        w_gate = jnp.pad(w_in[l][:, N_WIDE * d:], ((0, 0), (0, GATE_LANES - 2 * ML_HEADS)))
        w_gate_t = w_gate.T
        bg_row = jnp.pad(b_gates[l].astype(F32), (0, GATE_LANES - 2 * ML_HEADS)).reshape(1, GATE_LANES)
        bg_col = bg_row.reshape(GATE_LANES, 1)
        nw = norm_w[l].reshape(1, d)
        qnw, knw = q_norm_w[l].reshape(1, ATT_HD), k_norm_w[l].reshape(1, ATT_HD)
        mhw = mh_norm_w[l].reshape(1, ML_HD)
        wa, wm, wo = w_proj_attn[l].astype(BF16), w_proj_mlstm[l].astype(BF16), w_out[l].astype(BF16)

        q, k, v, r, g, gt = _inproj(y_p, nw, w_wide, w_gate, w_gate_t, rope_p, qnw, knw, tm_proj, seq // tm_proj)
        am = _moba(q, k, v, r, batch, seq)
        hm, c_new, n_new, m_new = _mlstm(r, g, gt, bg_row, bg_col, mhw, batch, seq, _row_tile(seq, 256))
        y_p = _merge(y_p, am, hm, r, wa, wm, wo, _row_tile(t_p, 512))
        for dst, val in zip(outs[:5], (k.reshape(batch, seq, ATT_HEADS, ATT_HD), v.reshape(batch, seq, ATT_HEADS, ATT_HD),
                                       c_new, n_new, m_new[:, :, 0])):
            dst.append(val)

        q, k, v, r, g, _ = _inproj(y_s, nw, w_wide, w_gate, w_gate_t, rope_s, qnw, knw, t_s, 1)
        r3 = r.reshape(t_s, 1, r.shape[-1])
        by_head = lambda t: t.reshape(t_s, ATT_HEADS, ATT_HD)
        sel = _sgate(page_table, by_head(q), cache_k[l], 8)
        sel = sel[:, :, :MOBA_TOPK].reshape(t_s, ATT_HEADS * MOBA_TOPK)
        am = _sattn(page_table, sel, by_head(q), by_head(k), by_head(v),
                    by_head(r[:, SEC_AZ * d:(SEC_AZ + 1) * d]), cache_k[l], cache_v[l])
        hm, c_new, n_new, m_new = _sstep(
            r3, g.reshape(t_s, 1, GATE_LANES), bg_row, mhw,
            state_mlstm_C[l], state_mlstm_n[l], state_mlstm_m[l].reshape(dec_batch, 1, ML_HEADS))
        y_s = _merge(y_s, am.reshape(t_s, d), hm.reshape(t_s, d), r, wa, wm, wo, t_s)
        for dst, val in zip(outs[5:], (k.reshape(dec_batch, dec_seq, ATT_HEADS, ATT_HD),
                                       v.reshape(dec_batch, dec_seq, ATT_HEADS, ATT_HD),
                                       c_new, n_new, m_new.reshape(dec_batch, ML_HEADS))):
            dst.append(val)

    st = state_mlstm_C.dtype
    k_p, v_p, c_p, n_p, m_p, k_s, v_s, c_s, n_s, m_s = (jnp.stack(o) for o in outs)
    return (y_p.reshape(batch, seq, d), y_s.reshape(dec_batch, dec_seq, d),
            k_p, v_p, c_p.astype(st), n_p.astype(st), m_p.astype(st),
            k_s, v_s, c_s.astype(st), n_s.astype(st), m_s.astype(st))
```

```python
import functools

import jax
import jax.numpy as jnp
from jax import lax
from jax.experimental import pallas as pl
from jax.experimental.pallas import tpu as pltpu

F32 = jnp.float32
BF16 = jnp.bfloat16
HIGHEST = lax.Precision.HIGHEST
NEG_INF = float("-inf")

ATT_HEADS = 8
ATT_HD = 128
ROT_DIM = ATT_HD // 4
ROPE_THETA = 500000.0
MOBA_BLOCK = 256
MOBA_TOPK = 3
ML_HEADS = 4
ML_HD = 256
PAGE_SIZE = 128
EPS = 1e-6
N_WIDE = 11
N_BF16_SECTIONS = 8
SEC_AZ, SEC_MQ, SEC_MK, SEC_MV, SEC_MO, SEC_MZ, SEC_GA, SEC_GM = range(8)
GATE_LANES = 128

VMEM_LIMIT_BYTES = 56 * 1024 * 1024

NT_DIMS = (((1,), (1,)), ((), ()))
TN_DIMS = (((0,), (0,)), ((), ()))


def _sigmoid(x):
    return 1.0 / (1.0 + jnp.exp(-x))


def _silu(x):
    return x * _sigmoid(x)


def _log_sigmoid(x):
    return jnp.minimum(x, 0.0) - jnp.log1p(jnp.exp(-jnp.abs(x)))


def _params(*semantics):
    return pltpu.CompilerParams(dimension_semantics=semantics, vmem_limit_bytes=VMEM_LIMIT_BYTES)


def _head_norm_rope(t, w, rc, rs1, rs2):
    half = ROT_DIM // 2
    outs = []
    for h in range(ATT_HEADS):
        th = t[:, h * ATT_HD:(h + 1) * ATT_HD]
        y = th * lax.rsqrt(jnp.mean(th * th, axis=-1, keepdims=True) + EPS) * w
        up = pltpu.roll(y, ATT_HD - half, axis=1)
        down = pltpu.roll(y, half, axis=1)
        outs.append(y * rc + up * rs1 + down * rs2)
    return outs


def _cast_kernel(x_ref, o_ref):
    o_ref[...] = x_ref[...].astype(o_ref.dtype)


def _cast_wide_weight(w_in, layer, d):
    return pl.pallas_call(
        _cast_kernel,
        grid=(N_WIDE,),
        in_specs=[pl.BlockSpec((None, d, d), lambda j: (layer, 0, j))],
        out_specs=pl.BlockSpec((d, d), lambda j: (0, j)),
        out_shape=jax.ShapeDtypeStruct((d, N_WIDE * d), BF16),
        compiler_params=_params("arbitrary"),
        name="cast_weight",
    )(w_in)


def _page_specs(n_step, first_page, step_of):
    def spec(p):
        return pl.BlockSpec((1, PAGE_SIZE, ATT_HEADS, ATT_HD),
                            lambda *ids: (ids[-1][first_page + step_of(*ids[:-1]) * n_step + p], 0, 0, 0))
    return [spec(p) for p in range(n_step)]


def _block_means(page_refs, out_ref):
    ppb = MOBA_BLOCK // PAGE_SIZE
    for j in range(len(page_refs) // ppb):
        tot = jnp.sum(page_refs[ppb * j][0], axis=0)
        for p in range(1, ppb):
            tot = tot + jnp.sum(page_refs[ppb * j + p][0], axis=0)
        out_ref[j] = tot * (1.0 / MOBA_BLOCK)


def _means_out(n_step, n_steps, step_of):
    ppb = MOBA_BLOCK // PAGE_SIZE
    spec = pl.BlockSpec((n_step // ppb, ATT_HEADS, ATT_HD), lambda *ids: (step_of(*ids[:-1]), 0, 0))
    return spec, jax.ShapeDtypeStruct((n_steps * n_step // ppb, ATT_HEADS, ATT_HD), F32)


def _inproj_kernel(pt_ref, x_ref, nw_ref, w_ref, wg_ref, rc_ref, rs1_ref, rs2_ref, qnw_ref, knw_ref, *refs,
                   n_pages):
    del pt_ref
    q_ref, k_ref, v_ref, r_ref, g_ref, gt_ref = refs[n_pages:n_pages + 6]
    if n_pages:
        _block_means(refs[:n_pages], refs[n_pages + 6])
    d = x_ref.shape[1]
    x = x_ref[...]
    xn = x * lax.rsqrt(jnp.mean(x * x, axis=-1, keepdims=True) + EPS) * nw_ref[...]
    xb = xn.astype(BF16)

    def section(i):
        return jnp.dot(xb, w_ref[:, i * d:(i + 1) * d], preferred_element_type=F32)

    rc, rs1, rs2 = rc_ref[...], rs1_ref[...], rs2_ref[...]
    for h, t in enumerate(_head_norm_rope(section(0), qnw_ref[...], rc, rs1, rs2)):
        q_ref[:, h * ATT_HD:(h + 1) * ATT_HD] = t
    for h, t in enumerate(_head_norm_rope(section(1), knw_ref[...], rc, rs1, rs2)):
        k_ref[:, h * ATT_HD:(h + 1) * ATT_HD] = t
    v_ref[...] = section(2)
    for j in range(N_BF16_SECTIONS):
        t = section(3 + j)
        if j == SEC_MK:
            t = t * (ML_HD ** -0.5)
        r_ref[:, j * d:(j + 1) * d] = t.astype(BF16)
    n_gate = 2 * ML_HEADS
    x_lo = (xn - xb.astype(F32)).astype(BF16)
    both = (jnp.dot(xb, wg_ref[...], preferred_element_type=F32)
            + jnp.dot(x_lo, wg_ref[...], preferred_element_type=F32))
    lane = lax.broadcasted_iota(jnp.int32, both.shape, 1)
    g = jnp.where(lane < n_gate, both + pltpu.roll(both, GATE_LANES - n_gate, axis=1), 0.0)
    g_ref[...] = g
    gt_ref[...] = g.T


def _split_gate_weight(w_gate):
    hi = w_gate.astype(BF16)
    lo = (w_gate - hi.astype(F32)).astype(BF16)
    return jnp.pad(jnp.concatenate([hi, lo], axis=1), ((0, 0), (0, GATE_LANES - 2 * w_gate.shape[1])))


def _inproj(page_table_flat, x, norm_w, w_wide, w_gate2, rope, q_norm_w, k_norm_w, cache_k, tm, n_pos_blocks,
            pages_step, first_page):
    t_rows, d = x.shape
    rc, rs1, rs2 = rope
    n_steps = t_rows // tm
    row = lambda i, pt: (i, 0)
    fixed = lambda i, pt: (0, 0)
    pos = lambda i, pt: (i % n_pos_blocks, 0)
    once = pl.Buffered(1)
    step_of = lambda i: i
    out_specs = [
        pl.BlockSpec((tm, d), row),
        pl.BlockSpec((tm, d), row),
        pl.BlockSpec((tm, d), row),
        pl.BlockSpec((tm, N_BF16_SECTIONS * d), row),
        pl.BlockSpec((tm, GATE_LANES), row),
        pl.BlockSpec((GATE_LANES, tm), lambda i, pt: (0, i)),
    ]
    out_shape = [
        jax.ShapeDtypeStruct((t_rows, d), F32),
        jax.ShapeDtypeStruct((t_rows, d), F32),
        jax.ShapeDtypeStruct((t_rows, d), F32),
        jax.ShapeDtypeStruct((t_rows, N_BF16_SECTIONS * d), BF16),
        jax.ShapeDtypeStruct((t_rows, GATE_LANES), F32),
        jax.ShapeDtypeStruct((GATE_LANES, t_rows), F32),
    ]
    if pages_step:
        spec, shape = _means_out(pages_step, n_steps, step_of)
        out_specs.append(spec)
        out_shape.append(shape)
    return pl.pallas_call(
        functools.partial(_inproj_kernel, n_pages=pages_step),
        grid_spec=pltpu.PrefetchScalarGridSpec(
            num_scalar_prefetch=1,
            grid=(n_steps,),
            in_specs=[
                pl.BlockSpec((tm, d), row),
                pl.BlockSpec((1, d), fixed),
                pl.BlockSpec((d, N_WIDE * d), fixed, pipeline_mode=once),
                pl.BlockSpec((d, GATE_LANES), fixed, pipeline_mode=once),
                pl.BlockSpec((tm, ATT_HD), pos),
                pl.BlockSpec((tm, ATT_HD), pos),
                pl.BlockSpec((tm, ATT_HD), pos),
                pl.BlockSpec((1, ATT_HD), fixed),
                pl.BlockSpec((1, ATT_HD), fixed),
            ] + _page_specs(pages_step, first_page, step_of),
            out_specs=out_specs,
        ),
        out_shape=out_shape,
        compiler_params=_params("arbitrary"),
        name="inproj",
    )(page_table_flat, x, norm_w, w_wide, w_gate2, rc, rs1, rs2, q_norm_w, k_norm_w, *([cache_k] * pages_step))


def _rope_tables(pos):
    half = ROT_DIM // 2
    inv = ROPE_THETA ** (-(jnp.arange(half, dtype=F32) * 2.0) / ROT_DIM)
    ang = pos.astype(F32)[:, None] * inv[None, :]
    cos, sin = jnp.cos(ang), jnp.sin(ang)
    n = pos.shape[0]
    zeros = jnp.zeros((n, half), F32)
    tail0 = jnp.zeros((n, ATT_HD - ROT_DIM), F32)
    rc = jnp.concatenate([cos, cos, jnp.ones((n, ATT_HD - ROT_DIM), F32)], axis=-1)
    rs1 = jnp.concatenate([-sin, zeros, tail0], axis=-1)
    rs2 = jnp.concatenate([zeros, sin, tail0], axis=-1)
    return rc, rs1, rs2


def _moba_kernel(pt_ref, q_ref, k_ref, v_ref, az_ref, *refs, n_pages):
    del pt_ref
    o_ref, means_ref, kmean_sc, kb_sc, vt_sc, s_sc, p_sc = refs[n_pages:]
    _block_means(refs[:n_pages], means_ref)
    blk = MOBA_BLOCK
    nb = k_ref.shape[0] // blk
    rows = lambda j: slice(j * blk, (j + 1) * blk)

    for j in range(nb):
        kj = k_ref[rows(j), :]
        kmean_sc[j:j + 1, :] = jnp.sum(kj, axis=0, keepdims=True) * (1.0 / blk)
        kb_sc[rows(j), :] = kj.astype(BF16)
        vt_sc[:, rows(j)] = v_ref[rows(j), :].T.astype(BF16)
    kmean = kmean_sc[...]

    blk_id = lax.broadcasted_iota(jnp.int32, (nb, blk), 0)
    key_ix = lax.broadcasted_iota(jnp.int32, (blk, blk), 0)
    qry_ix = lax.broadcasted_iota(jnp.int32, (blk, blk), 1)
    causal = key_ix <= qry_ix

    for c in range(nb):
        q = q_ref[rows(c), :]
        gate = lax.dot_general(kmean, q, NT_DIMS, preferred_element_type=F32, precision=HIGHEST)
        past = blk_id < c
        gate = jnp.where(past, gate, NEG_INF)
        beaten = jnp.zeros(gate.shape, F32)
        for m in range(c):
            gm = gate[m:m + 1, :]
            wins = jnp.where(gm > gate, 1.0, jnp.where(gm == gate, jnp.where(blk_id > m, 1.0, 0.0), 0.0))
            beaten = beaten + wins
        bias = jnp.where(past, jnp.where(beaten < MOBA_TOPK, 0.0, NEG_INF), NEG_INF)

        qs = (q * (ATT_HD ** -0.5)).astype(BF16)
        buf = c % 2
        m_col = None
        for j in range(c + 1):
            s = lax.dot_general(kb_sc[rows(j), :], qs, NT_DIMS, preferred_element_type=F32)
            s = jnp.where(causal, s, NEG_INF) if j == c else s + bias[j:j + 1, :]
            s_sc[buf, rows(j), :] = s
            s_max = jnp.max(s, axis=0, keepdims=True)
            m_col = s_max if m_col is None else jnp.maximum(m_col, s_max)
        l_col = jnp.zeros((1, blk), F32)
        for j in range(c + 1):
            p = jnp.exp(s_sc[buf, rows(j), :] - m_col)
            l_col = l_col + jnp.sum(p, axis=0, keepdims=True)
            p_sc[buf, rows(j), :] = p.astype(BF16)
        n_keys = (c + 1) * blk
        acc = jnp.dot(vt_sc[:, :n_keys], p_sc[buf, :n_keys, :], preferred_element_type=F32)
        out = (acc / l_col).T
        o_ref[rows(c), :] = (out * _silu(az_ref[rows(c), :].astype(F32))).astype(BF16)


def _moba(page_table_flat, q, k, v, r, cache_k, batch, seq, pages_step, first_page):
    t_rows, d = q.shape
    blk = MOBA_BLOCK
    nb = seq // blk
    head = lambda b, h, pt: (b, h)
    step_of = lambda b, h: b * ATT_HEADS + h
    means_spec, means_shape = _means_out(pages_step, batch * ATT_HEADS, step_of)
    return pl.pallas_call(
        functools.partial(_moba_kernel, n_pages=pages_step),
        grid_spec=pltpu.PrefetchScalarGridSpec(
            num_scalar_prefetch=1,
            grid=(batch, ATT_HEADS),
            in_specs=[
                pl.BlockSpec((seq, ATT_HD), head),
                pl.BlockSpec((seq, ATT_HD), head),
                pl.BlockSpec((seq, ATT_HD), head),
                pl.BlockSpec((seq, ATT_HD), lambda b, h, pt: (b, SEC_AZ * ATT_HEADS + h)),
            ] + _page_specs(pages_step, first_page, step_of),
            out_specs=[pl.BlockSpec((seq, ATT_HD), head), means_spec],
            scratch_shapes=[
                pltpu.VMEM((nb, ATT_HD), F32),
                pltpu.VMEM((seq, ATT_HD), BF16),
                pltpu.VMEM((ATT_HD, seq), BF16),
                pltpu.VMEM((2, seq, blk), F32),
                pltpu.VMEM((2, seq, blk), BF16),
            ],
        ),
        out_shape=[jax.ShapeDtypeStruct((t_rows, d), BF16), means_shape],
        compiler_params=_params("arbitrary", "arbitrary"),
        name="moba_prompt",
    )(page_table_flat, q, k, v, r, *([cache_k] * pages_step))


def _mlstm_post(h, mo, mz, nw):
    h = h * _sigmoid(mo)
    h = h * lax.rsqrt(jnp.mean(h * h, axis=-1, keepdims=True) + EPS) * nw
    return h * _silu(mz)


def _mlstm_kernel(pt_ref, q_ref, k_ref, v_ref, mo_ref, mz_ref, g_ref, gt_ref, bgr_ref, bgc_ref, nw_ref, *refs,
                  n_pages):
    del pt_ref
    hm_ref, c_out, n_out, m_out, means_ref, c_sc, n_sc, m_sc = refs[n_pages:]
    _block_means(refs[:n_pages], means_ref)
    c = pl.program_id(1)
    chunk = q_ref.shape[0]

    @pl.when(c == 0)
    def _():
        c_sc[...] = jnp.zeros(c_sc.shape, F32)
        n_sc[...] = jnp.zeros(n_sc.shape, F32)
        m_sc[...] = jnp.zeros(m_sc.shape, F32)

    gc = g_ref[...] + bgr_ref[...]
    lane = lax.broadcasted_iota(jnp.int32, gc.shape, 1)
    gc = jnp.where(lane >= ML_HEADS, _log_sigmoid(gc), gc)
    gr = gt_ref[0:2 * ML_HEADS, :] + bgc_ref[0:2 * ML_HEADS, :]
    sub = lax.broadcasted_iota(jnp.int32, gr.shape, 0)
    gr = jnp.where(sub >= ML_HEADS, _log_sigmoid(gr), gr)

    t_ix = lax.broadcasted_iota(jnp.int32, (chunk, chunk), 0)
    s_ix = lax.broadcasted_iota(jnp.int32, (chunk, chunk), 1)
    causal = s_ix <= t_ix
    tril = jnp.where(causal, 1.0, 0.0)
    bc = jnp.dot(tril, gc, preferred_element_type=F32, precision=HIGHEST)
    br = lax.dot_general(gr, tril, NT_DIMS, preferred_element_type=F32, precision=HIGHEST)

    for h in range(ML_HEADS):
        cols = slice(h * ML_HD, (h + 1) * ML_HD)
        q, k, v = q_ref[:, cols], k_ref[:, cols], v_ref[:, cols]
        ig_r, b_r = gr[h:h + 1, :], br[ML_HEADS + h:ML_HEADS + h + 1, :]
        ig_c, b_c = gc[:, h:h + 1], bc[:, ML_HEADS + h:ML_HEADS + h + 1]
        m_prev = m_sc[h:h + 1, 0:1]
        c_prev = c_sc[h]
        n_prev = n_sc[h:h + 1, :]

        log_w = jnp.where(causal, b_c - b_r + ig_r, NEG_INF)
        log_inter = b_c + m_prev
        m_t = jnp.maximum(log_inter, jnp.max(log_w, axis=-1, keepdims=True))
        w_intra = jnp.exp(log_w - m_t)
        w_inter = jnp.exp(log_inter - m_t)
        s = lax.dot_general(q, k, NT_DIMS, preferred_element_type=F32) * w_intra
        num = (w_inter * jnp.dot(q, c_prev.astype(BF16), preferred_element_type=F32)
               + jnp.dot(s.astype(BF16), v, preferred_element_type=F32))
        qn = jnp.sum(q.astype(F32) * n_prev, axis=-1, keepdims=True)
        den = w_inter * qn + jnp.sum(s, axis=-1, keepdims=True)
        hh = num / jnp.maximum(jnp.abs(den), jnp.exp(-m_t))

        m_new = m_t[chunk - 1:chunk, :]
        b_last = b_c[chunk - 1:chunk, :]
        a_prev = jnp.exp(b_last + m_prev - m_new)
        a_c = jnp.exp(b_last - b_c + ig_c - m_new)
        a_r = jnp.exp(b_last - b_r + ig_r - m_new)
        kf = k.astype(F32)
        ka = (kf * a_c).astype(BF16)
        c_sc[h] = a_prev * c_prev + lax.dot_general(ka, v, TN_DIMS, preferred_element_type=F32)
        a_rows = jnp.broadcast_to(a_r, (8, chunk))
        n_sc[h:h + 1, :] = a_prev * n_prev + jnp.dot(
            a_rows, kf, preferred_element_type=F32, precision=HIGHEST)[0:1, :]
        m_sc[h:h + 1, :] = jnp.broadcast_to(m_new, (1, m_sc.shape[1]))

        hm_ref[:, cols] = _mlstm_post(
            hh, mo_ref[:, cols].astype(F32), mz_ref[:, cols].astype(F32), nw_ref[...]).astype(BF16)

    @pl.when(c == pl.num_programs(1) - 1)
    def _():
        c_out[0] = c_sc[...]
        n_out[0] = n_sc[...]
        m_out[0] = m_sc[...]


def _mlstm(page_table_flat, r, g, gt, bg_row, bg_col, mh_norm_w, cache_k, batch, seq, chunk, pages_step, first_page):
    t_rows = r.shape[0]
    d = ML_HEADS * ML_HD
    nc = seq // chunk
    sec = lambda j: (lambda b, c, pt: (b * nc + c, j))
    fixed = lambda b, c, pt: (0, 0)
    state = lambda b, c, pt: (b, 0, 0)
    step_of = lambda b, c: b * nc + c
    means_spec, means_shape = _means_out(pages_step, batch * nc, step_of)
    return pl.pallas_call(
        functools.partial(_mlstm_kernel, n_pages=pages_step),
        grid_spec=pltpu.PrefetchScalarGridSpec(
            num_scalar_prefetch=1,
            grid=(batch, nc),
            in_specs=[
                pl.BlockSpec((chunk, d), sec(SEC_MQ)),
                pl.BlockSpec((chunk, d), sec(SEC_MK)),
                pl.BlockSpec((chunk, d), sec(SEC_MV)),
                pl.BlockSpec((chunk, d), sec(SEC_MO)),
                pl.BlockSpec((chunk, d), sec(SEC_MZ)),
                pl.BlockSpec((chunk, GATE_LANES), lambda b, c, pt: (b * nc + c, 0)),
                pl.BlockSpec((GATE_LANES, chunk), lambda b, c, pt: (0, b * nc + c)),
                pl.BlockSpec((1, GATE_LANES), fixed),
                pl.BlockSpec((GATE_LANES, 1), fixed),
                pl.BlockSpec((1, ML_HD), fixed),
            ] + _page_specs(pages_step, first_page, step_of),
            out_specs=[
                pl.BlockSpec((chunk, d), lambda b, c, pt: (b * nc + c, 0)),
                pl.BlockSpec((1, ML_HEADS, ML_HD, ML_HD), lambda b, c, pt: (b, 0, 0, 0)),
                pl.BlockSpec((1, ML_HEADS, ML_HD), state),
                pl.BlockSpec((1, ML_HEADS, GATE_LANES), state),
                means_spec,
            ],
            scratch_shapes=[
                pltpu.VMEM((ML_HEADS, ML_HD, ML_HD), F32),
                pltpu.VMEM((ML_HEADS, ML_HD), F32),
                pltpu.VMEM((ML_HEADS, GATE_LANES), F32),
            ],
        ),
        out_shape=[
            jax.ShapeDtypeStruct((t_rows, d), BF16),
            jax.ShapeDtypeStruct((batch, ML_HEADS, ML_HD, ML_HD), F32),
            jax.ShapeDtypeStruct((batch, ML_HEADS, ML_HD), F32),
            jax.ShapeDtypeStruct((batch, ML_HEADS, GATE_LANES), F32),
            means_shape,
        ],
        compiler_params=_params("arbitrary", "arbitrary"),
        name="mlstm_prompt",
    )(page_table_flat, r, r, r, r, r, g, gt, bg_row, bg_col, mh_norm_w, *([cache_k] * pages_step))


def _sstep_kernel(q_ref, k_ref, v_ref, mo_ref, mz_ref, g_ref, bgr_ref, nw_ref, c_ref, n_ref, m_ref,
                  hm_ref, c_out, n_out, m_out):
    gates = g_ref[0] + bgr_ref[...]
    for h in range(ML_HEADS):
        cols = slice(h * ML_HD, (h + 1) * ML_HD)
        q = q_ref[0, :, cols].astype(F32)
        k = k_ref[0, :, cols].astype(F32)
        v = v_ref[0, :, cols].astype(F32)
        ig = gates[:, h:h + 1]
        lf = _log_sigmoid(gates[:, ML_HEADS + h:ML_HEADS + h + 1])
        m_prev = m_ref[0, :, h:h + 1]
        c_prev = c_ref[0, h]
        n_prev = n_ref[0, h:h + 1, :]

        log_inter = lf + m_prev
        m_t = jnp.maximum(log_inter, ig)
        w_intra = jnp.exp(ig - m_t)
        w_inter = jnp.exp(log_inter - m_t)
        s = jnp.sum(q * k, axis=-1, keepdims=True) * w_intra
        q8 = jnp.broadcast_to(q, (8, ML_HD))
        qc = jnp.dot(q8, c_prev, preferred_element_type=F32, precision=HIGHEST)[0:1, :]
        num = w_inter * qc + s * v
        den = w_inter * jnp.sum(q * n_prev, axis=-1, keepdims=True) + s
        hh = num / jnp.maximum(jnp.abs(den), jnp.exp(-m_t))

        sub = lax.broadcasted_iota(jnp.int32, (8, ML_HD), 0)
        k8 = jnp.where(sub == 0, jnp.broadcast_to(k, (8, ML_HD)), 0.0)
        v8 = jnp.broadcast_to(v, (8, ML_HD))
        kv = lax.dot_general(k8, v8, TN_DIMS, preferred_element_type=F32, precision=HIGHEST)
        c_out[0, h] = w_inter * c_prev + w_intra * kv
        n_out[0, h:h + 1, :] = w_inter * n_prev + w_intra * k
        m_out[0, :, h:h + 1] = m_t

        hm_ref[0, :, cols] = _mlstm_post(
            hh, mo_ref[0, :, cols].astype(F32), mz_ref[0, :, cols].astype(F32), nw_ref[...]).astype(BF16)


def _sstep(r3, g3, bg_row, mh_norm_w, c_state, n_state, m_state3):
    nb = r3.shape[0]
    d = ML_HEADS * ML_HD
    sec = lambda j: (lambda b: (b, 0, j))
    fixed = lambda b: (0, 0)
    row3 = lambda b: (b, 0, 0)
    return pl.pallas_call(
        _sstep_kernel,
        grid=(nb,),
        in_specs=[
            pl.BlockSpec((1, 1, d), sec(SEC_MQ)),
            pl.BlockSpec((1, 1, d), sec(SEC_MK)),
            pl.BlockSpec((1, 1, d), sec(SEC_MV)),
            pl.BlockSpec((1, 1, d), sec(SEC_MO)),
            pl.BlockSpec((1, 1, d), sec(SEC_MZ)),
            pl.BlockSpec((1, 1, GATE_LANES), row3),
            pl.BlockSpec((1, GATE_LANES), fixed),
            pl.BlockSpec((1, ML_HD), fixed),
            pl.BlockSpec((1, ML_HEADS, ML_HD, ML_HD), lambda b: (b, 0, 0, 0)),
            pl.BlockSpec((1, ML_HEADS, ML_HD), row3),
            pl.BlockSpec((1, 1, ML_HEADS), row3),
        ],
        out_specs=[
            pl.BlockSpec((1, 1, d), row3),
            pl.BlockSpec((1, ML_HEADS, ML_HD, ML_HD), lambda b: (b, 0, 0, 0)),
            pl.BlockSpec((1, ML_HEADS, ML_HD), row3),
            pl.BlockSpec((1, 1, ML_HEADS), row3),
        ],
        out_shape=[
            jax.ShapeDtypeStruct((nb, 1, d), BF16),
            jax.ShapeDtypeStruct(c_state.shape, F32),
            jax.ShapeDtypeStruct(n_state.shape, F32),
            jax.ShapeDtypeStruct(m_state3.shape, F32),
        ],
        compiler_params=_params("arbitrary"),
        name="mlstm_sample",
    )(r3, r3, r3, r3, r3, g3, bg_row, mh_norm_w, c_state, n_state, m_state3)


def _sgate_kernel(q_ref, kmean_ref, sel_ref):
    n_blocks = kmean_ref.shape[1]
    gate = jnp.sum(kmean_ref[0] * q_ref[...], axis=-1, keepdims=True)
    row = lax.broadcasted_iota(jnp.int32, gate.shape, 0).astype(F32)
    lane = lax.broadcasted_iota(jnp.int32, (ATT_HEADS, GATE_LANES), 1)
    picked = jnp.zeros((ATT_HEADS, GATE_LANES), F32)
    for j in range(MOBA_TOPK):
        best = jnp.max(gate, axis=0, keepdims=True)
        idx = jnp.min(jnp.where(gate == best, row, float(n_blocks)), axis=0, keepdims=True)
        picked = jnp.where(lane == j, idx[0], picked)
        gate = jnp.where(row == idx, NEG_INF, gate)
    sel_ref[0] = picked.astype(jnp.int32)


def _sgate(q_heads, kmean):
    nb, n_blocks = kmean.shape[:2]
    return pl.pallas_call(
        _sgate_kernel,
        grid=(nb,),
        in_specs=[pl.BlockSpec((1, ATT_HEADS, ATT_HD), lambda b: (b, 0, 0)),
                  pl.BlockSpec((1, n_blocks, ATT_HEADS, ATT_HD), lambda b: (b, 0, 0, 0))],
        out_specs=pl.BlockSpec((1, ATT_HEADS, GATE_LANES), lambda b: (b, 0, 0)),
        out_shape=jax.ShapeDtypeStruct((nb, ATT_HEADS, GATE_LANES), jnp.int32),
        compiler_params=_params("arbitrary"),
        name="moba_sample_gate",
    )(q_heads, kmean)


def _sattn_copies(pt_ref, sel_ref, ck_ref, cv_ref, kbuf, vbuf, sems, b, slot):
    ppb = MOBA_BLOCK // PAGE_SIZE
    copies = []
    for h in range(ATT_HEADS):
        for j in range(MOBA_TOPK):
            block = sel_ref[b, h * MOBA_TOPK + j]
            for p in range(ppb):
                page = pt_ref[b, block * ppb + p]
                rows = pl.ds((j * ppb + p) * PAGE_SIZE, PAGE_SIZE)
                copies.append(pltpu.make_async_copy(ck_ref.at[page, :, h, :], kbuf.at[slot, h, rows, :], sems.at[0, slot]))
                copies.append(pltpu.make_async_copy(cv_ref.at[page, :, h, :], vbuf.at[slot, h, rows, :], sems.at[1, slot]))
    return copies


def _sattn_kernel(pt_ref, sel_ref, q_ref, kn_ref, vn_ref, az_ref, ck_ref, cv_ref, o_ref, kbuf, vbuf, sems):
    b = pl.program_id(0)
    slot = b % 2
    fetch = functools.partial(_sattn_copies, pt_ref, sel_ref, ck_ref, cv_ref, kbuf, vbuf, sems)

    @pl.when(b == 0)
    def _():
        for cp in fetch(0, 0):
            cp.start()

    @pl.when(b + 1 < pl.num_programs(0))
    def _():
        for cp in fetch(b + 1, 1 - slot):
            cp.start()

    for cp in fetch(b, slot):
        cp.wait()

    qs = q_ref[0] * (ATT_HD ** -0.5)
    qb = qs.astype(BF16)
    s_new = jnp.sum(qs * kn_ref[0], axis=-1, keepdims=True)
    vn = vn_ref[0]
    sub = lax.broadcasted_iota(jnp.int32, (ATT_HEADS, ATT_HD), 0)
    out = jnp.zeros((ATT_HEADS, ATT_HD), F32)
    for h in range(ATT_HEADS):
        s = lax.dot_general(qb, kbuf[slot, h].astype(BF16), NT_DIMS, preferred_element_type=F32)[h:h + 1, :]
        sn = s_new[h:h + 1, :]
        m = jnp.maximum(sn, jnp.max(s, axis=-1, keepdims=True))
        p = jnp.exp(s - m)
        pn = jnp.exp(sn - m)
        l = pn + jnp.sum(p, axis=-1, keepdims=True)
        p8 = jnp.broadcast_to(p, (8, p.shape[1])).astype(BF16)
        acc = pn * vn[h:h + 1, :] + jnp.dot(p8, vbuf[slot, h].astype(BF16), preferred_element_type=F32)[0:1, :]
        out = jnp.where(sub == h, acc / l, out)
    o_ref[0] = (out * _silu(az_ref[0].astype(F32))).astype(BF16)


def _sattn(page_table, sel, q_heads, k_heads, v_heads, az_heads, cache_k, cache_v):
    nb = page_table.shape[0]
    n_keys = MOBA_TOPK * MOBA_BLOCK
    heads = lambda b, pt, sl: (b, 0, 0)
    head_spec = pl.BlockSpec((1, ATT_HEADS, ATT_HD), heads)
    return pl.pallas_call(
        _sattn_kernel,
        grid_spec=pltpu.PrefetchScalarGridSpec(
            num_scalar_prefetch=2,
            grid=(nb,),
            in_specs=[head_spec, head_spec, head_spec, head_spec,
                      pl.BlockSpec(memory_space=pl.ANY), pl.BlockSpec(memory_space=pl.ANY)],
            out_specs=head_spec,
            scratch_shapes=[
                pltpu.VMEM((2, ATT_HEADS, n_keys, ATT_HD), F32),
                pltpu.VMEM((2, ATT_HEADS, n_keys, ATT_HD), F32),
                pltpu.SemaphoreType.DMA((2, 2)),
            ],
        ),
        out_shape=jax.ShapeDtypeStruct((nb, ATT_HEADS, ATT_HD), BF16),
        compiler_params=_params("arbitrary"),
        name="moba_sample_attend",
    )(page_table, sel, q_heads, k_heads, v_heads, az_heads, cache_k, cache_v)


def _merge_kernel(x_ref, am_ref, hm_ref, ga_ref, gm_ref, wa_ref, wm_ref, wo_ref, y_ref):
    y_att = jnp.dot(am_ref[...], wa_ref[...], preferred_element_type=F32)
    y_ml = jnp.dot(hm_ref[...], wm_ref[...], preferred_element_type=F32)
    mix = _sigmoid(ga_ref[...].astype(F32)) * y_att + _sigmoid(gm_ref[...].astype(F32)) * y_ml
    y_ref[...] = x_ref[...] + jnp.dot(mix.astype(BF16), wo_ref[...], preferred_element_type=F32)


def _merge(x, am, hm, r, wa, wm, wo, tm):
    t_rows, d = x.shape
    row = lambda i: (i, 0)
    fixed = lambda i: (0, 0)
    return pl.pallas_call(
        _merge_kernel,
        grid=(t_rows // tm,),
        in_specs=[
            pl.BlockSpec((tm, d), row),
            pl.BlockSpec((tm, d), row),
            pl.BlockSpec((tm, d), row),
            pl.BlockSpec((tm, d), lambda i: (i, SEC_GA)),
            pl.BlockSpec((tm, d), lambda i: (i, SEC_GM)),
            pl.BlockSpec((d, d), fixed),
            pl.BlockSpec((d, d), fixed),
            pl.BlockSpec((d, d), fixed),
        ],
        out_specs=pl.BlockSpec((tm, d), row),
        out_shape=jax.ShapeDtypeStruct((t_rows, d), F32),
        compiler_params=_params("arbitrary"),
        name="merge",
    )(x, am, hm, r, r, wa, wm, wo)


def _row_tile(rows, target):
    tm = min(rows, target)
    while rows % tm:
        tm //= 2
    return tm


def kernel(x_prompt, x_sample, cache_k, cache_v, page_table, state_mlstm_C, state_mlstm_n, state_mlstm_m,
           norm_w, w_in, b_gates, q_norm_w, k_norm_w, mh_norm_w, w_proj_attn, w_proj_mlstm, w_out):
    batch, seq, d = x_prompt.shape
    dec_batch, dec_seq, _ = x_sample.shape
    assert dec_seq == 1 and d == ATT_HEADS * ATT_HD == ML_HEADS * ML_HD
    depth = w_in.shape[0]
    n_pages = page_table.shape[1]
    past = n_pages * PAGE_SIZE
    assert past % MOBA_BLOCK == 0 and seq % MOBA_BLOCK == 0
    t_p, t_s = batch * seq, dec_batch * dec_seq

    tm_proj = _row_tile(seq, 256)
    chunk = _row_tile(seq, 256)
    rope_p = _rope_tables(jnp.arange(seq, dtype=jnp.int32))
    rope_s = _rope_tables(jnp.tile(past + jnp.arange(dec_seq, dtype=jnp.int32), dec_batch))

    ppb = MOBA_BLOCK // PAGE_SIZE
    steps = (t_p // tm_proj, batch * ATT_HEADS, batch * (seq // chunk))
    weights = (1, 2, 1)
    total_pages = dec_batch * n_pages
    unit = -(-total_pages // (sum(s * w for s, w in zip(steps, weights)) * ppb)) * ppb
    share = tuple(unit * w for w in weights)
    first = (0, steps[0] * share[0], steps[0] * share[0] + steps[1] * share[1])
    capacity = first[2] + steps[2] * share[2]
    pt_flat = jnp.pad(page_table.reshape(-1), (0, capacity - total_pages))

    y_p = x_prompt.reshape(t_p, d)
    y_s = x_sample.reshape(t_s, d)
    outs = [[] for _ in range(10)]
    for l in range(depth):
        w_wide = _cast_wide_weight(w_in, l, d)
        w_gate2 = _split_gate_weight(w_in[l][:, N_WIDE * d:])
        bg_row = jnp.pad(b_gates[l].astype(F32), (0, GATE_LANES - 2 * ML_HEADS)).reshape(1, GATE_LANES)
        bg_col = bg_row.reshape(GATE_LANES, 1)
        nw = norm_w[l].reshape(1, d)
        qnw, knw = q_norm_w[l].reshape(1, ATT_HD), k_norm_w[l].reshape(1, ATT_HD)
        mhw = mh_norm_w[l].reshape(1, ML_HD)
        wa, wm, wo = w_proj_attn[l].astype(BF16), w_proj_mlstm[l].astype(BF16), w_out[l].astype(BF16)

        q, k, v, r, g, gt, means0 = _inproj(pt_flat, y_p, nw, w_wide, w_gate2, rope_p, qnw, knw, cache_k[l],
                                            tm_proj, seq // tm_proj, share[0], first[0])
        am, means1 = _moba(pt_flat, q, k, v, r, cache_k[l], batch, seq, share[1], first[1])
        hm, c_new, n_new, m_new, means2 = _mlstm(pt_flat, r, g, gt, bg_row, bg_col, mhw, cache_k[l], batch, seq, chunk,
                                                 share[2], first[2])
        y_p = _merge(y_p, am, hm, r, wa, wm, wo, _row_tile(t_p, 512))
        for dst, val in zip(outs[:5], (k.reshape(batch, seq, ATT_HEADS, ATT_HD), v.reshape(batch, seq, ATT_HEADS, ATT_HD),
                                       c_new, n_new, m_new[:, :, 0])):
            dst.append(val)
        kmean = jnp.concatenate([means0, means1, means2], axis=0)[:total_pages // ppb]
        kmean = kmean.reshape(dec_batch, n_pages // ppb, ATT_HEADS, ATT_HD)

        q, k, v, r, g, _ = _inproj(pt_flat, y_s, nw, w_wide, w_gate2, rope_s, qnw, knw, cache_k[l], t_s, 1, 0, 0)
        r3 = r.reshape(t_s, 1, r.shape[-1])
        by_head = lambda t: t.reshape(t_s, ATT_HEADS, ATT_HD)
        sel = _sgate(by_head(q), kmean)
        sel = sel[:, :, :MOBA_TOPK].reshape(t_s, ATT_HEADS * MOBA_TOPK)
        am = _sattn(page_table, sel, by_head(q), by_head(k), by_head(v),
                    by_head(r[:, SEC_AZ * d:(SEC_AZ + 1) * d]), cache_k[l], cache_v[l])
        hm, c_new, n_new, m_new = _sstep(
            r3, g.reshape(t_s, 1, GATE_LANES), bg_row, mhw,
            state_mlstm_C[l], state_mlstm_n[l], state_mlstm_m[l].reshape(dec_batch, 1, ML_HEADS))
        y_s = _merge(y_s, am.reshape(t_s, d), hm.reshape(t_s, d), r, wa, wm, wo, t_s)
        for dst, val in zip(outs[5:], (k.reshape(dec_batch, dec_seq, ATT_HEADS, ATT_HD),
                                       v.reshape(dec_batch, dec_seq, ATT_HEADS, ATT_HD),
                                       c_new, n_new, m_new.reshape(dec_batch, ML_HEADS))):
            dst.append(val)

    st = state_mlstm_C.dtype
    k_p, v_p, c_p, n_p, m_p, k_s, v_s, c_s, n_s, m_s = (jnp.stack(o) for o in outs)
    return (y_p.reshape(batch, seq, d), y_s.reshape(dec_batch, dec_seq, d),
            k_p, v_p, c_p.astype(st), n_p.astype(st), m_p.astype(st),
            k_s, v_s, c_s.astype(st), n_s.astype(st), m_s.astype(st))
```

```python
import functools

import jax
import jax.numpy as jnp
from jax import lax
from jax.experimental import pallas as pl
from jax.experimental.pallas import tpu as pltpu

F32 = jnp.float32
BF16 = jnp.bfloat16
HIGHEST = lax.Precision.HIGHEST
NEG_INF = float("-inf")

ATT_HEADS = 8
ATT_HD = 128
ROT_DIM = ATT_HD // 4
ROPE_THETA = 500000.0
MOBA_BLOCK = 256
MOBA_TOPK = 3
ML_HEADS = 4
ML_HD = 256
PAGE_SIZE = 128
EPS = 1e-6
N_WIDE = 11
N_BF16_SECTIONS = 8
SEC_AZ, SEC_MQ, SEC_MK, SEC_MV, SEC_MO, SEC_MZ, SEC_GA, SEC_GM = range(8)
GATE_LANES = 128
LOG2_E = 1.4426950408889634
KEY_TILE = 256

VMEM_LIMIT_BYTES = 56 * 1024 * 1024

NT_DIMS = (((1,), (1,)), ((), ()))
TN_DIMS = (((0,), (0,)), ((), ()))


def _sigmoid(x):
    return 1.0 / (1.0 + jnp.exp(-x))


def _silu(x):
    return x * _sigmoid(x)


def _log_sigmoid(x):
    return jnp.minimum(x, 0.0) - jnp.log1p(jnp.exp(-jnp.abs(x)))


def _params(*semantics):
    return pltpu.CompilerParams(dimension_semantics=semantics, vmem_limit_bytes=VMEM_LIMIT_BYTES)


def _head_norm_rope(t, w, rc, rs1, rs2):
    half = ROT_DIM // 2
    outs = []
    for h in range(ATT_HEADS):
        th = t[:, h * ATT_HD:(h + 1) * ATT_HD]
        y = th * lax.rsqrt(jnp.mean(th * th, axis=-1, keepdims=True) + EPS) * w
        up = pltpu.roll(y, ATT_HD - half, axis=1)
        down = pltpu.roll(y, half, axis=1)
        outs.append(y * rc + up * rs1 + down * rs2)
    return outs


def _cast_kernel(x_ref, o_ref):
    o_ref[...] = x_ref[...].T.astype(o_ref.dtype)


def _cast_wide_weight(w_in, layer, d):
    return pl.pallas_call(
        _cast_kernel,
        grid=(N_WIDE,),
        in_specs=[pl.BlockSpec((None, d, d), lambda j: (layer, j, 0))],
        out_specs=pl.BlockSpec((d, d), lambda j: (0, j)),
        out_shape=jax.ShapeDtypeStruct((d, N_WIDE * d), BF16),
        compiler_params=_params("arbitrary"),
        name="cast_weight",
    )(jnp.swapaxes(w_in, 1, 2))


def _page_specs(n_step, first_page, step_of):
    def spec(p):
        return pl.BlockSpec((1, PAGE_SIZE, ATT_HEADS, ATT_HD),
                            lambda *ids: (ids[-1][first_page + step_of(*ids[:-1]) * n_step + p], 0, 0, 0))
    return [spec(p) for p in range(n_step)]


def _block_means_steps(page_refs, out_ref):
    ppb = MOBA_BLOCK // PAGE_SIZE
    for j in range(len(page_refs) // ppb):
        tot = jnp.sum(page_refs[ppb * j][0], axis=0)
        for p in range(1, ppb):
            tot = tot + jnp.sum(page_refs[ppb * j + p][0], axis=0)
        out_ref[j] = tot * (1.0 / MOBA_BLOCK)
        yield


def _block_means(page_refs, out_ref):
    for _ in _block_means_steps(page_refs, out_ref):
        pass


def _emit_round_robin(streams):
    streams = list(streams)
    while streams:
        for g in list(streams):
            if next(g, "done") == "done":
                streams.remove(g)


def _means_out(n_step, n_steps, step_of):
    ppb = MOBA_BLOCK // PAGE_SIZE
    spec = pl.BlockSpec((n_step // ppb, ATT_HEADS, ATT_HD), lambda *ids: (step_of(*ids[:-1]), 0, 0))
    return spec, jax.ShapeDtypeStruct((n_steps * n_step // ppb, ATT_HEADS, ATT_HD), F32)


def _inproj_kernel(pt_ref, x_ref, nw_ref, w_ref, wg_ref, rc_ref, rs1_ref, rs2_ref, qnw_ref, knw_ref, *refs,
                   n_pages):
    del pt_ref
    q_ref, k_ref, v_ref, r_ref, g_ref, gt_ref = refs[n_pages:n_pages + 6]
    if n_pages:
        _block_means(refs[:n_pages], refs[n_pages + 6])
    d = x_ref.shape[1]
    x = x_ref[...]
    xn = x * lax.rsqrt(jnp.mean(x * x, axis=-1, keepdims=True) + EPS) * nw_ref[...]
    xb = xn.astype(BF16)

    def section(i):
        return jnp.dot(xb, w_ref[:, i * d:(i + 1) * d], preferred_element_type=F32)

    rc, rs1, rs2 = rc_ref[...], rs1_ref[...], rs2_ref[...]
    for h, t in enumerate(_head_norm_rope(section(0), qnw_ref[...], rc, rs1, rs2)):
        q_ref[:, h * ATT_HD:(h + 1) * ATT_HD] = t
    for h, t in enumerate(_head_norm_rope(section(1), knw_ref[...], rc, rs1, rs2)):
        k_ref[:, h * ATT_HD:(h + 1) * ATT_HD] = t
    v_ref[...] = section(2)
    for j in range(N_BF16_SECTIONS):
        t = section(3 + j)
        if j == SEC_MK:
            t = t * (ML_HD ** -0.5)
        r_ref[:, j * d:(j + 1) * d] = t.astype(BF16)
    n_gate = 2 * ML_HEADS
    x_lo = (xn - xb.astype(F32)).astype(BF16)
    both = (jnp.dot(xb, wg_ref[...], preferred_element_type=F32)
            + jnp.dot(x_lo, wg_ref[...], preferred_element_type=F32))
    lane = lax.broadcasted_iota(jnp.int32, both.shape, 1)
    g = jnp.where(lane < n_gate, both + pltpu.roll(both, GATE_LANES - n_gate, axis=1), 0.0)
    g_ref[...] = g
    gt_ref[...] = g.T


def _split_gate_weight(w_gate):
    hi = w_gate.astype(BF16)
    lo = (w_gate - hi.astype(F32)).astype(BF16)
    return jnp.pad(jnp.concatenate([hi, lo], axis=1), ((0, 0), (0, GATE_LANES - 2 * w_gate.shape[1])))


def _inproj(page_table_flat, x, norm_w, w_wide, w_gate2, rope, q_norm_w, k_norm_w, cache_k, tm, n_pos_blocks,
            pages_step, first_page):
    t_rows, d = x.shape
    rc, rs1, rs2 = rope
    n_steps = t_rows // tm
    row = lambda i, pt: (i, 0)
    fixed = lambda i, pt: (0, 0)
    pos = lambda i, pt: (i % n_pos_blocks, 0)
    once = pl.Buffered(1)
    step_of = lambda i: i
    out_specs = [
        pl.BlockSpec((tm, d), row),
        pl.BlockSpec((tm, d), row),
        pl.BlockSpec((tm, d), row),
        pl.BlockSpec((tm, N_BF16_SECTIONS * d), row),
        pl.BlockSpec((tm, GATE_LANES), row),
        pl.BlockSpec((GATE_LANES, tm), lambda i, pt: (0, i)),
    ]
    out_shape = [
        jax.ShapeDtypeStruct((t_rows, d), F32),
        jax.ShapeDtypeStruct((t_rows, d), F32),
        jax.ShapeDtypeStruct((t_rows, d), F32),
        jax.ShapeDtypeStruct((t_rows, N_BF16_SECTIONS * d), BF16),
        jax.ShapeDtypeStruct((t_rows, GATE_LANES), F32),
        jax.ShapeDtypeStruct((GATE_LANES, t_rows), F32),
    ]
    if pages_step:
        spec, shape = _means_out(pages_step, n_steps, step_of)
        out_specs.append(spec)
        out_shape.append(shape)
    return pl.pallas_call(
        functools.partial(_inproj_kernel, n_pages=pages_step),
        grid_spec=pltpu.PrefetchScalarGridSpec(
            num_scalar_prefetch=1,
            grid=(n_steps,),
            in_specs=[
                pl.BlockSpec((tm, d), row),
                pl.BlockSpec((1, d), fixed),
                pl.BlockSpec((d, N_WIDE * d), fixed, pipeline_mode=once),
                pl.BlockSpec((d, GATE_LANES), fixed, pipeline_mode=once),
                pl.BlockSpec((tm, ATT_HD), pos),
                pl.BlockSpec((tm, ATT_HD), pos),
                pl.BlockSpec((tm, ATT_HD), pos),
                pl.BlockSpec((1, ATT_HD), fixed),
                pl.BlockSpec((1, ATT_HD), fixed),
            ] + _page_specs(pages_step, first_page, step_of),
            out_specs=out_specs,
        ),
        out_shape=out_shape,
        compiler_params=_params("arbitrary"),
        name="inproj",
    )(page_table_flat, x, norm_w, w_wide, w_gate2, rc, rs1, rs2, q_norm_w, k_norm_w, *([cache_k] * pages_step))


def _rope_tables(pos):
    half = ROT_DIM // 2
    inv = ROPE_THETA ** (-(jnp.arange(half, dtype=F32) * 2.0) / ROT_DIM)
    ang = pos.astype(F32)[:, None] * inv[None, :]
    cos, sin = jnp.cos(ang), jnp.sin(ang)
    n = pos.shape[0]
    zeros = jnp.zeros((n, half), F32)
    tail0 = jnp.zeros((n, ATT_HD - ROT_DIM), F32)
    rc = jnp.concatenate([cos, cos, jnp.ones((n, ATT_HD - ROT_DIM), F32)], axis=-1)
    rs1 = jnp.concatenate([-sin, zeros, tail0], axis=-1)
    rs2 = jnp.concatenate([zeros, sin, tail0], axis=-1)
    return rc, rs1, rs2


def _moba_kernel(pt_ref, q_ref, k_ref, v_ref, az_ref, *refs, n_pages):
    del pt_ref
    o_ref, means_ref, kmean_sc, kb_sc, vt_sc, s_sc, p_sc = refs[n_pages:]
    blk = MOBA_BLOCK
    nb = k_ref.shape[0] // blk
    rows = lambda j: slice(j * blk, (j + 1) * blk)

    def prep(j):
        kj = k_ref[rows(j), :]
        kmean_sc[j:j + 1, :] = jnp.sum(kj, axis=0, keepdims=True) * (1.0 / blk)
        kb_sc[rows(j), :] = kj.astype(BF16)
        yield
        vt_sc[:, rows(j)] = v_ref[rows(j), :].T.astype(BF16)
        yield

    kmean_sc[...] = jnp.zeros(kmean_sc.shape, F32)

    blk_id = lax.broadcasted_iota(jnp.int32, (nb, blk), 0)
    key_ix = lax.broadcasted_iota(jnp.int32, (blk, blk), 0)
    qry_ix = lax.broadcasted_iota(jnp.int32, (blk, blk), 1)
    causal = key_ix <= qry_ix

    tiles_per_blk = blk // KEY_TILE
    fold = lambda t: t.reshape(KEY_TILE // 8, 8, blk)
    state = {}

    def pass1(c):
        q = q_ref[rows(c), :]
        gate = lax.dot_general(kmean_sc[...], q, NT_DIMS, preferred_element_type=F32, precision=HIGHEST)
        past = blk_id < c
        gate = jnp.where(past, gate, NEG_INF)
        beaten = jnp.zeros(gate.shape, F32)
        for m in range(c):
            gm = gate[m:m + 1, :]
            wins = jnp.where(gm > gate, 1.0, jnp.where(gm == gate, jnp.where(blk_id > m, 1.0, 0.0), 0.0))
            beaten = beaten + wins
        bias = jnp.where(past, jnp.where(beaten < MOBA_TOPK, 0.0, NEG_INF), NEG_INF)
        qs = (q * (ATT_HD ** -0.5 * LOG2_E)).astype(BF16)
        yield
        for j in range(c + 1):
            for t in range(tiles_per_blk):
                keys = slice(j * blk + t * KEY_TILE, j * blk + (t + 1) * KEY_TILE)
                s = lax.dot_general(kb_sc[keys, :], qs, NT_DIMS, preferred_element_type=F32)
                if j == c:
                    s = jnp.where(causal[t * KEY_TILE:(t + 1) * KEY_TILE, :], s, NEG_INF)
                else:
                    s = s + bias[j:j + 1, :]
                s_sc[c % 2, keys, :] = s
                s_max = jnp.max(fold(s), axis=0)
                state[c] = s_max if c not in state else jnp.maximum(state[c], s_max)
                yield

    def pass2(c):
        m_col = jnp.max(state[c], axis=0, keepdims=True)
        n_keys = (c + 1) * blk
        l_acc = jnp.zeros((8, blk), F32)
        for t in range(n_keys // KEY_TILE):
            keys = slice(t * KEY_TILE, (t + 1) * KEY_TILE)
            p = jnp.exp2(s_sc[c % 2, keys, :] - m_col)
            l_acc = l_acc + jnp.sum(fold(p), axis=0)
            p_sc[c % 2, keys, :] = p.astype(BF16)
            yield
        l_col = jnp.sum(l_acc, axis=0, keepdims=True)
        acc = jnp.dot(vt_sc[:, :n_keys], p_sc[c % 2, :n_keys, :], preferred_element_type=F32)
        out = (acc / l_col).T
        o_ref[rows(c), :] = (out * _silu(az_ref[rows(c), :].astype(F32))).astype(BF16)
        yield

    ppb = MOBA_BLOCK // PAGE_SIZE
    page_refs = refs[:n_pages]
    per_round = -(-n_pages // (nb * ppb)) * ppb
    _emit_round_robin([prep(j) for j in range(min(2, nb))])
    _emit_round_robin([pass1(0)])
    for c in range(nb):
        streams = [pass2(c)]
        if c + 1 < nb:
            streams.append(pass1(c + 1))
        if c + 2 < nb:
            streams.append(prep(c + 2))
        lo, hi = min(c * per_round, n_pages), min((c + 1) * per_round, n_pages)
        if hi > lo:
            streams.append(_block_means_steps(page_refs[lo:hi], means_ref.at[pl.ds(lo // ppb, (hi - lo) // ppb)]))
        _emit_round_robin(streams)


def _moba(page_table_flat, q, k, v, r, cache_k, batch, seq, pages_step, first_page):
    t_rows, d = q.shape
    blk = MOBA_BLOCK
    nb = seq // blk
    head = lambda b, h, pt: (b, h)
    step_of = lambda b, h: b * ATT_HEADS + h
    means_spec, means_shape = _means_out(pages_step, batch * ATT_HEADS, step_of)
    return pl.pallas_call(
        functools.partial(_moba_kernel, n_pages=pages_step),
        grid_spec=pltpu.PrefetchScalarGridSpec(
            num_scalar_prefetch=1,
            grid=(batch, ATT_HEADS),
            in_specs=[
                pl.BlockSpec((seq, ATT_HD), head),
                pl.BlockSpec((seq, ATT_HD), head),
                pl.BlockSpec((seq, ATT_HD), head),
                pl.BlockSpec((seq, ATT_HD), lambda b, h, pt: (b, SEC_AZ * ATT_HEADS + h)),
            ] + _page_specs(pages_step, first_page, step_of),
            out_specs=[pl.BlockSpec((seq, ATT_HD), head), means_spec],
            scratch_shapes=[
                pltpu.VMEM((nb, ATT_HD), F32),
                pltpu.VMEM((seq, ATT_HD), BF16),
                pltpu.VMEM((ATT_HD, seq), BF16),
                pltpu.VMEM((2, seq, blk), F32),
                pltpu.VMEM((2, seq, blk), BF16),
            ],
        ),
        out_shape=[jax.ShapeDtypeStruct((t_rows, d), BF16), means_shape],
        compiler_params=_params("arbitrary", "arbitrary"),
        name="moba_prompt",
    )(page_table_flat, q, k, v, r, *([cache_k] * pages_step))


def _mlstm_post(h, mo, mz, nw):
    h = h * _sigmoid(mo)
    h = h * lax.rsqrt(jnp.mean(h * h, axis=-1, keepdims=True) + EPS) * nw
    return h * _silu(mz)


def _dot_rows_hi_lo(mat, row, transposed):
    rows = jnp.broadcast_to(row, (8, row.shape[1]))
    hi = rows.astype(BF16)
    lo = (rows - hi.astype(F32)).astype(BF16)
    if transposed:
        return (lax.dot_general(mat, hi, NT_DIMS, preferred_element_type=F32)
                + lax.dot_general(mat, lo, NT_DIMS, preferred_element_type=F32))
    return jnp.dot(hi, mat, preferred_element_type=F32) + jnp.dot(lo, mat, preferred_element_type=F32)


def _mlstm_kernel(pt_ref, q_ref, k_ref, v_ref, mo_ref, mz_ref, g_ref, gt_ref, bgr_ref, bgc_ref, nw_ref, *refs,
                  n_pages):
    del pt_ref
    hm_ref, c_out, n_out, m_out, means_ref, c_sc, n_sc, m_sc = refs[n_pages:]
    _block_means(refs[:n_pages], means_ref)
    c = pl.program_id(1)
    chunk = q_ref.shape[0]

    @pl.when(c == 0)
    def _():
        c_sc[...] = jnp.zeros(c_sc.shape, F32)
        n_sc[...] = jnp.zeros(n_sc.shape, F32)
        m_sc[...] = jnp.zeros(m_sc.shape, F32)

    gc = g_ref[...] + bgr_ref[...]
    lane = lax.broadcasted_iota(jnp.int32, gc.shape, 1)
    gc = jnp.where(lane >= ML_HEADS, _log_sigmoid(gc), gc)
    gr = gt_ref[0:2 * ML_HEADS, :] + bgc_ref[0:2 * ML_HEADS, :]
    sub = lax.broadcasted_iota(jnp.int32, gr.shape, 0)
    gr = jnp.where(sub >= ML_HEADS, _log_sigmoid(gr), gr)

    t_ix = lax.broadcasted_iota(jnp.int32, (chunk, chunk), 0)
    s_ix = lax.broadcasted_iota(jnp.int32, (chunk, chunk), 1)
    causal = s_ix <= t_ix
    tril = jnp.where(causal, 1.0, 0.0)
    bc = jnp.dot(tril, gc, preferred_element_type=F32, precision=HIGHEST)
    br = lax.dot_general(gr, tril, NT_DIMS, preferred_element_type=F32, precision=HIGHEST)

    for h in range(ML_HEADS):
        cols = slice(h * ML_HD, (h + 1) * ML_HD)
        q, k, v = q_ref[:, cols], k_ref[:, cols], v_ref[:, cols]
        ig_r, b_r = gr[h:h + 1, :], br[ML_HEADS + h:ML_HEADS + h + 1, :]
        ig_c, b_c = gc[:, h:h + 1], bc[:, ML_HEADS + h:ML_HEADS + h + 1]
        m_prev = m_sc[h:h + 1, 0:1]
        c_prev = c_sc[h]
        n_prev = n_sc[h:h + 1, :]

        log_w = jnp.where(causal, b_c - b_r + ig_r, NEG_INF)
        log_inter = b_c + m_prev
        m_t = jnp.maximum(log_inter, jnp.max(log_w, axis=-1, keepdims=True))
        w_intra = jnp.exp(log_w - m_t)
        w_inter = jnp.exp(log_inter - m_t)
        s = lax.dot_general(q, k, NT_DIMS, preferred_element_type=F32) * w_intra
        num = (w_inter * jnp.dot(q, c_prev.astype(BF16), preferred_element_type=F32)
               + jnp.dot(s.astype(BF16), v, preferred_element_type=F32))
        qn = jnp.sum(q.astype(F32) * n_prev, axis=-1, keepdims=True)
        den = w_inter * qn + jnp.sum(s, axis=-1, keepdims=True)
        hh = num / jnp.maximum(jnp.abs(den), jnp.exp(-m_t))

        m_new = m_t[chunk - 1:chunk, :]
        b_last = b_c[chunk - 1:chunk, :]
        a_prev = jnp.exp(b_last + m_prev - m_new)
        a_c = jnp.exp(b_last - b_c + ig_c - m_new)
        a_r = jnp.exp(b_last - b_r + ig_r - m_new)
        kf = k.astype(F32)
        ka = (kf * a_c).astype(BF16)
        c_sc[h] = a_prev * c_prev + lax.dot_general(ka, v, TN_DIMS, preferred_element_type=F32)
        a_rows = jnp.broadcast_to(a_r, (8, chunk))
        n_sc[h:h + 1, :] = a_prev * n_prev + jnp.dot(
            a_rows, kf, preferred_element_type=F32, precision=HIGHEST)[0:1, :]
        m_sc[h:h + 1, :] = jnp.broadcast_to(m_new, (1, m_sc.shape[1]))

        hm_ref[:, cols] = _mlstm_post(
            hh, mo_ref[:, cols].astype(F32), mz_ref[:, cols].astype(F32), nw_ref[...]).astype(BF16)

    @pl.when(c == pl.num_programs(1) - 1)
    def _():
        c_out[0] = c_sc[...]
        n_out[0] = n_sc[...]
        m_out[0] = m_sc[...]


def _mlstm(page_table_flat, r, g, gt, bg_row, bg_col, mh_norm_w, cache_k, batch, seq, chunk, pages_step, first_page):
    t_rows = r.shape[0]
    d = ML_HEADS * ML_HD
    nc = seq // chunk
    sec = lambda j: (lambda b, c, pt: (b * nc + c, j))
    fixed = lambda b, c, pt: (0, 0)
    state = lambda b, c, pt: (b, 0, 0)
    step_of = lambda b, c: b * nc + c
    means_spec, means_shape = _means_out(pages_step, batch * nc, step_of)
    return pl.pallas_call(
        functools.partial(_mlstm_kernel, n_pages=pages_step),
        grid_spec=pltpu.PrefetchScalarGridSpec(
            num_scalar_prefetch=1,
            grid=(batch, nc),
            in_specs=[
                pl.BlockSpec((chunk, d), sec(SEC_MQ)),
                pl.BlockSpec((chunk, d), sec(SEC_MK)),
                pl.BlockSpec((chunk, d), sec(SEC_MV)),
                pl.BlockSpec((chunk, d), sec(SEC_MO)),
                pl.BlockSpec((chunk, d), sec(SEC_MZ)),
                pl.BlockSpec((chunk, GATE_LANES), lambda b, c, pt: (b * nc + c, 0)),
                pl.BlockSpec((GATE_LANES, chunk), lambda b, c, pt: (0, b * nc + c)),
                pl.BlockSpec((1, GATE_LANES), fixed),
                pl.BlockSpec((GATE_LANES, 1), fixed),
                pl.BlockSpec((1, ML_HD), fixed),
            ] + _page_specs(pages_step, first_page, step_of),
            out_specs=[
                pl.BlockSpec((chunk, d), lambda b, c, pt: (b * nc + c, 0)),
                pl.BlockSpec((1, ML_HEADS, ML_HD, ML_HD), lambda b, c, pt: (b, 0, 0, 0)),
                pl.BlockSpec((1, ML_HEADS, ML_HD), state),
                pl.BlockSpec((1, ML_HEADS, GATE_LANES), state),
                means_spec,
            ],
            scratch_shapes=[
                pltpu.VMEM((ML_HEADS, ML_HD, ML_HD), F32),
                pltpu.VMEM((ML_HEADS, ML_HD), F32),
                pltpu.VMEM((ML_HEADS, GATE_LANES), F32),
            ],
        ),
        out_shape=[
            jax.ShapeDtypeStruct((t_rows, d), BF16),
            jax.ShapeDtypeStruct((batch, ML_HEADS, ML_HD, ML_HD), F32),
            jax.ShapeDtypeStruct((batch, ML_HEADS, ML_HD), F32),
            jax.ShapeDtypeStruct((batch, ML_HEADS, GATE_LANES), F32),
            means_shape,
        ],
        compiler_params=_params("arbitrary", "arbitrary"),
        name="mlstm_prompt",
    )(page_table_flat, r, r, r, r, r, g, gt, bg_row, bg_col, mh_norm_w, *([cache_k] * pages_step))


def _sstep_kernel(q_ref, k_ref, v_ref, mo_ref, mz_ref, g_ref, bgr_ref, nw_ref, c_ref, n_ref, m_ref,
                  hm_ref, c_out, n_out, m_out):
    for b in range(q_ref.shape[0]):
        gates = g_ref[b] + bgr_ref[...]
        for h in range(ML_HEADS):
            cols = slice(h * ML_HD, (h + 1) * ML_HD)
            q = q_ref[b, :, cols].astype(F32)
            k = k_ref[b, :, cols].astype(F32)
            v = v_ref[b, :, cols].astype(F32)
            ig = gates[:, h:h + 1]
            lf = _log_sigmoid(gates[:, ML_HEADS + h:ML_HEADS + h + 1])
            m_prev = m_ref[b, :, h:h + 1]
            c_prev = c_ref[b, h]
            n_prev = n_ref[b, h:h + 1, :]

            log_inter = lf + m_prev
            m_t = jnp.maximum(log_inter, ig)
            w_intra = jnp.exp(ig - m_t)
            w_inter = jnp.exp(log_inter - m_t)
            s = jnp.sum(q * k, axis=-1, keepdims=True) * w_intra
            qc = _dot_rows_hi_lo(c_prev.astype(BF16), q, transposed=False)[0:1, :]
            num = w_inter * qc + s * v
            den = w_inter * jnp.sum(q * n_prev, axis=-1, keepdims=True) + s
            hh = num / jnp.maximum(jnp.abs(den), jnp.exp(-m_t))
            hm_ref[b, :, cols] = _mlstm_post(
                hh, mo_ref[b, :, cols].astype(F32), mz_ref[b, :, cols].astype(F32), nw_ref[...]).astype(BF16)

            sub = lax.broadcasted_iota(jnp.int32, (8, ML_HD), 0)
            k8 = jnp.where(sub == 0, jnp.broadcast_to(k, (8, ML_HD)), 0.0)
            v8 = jnp.broadcast_to(v, (8, ML_HD))
            kv = lax.dot_general(k8, v8, TN_DIMS, preferred_element_type=F32, precision=HIGHEST)
            c_out[b, h] = w_inter * c_prev + w_intra * kv
            n_out[b, h:h + 1, :] = w_inter * n_prev + w_intra * k
            m_out[b, :, h:h + 1] = m_t


def _sstep(r3, g3, bg_row, mh_norm_w, c_state, n_state, m_state3, rows_step):
    nb = r3.shape[0]
    d = ML_HEADS * ML_HD
    sec = lambda j: (lambda b: (b, 0, j))
    fixed = lambda b: (0, 0)
    row3 = lambda b: (b, 0, 0)
    state4 = lambda b: (b, 0, 0, 0)
    return pl.pallas_call(
        _sstep_kernel,
        grid=(nb // rows_step,),
        in_specs=[
            pl.BlockSpec((rows_step, 1, d), sec(SEC_MQ)),
            pl.BlockSpec((rows_step, 1, d), sec(SEC_MK)),
            pl.BlockSpec((rows_step, 1, d), sec(SEC_MV)),
            pl.BlockSpec((rows_step, 1, d), sec(SEC_MO)),
            pl.BlockSpec((rows_step, 1, d), sec(SEC_MZ)),
            pl.BlockSpec((rows_step, 1, GATE_LANES), row3),
            pl.BlockSpec((1, GATE_LANES), fixed),
            pl.BlockSpec((1, ML_HD), fixed),
            pl.BlockSpec((rows_step, ML_HEADS, ML_HD, ML_HD), state4),
            pl.BlockSpec((rows_step, ML_HEADS, ML_HD), row3),
            pl.BlockSpec((rows_step, 1, ML_HEADS), row3),
        ],
        out_specs=[
            pl.BlockSpec((rows_step, 1, d), row3),
            pl.BlockSpec((rows_step, ML_HEADS, ML_HD, ML_HD), state4),
            pl.BlockSpec((rows_step, ML_HEADS, ML_HD), row3),
            pl.BlockSpec((rows_step, 1, ML_HEADS), row3),
        ],
        out_shape=[
            jax.ShapeDtypeStruct((nb, 1, d), BF16),
            jax.ShapeDtypeStruct(c_state.shape, F32),
            jax.ShapeDtypeStruct(n_state.shape, F32),
            jax.ShapeDtypeStruct(m_state3.shape, F32),
        ],
        compiler_params=_params("arbitrary"),
        name="mlstm_sample",
    )(r3, r3, r3, r3, r3, g3, bg_row, mh_norm_w, c_state, n_state, m_state3)


def _sgate_kernel(q_ref, kmean_ref, sel_ref):
    n_blocks = kmean_ref.shape[1]
    gate = jnp.sum(kmean_ref[0] * q_ref[...], axis=-1, keepdims=True)
    row = lax.broadcasted_iota(jnp.int32, gate.shape, 0).astype(F32)
    lane = lax.broadcasted_iota(jnp.int32, (ATT_HEADS, GATE_LANES), 1)
    picked = jnp.zeros((ATT_HEADS, GATE_LANES), F32)
    for j in range(MOBA_TOPK):
        best = jnp.max(gate, axis=0, keepdims=True)
        idx = jnp.min(jnp.where(gate == best, row, float(n_blocks)), axis=0, keepdims=True)
        picked = jnp.where(lane == j, idx[0], picked)
        gate = jnp.where(row == idx, NEG_INF, gate)
    sel_ref[0] = picked.astype(jnp.int32)


def _sgate(q_heads, kmean):
    nb, n_blocks = kmean.shape[:2]
    return pl.pallas_call(
        _sgate_kernel,
        grid=(nb,),
        in_specs=[pl.BlockSpec((1, ATT_HEADS, ATT_HD), lambda b: (b, 0, 0)),
                  pl.BlockSpec((1, n_blocks, ATT_HEADS, ATT_HD), lambda b: (b, 0, 0, 0))],
        out_specs=pl.BlockSpec((1, ATT_HEADS, GATE_LANES), lambda b: (b, 0, 0)),
        out_shape=jax.ShapeDtypeStruct((nb, ATT_HEADS, GATE_LANES), jnp.int32),
        compiler_params=_params("arbitrary"),
        name="moba_sample_gate",
    )(q_heads, kmean)


def _sattn_copies(pt_ref, sel_ref, ck_ref, cv_ref, kbuf, vbuf, sems, b, slot):
    ppb = MOBA_BLOCK // PAGE_SIZE
    copies = []
    for h in range(ATT_HEADS):
        for j in range(MOBA_TOPK):
            block = sel_ref[b, h * MOBA_TOPK + j]
            for p in range(ppb):
                page = pt_ref[b, block * ppb + p]
                rows = pl.ds((j * ppb + p) * PAGE_SIZE, PAGE_SIZE)
                copies.append(pltpu.make_async_copy(ck_ref.at[page, :, h, :], kbuf.at[slot, h, rows, :], sems.at[0, slot]))
                copies.append(pltpu.make_async_copy(cv_ref.at[page, :, h, :], vbuf.at[slot, h, rows, :], sems.at[1, slot]))
    return copies


def _sattn_kernel(pt_ref, sel_ref, q_ref, kn_ref, vn_ref, az_ref, ck_ref, cv_ref, o_ref, kbuf, vbuf, sems):
    b = pl.program_id(0)
    slot = b % 2
    fetch = functools.partial(_sattn_copies, pt_ref, sel_ref, ck_ref, cv_ref, kbuf, vbuf, sems)

    @pl.when(b == 0)
    def _():
        for cp in fetch(0, 0):
            cp.start()

    @pl.when(b + 1 < pl.num_programs(0))
    def _():
        for cp in fetch(b + 1, 1 - slot):
            cp.start()

    for cp in fetch(b, slot):
        cp.wait()

    qs = q_ref[0] * (ATT_HD ** -0.5)
    qb = qs.astype(BF16)
    s_new = jnp.sum(qs * kn_ref[0], axis=-1, keepdims=True)
    vn = vn_ref[0]
    sub = lax.broadcasted_iota(jnp.int32, (ATT_HEADS, ATT_HD), 0)
    out = jnp.zeros((ATT_HEADS, ATT_HD), F32)
    for h in range(ATT_HEADS):
        s = lax.dot_general(qb, kbuf[slot, h].astype(BF16), NT_DIMS, preferred_element_type=F32)[h:h + 1, :]
        sn = s_new[h:h + 1, :]
        m = jnp.maximum(sn, jnp.max(s, axis=-1, keepdims=True))
        p = jnp.exp(s - m)
        pn = jnp.exp(sn - m)
        l = pn + jnp.sum(p, axis=-1, keepdims=True)
        p8 = jnp.broadcast_to(p, (8, p.shape[1])).astype(BF16)
        acc = pn * vn[h:h + 1, :] + jnp.dot(p8, vbuf[slot, h].astype(BF16), preferred_element_type=F32)[0:1, :]
        out = jnp.where(sub == h, acc / l, out)
    o_ref[0] = (out * _silu(az_ref[0].astype(F32))).astype(BF16)


def _sattn(page_table, sel, q_heads, k_heads, v_heads, az_heads, cache_k, cache_v):
    nb = page_table.shape[0]
    n_keys = MOBA_TOPK * MOBA_BLOCK
    heads = lambda b, pt, sl: (b, 0, 0)
    head_spec = pl.BlockSpec((1, ATT_HEADS, ATT_HD), heads)
    return pl.pallas_call(
        _sattn_kernel,
        grid_spec=pltpu.PrefetchScalarGridSpec(
            num_scalar_prefetch=2,
            grid=(nb,),
            in_specs=[head_spec, head_spec, head_spec, head_spec,
                      pl.BlockSpec(memory_space=pl.ANY), pl.BlockSpec(memory_space=pl.ANY)],
            out_specs=head_spec,
            scratch_shapes=[
                pltpu.VMEM((2, ATT_HEADS, n_keys, ATT_HD), F32),
                pltpu.VMEM((2, ATT_HEADS, n_keys, ATT_HD), F32),
                pltpu.SemaphoreType.DMA((2, 2)),
            ],
        ),
        out_shape=jax.ShapeDtypeStruct((nb, ATT_HEADS, ATT_HD), BF16),
        compiler_params=_params("arbitrary"),
        name="moba_sample_attend",
    )(page_table, sel, q_heads, k_heads, v_heads, az_heads, cache_k, cache_v)


def _merge_kernel(x_ref, am_ref, hm_ref, ga_ref, gm_ref, wa_ref, wm_ref, wo_ref, y_ref):
    y_att = jnp.dot(am_ref[...], wa_ref[...], preferred_element_type=F32)
    y_ml = jnp.dot(hm_ref[...], wm_ref[...], preferred_element_type=F32)
    mix = _sigmoid(ga_ref[...].astype(F32)) * y_att + _sigmoid(gm_ref[...].astype(F32)) * y_ml
    y_ref[...] = x_ref[...] + jnp.dot(mix.astype(BF16), wo_ref[...], preferred_element_type=F32)


def _merge(x, am, hm, r, wa, wm, wo, tm):
    t_rows, d = x.shape
    row = lambda i: (i, 0)
    fixed = lambda i: (0, 0)
    return pl.pallas_call(
        _merge_kernel,
        grid=(t_rows // tm,),
        in_specs=[
            pl.BlockSpec((tm, d), row),
            pl.BlockSpec((tm, d), row),
            pl.BlockSpec((tm, d), row),
            pl.BlockSpec((tm, d), lambda i: (i, SEC_GA)),
            pl.BlockSpec((tm, d), lambda i: (i, SEC_GM)),
            pl.BlockSpec((d, d), fixed),
            pl.BlockSpec((d, d), fixed),
            pl.BlockSpec((d, d), fixed),
        ],
        out_specs=pl.BlockSpec((tm, d), row),
        out_shape=jax.ShapeDtypeStruct((t_rows, d), F32),
        compiler_params=_params("arbitrary"),
        name="merge",
    )(x, am, hm, r, r, wa, wm, wo)


def _row_tile(rows, target):
    tm = min(rows, target)
    while rows % tm:
        tm //= 2
    return tm


def kernel(x_prompt, x_sample, cache_k, cache_v, page_table, state_mlstm_C, state_mlstm_n, state_mlstm_m,
           norm_w, w_in, b_gates, q_norm_w, k_norm_w, mh_norm_w, w_proj_attn, w_proj_mlstm, w_out):
    batch, seq, d = x_prompt.shape
    dec_batch, dec_seq, _ = x_sample.shape
    assert dec_seq == 1 and d == ATT_HEADS * ATT_HD == ML_HEADS * ML_HD
    depth = w_in.shape[0]
    n_pages = page_table.shape[1]
    past = n_pages * PAGE_SIZE
    assert past % MOBA_BLOCK == 0 and seq % MOBA_BLOCK == 0
    t_p, t_s = batch * seq, dec_batch * dec_seq

    tm_proj = _row_tile(seq, 256)
    chunk = _row_tile(seq, 256)
    rope_p = _rope_tables(jnp.arange(seq, dtype=jnp.int32))
    rope_s = _rope_tables(jnp.tile(past + jnp.arange(dec_seq, dtype=jnp.int32), dec_batch))

    ppb = MOBA_BLOCK // PAGE_SIZE
    steps = (t_p // tm_proj, batch * ATT_HEADS, batch * (seq // chunk))
    weights = (1, 2, 1)
    total_pages = dec_batch * n_pages
    unit = -(-total_pages // (sum(s * w for s, w in zip(steps, weights)) * ppb)) * ppb
    share = tuple(unit * w for w in weights)
    first = (0, steps[0] * share[0], steps[0] * share[0] + steps[1] * share[1])
    capacity = first[2] + steps[2] * share[2]
    pt_flat = jnp.pad(page_table.reshape(-1), (0, capacity - total_pages))

    y_p = x_prompt.reshape(t_p, d)
    y_s = x_sample.reshape(t_s, d)
    outs = [[] for _ in range(10)]
    for l in range(depth):
        w_wide = _cast_wide_weight(w_in, l, d)
        w_gate2 = _split_gate_weight(w_in[l][:, N_WIDE * d:])
        bg_row = jnp.pad(b_gates[l].astype(F32), (0, GATE_LANES - 2 * ML_HEADS)).reshape(1, GATE_LANES)
        bg_col = bg_row.reshape(GATE_LANES, 1)
        nw = norm_w[l].reshape(1, d)
        qnw, knw = q_norm_w[l].reshape(1, ATT_HD), k_norm_w[l].reshape(1, ATT_HD)
        mhw = mh_norm_w[l].reshape(1, ML_HD)
        wa, wm, wo = w_proj_attn[l].astype(BF16), w_proj_mlstm[l].astype(BF16), w_out[l].astype(BF16)

        q, k, v, r, g, gt, means0 = _inproj(pt_flat, y_p, nw, w_wide, w_gate2, rope_p, qnw, knw, cache_k[l],
                                            tm_proj, seq // tm_proj, share[0], first[0])
        am, means1 = _moba(pt_flat, q, k, v, r, cache_k[l], batch, seq, share[1], first[1])
        hm, c_new, n_new, m_new, means2 = _mlstm(pt_flat, r, g, gt, bg_row, bg_col, mhw, cache_k[l], batch, seq, chunk,
                                                 share[2], first[2])
        y_p = _merge(y_p, am, hm, r, wa, wm, wo, _row_tile(t_p, 512))
        for dst, val in zip(outs[:5], (k.reshape(batch, seq, ATT_HEADS, ATT_HD), v.reshape(batch, seq, ATT_HEADS, ATT_HD),
                                       c_new, n_new, m_new[:, :, 0])):
            dst.append(val)
        kmean = jnp.concatenate([means0, means1, means2], axis=0)[:total_pages // ppb]
        kmean = kmean.reshape(dec_batch, n_pages // ppb, ATT_HEADS, ATT_HD)

        q, k, v, r, g, _ = _inproj(pt_flat, y_s, nw, w_wide, w_gate2, rope_s, qnw, knw, cache_k[l], t_s, 1, 0, 0)
        r3 = r.reshape(t_s, 1, r.shape[-1])
        by_head = lambda t: t.reshape(t_s, ATT_HEADS, ATT_HD)
        sel = _sgate(by_head(q), kmean)
        sel = sel[:, :, :MOBA_TOPK].reshape(t_s, ATT_HEADS * MOBA_TOPK)
        am = _sattn(page_table, sel, by_head(q), by_head(k), by_head(v),
                    by_head(r[:, SEC_AZ * d:(SEC_AZ + 1) * d]), cache_k[l], cache_v[l])
        hm, c_new, n_new, m_new = _sstep(
            r3, g.reshape(t_s, 1, GATE_LANES), bg_row, mhw,
            state_mlstm_C[l], state_mlstm_n[l], state_mlstm_m[l].reshape(dec_batch, 1, ML_HEADS), _row_tile(t_s, 4))
        y_s = _merge(y_s, am.reshape(t_s, d), hm.reshape(t_s, d), r, wa, wm, wo, t_s)
        for dst, val in zip(outs[5:], (k.reshape(dec_batch, dec_seq, ATT_HEADS, ATT_HD),
                                       v.reshape(dec_batch, dec_seq, ATT_HEADS, ATT_HD),
                                       c_new, n_new, m_new.reshape(dec_batch, ML_HEADS))):
            dst.append(val)

    st = state_mlstm_C.dtype
    k_p, v_p, c_p, n_p, m_p, k_s, v_s, c_s, n_s, m_s = (jnp.stack(o) for o in outs)
    return (y_p.reshape(batch, seq, d), y_s.reshape(dec_batch, dec_seq, d),
            k_p, v_p, c_p.astype(st), n_p.astype(st), m_p.astype(st),
            k_s, v_s, c_s.astype(st), n_s.astype(st), m_s.astype(st))
```

```python
import functools

import jax
import jax.numpy as jnp
from jax import lax
from jax.experimental import pallas as pl
from jax.experimental.pallas import tpu as pltpu

F32 = jnp.float32
BF16 = jnp.bfloat16
HIGHEST = lax.Precision.HIGHEST
NEG_INF = float("-inf")

ATT_HEADS = 8
ATT_HD = 128
ROT_DIM = ATT_HD // 4
ROPE_THETA = 500000.0
MOBA_BLOCK = 256
MOBA_TOPK = 3
ML_HEADS = 4
ML_HD = 256
PAGE_SIZE = 128
EPS = 1e-6
N_WIDE = 11
N_BF16_SECTIONS = 8
SEC_AZ, SEC_MQ, SEC_MK, SEC_MV, SEC_MO, SEC_MZ, SEC_GA, SEC_GM = range(8)
GATE_LANES = 128
LOG2_E = 1.4426950408889634
KEY_TILE = 256

VMEM_LIMIT_BYTES = 60000 * 1024

NT_DIMS = (((1,), (1,)), ((), ()))
TN_DIMS = (((0,), (0,)), ((), ()))


def _sigmoid(x):
    return 1.0 / (1.0 + jnp.exp(-x))


def _silu(x):
    return x * _sigmoid(x)


def _log_sigmoid(x):
    return jnp.minimum(x, 0.0) - jnp.log1p(jnp.exp(-jnp.abs(x)))


def _params(*semantics):
    return pltpu.CompilerParams(dimension_semantics=semantics, vmem_limit_bytes=VMEM_LIMIT_BYTES)


def _head_norm_rope(t, w, rc, rs1, rs2):
    half = ROT_DIM // 2
    outs = []
    for h in range(ATT_HEADS):
        th = t[:, h * ATT_HD:(h + 1) * ATT_HD]
        y = th * lax.rsqrt(jnp.mean(th * th, axis=-1, keepdims=True) + EPS) * w
        up = pltpu.roll(y, ATT_HD - half, axis=1)
        down = pltpu.roll(y, half, axis=1)
        outs.append(y * rc + up * rs1 + down * rs2)
    return outs


def _cast_kernel(x_ref, o_ref):
    o_ref[...] = x_ref[...].T.astype(o_ref.dtype)


def _cast_wide_weight(w_in, layer, d):
    return pl.pallas_call(
        _cast_kernel,
        grid=(N_WIDE,),
        in_specs=[pl.BlockSpec((None, d, d), lambda j: (layer, j, 0))],
        out_specs=pl.BlockSpec((d, d), lambda j: (0, j)),
        out_shape=jax.ShapeDtypeStruct((d, N_WIDE * d), BF16),
        compiler_params=_params("arbitrary"),
        name="cast_weight",
    )(jnp.swapaxes(w_in, 1, 2))


def _page_specs(n_step, first_page, step_of):
    def spec(p):
        return pl.BlockSpec((1, PAGE_SIZE, ATT_HEADS, ATT_HD),
                            lambda *ids: (ids[-1][first_page + step_of(*ids[:-1]) * n_step + p], 0, 0, 0))
    return [spec(p) for p in range(n_step)]


def _block_means_steps(page_refs, out_ref):
    ppb = MOBA_BLOCK // PAGE_SIZE
    for j in range(len(page_refs) // ppb):
        tot = jnp.sum(page_refs[ppb * j][0], axis=0)
        for p in range(1, ppb):
            tot = tot + jnp.sum(page_refs[ppb * j + p][0], axis=0)
        out_ref[j] = tot * (1.0 / MOBA_BLOCK)
        yield


def _block_means(page_refs, out_ref):
    for _ in _block_means_steps(page_refs, out_ref):
        pass


def _emit_round_robin(streams):
    streams = list(streams)
    while streams:
        for g in list(streams):
            if next(g, "done") == "done":
                streams.remove(g)


def _means_out(n_step, n_steps, step_of):
    ppb = MOBA_BLOCK // PAGE_SIZE
    spec = pl.BlockSpec((n_step // ppb, ATT_HEADS, ATT_HD), lambda *ids: (step_of(*ids[:-1]), 0, 0))
    return spec, jax.ShapeDtypeStruct((n_steps * n_step // ppb, ATT_HEADS, ATT_HD), F32)


def _inproj_kernel(pt_ref, x_ref, nw_ref, wg_ref, rc_ref, rs1_ref, rs2_ref, qnw_ref, knw_ref, *refs, n_pages):
    del pt_ref
    w_refs, refs = refs[:N_WIDE], refs[N_WIDE:]
    q_ref, k_ref, v_ref, r_ref, g_ref, gt_ref = refs[n_pages:n_pages + 6]
    if n_pages:
        _block_means(refs[:n_pages], refs[n_pages + 6])
    d = x_ref.shape[1]
    x = x_ref[...]
    xn = x * lax.rsqrt(jnp.mean(x * x, axis=-1, keepdims=True) + EPS) * nw_ref[...]
    xb = xn.astype(BF16)

    def section(i):
        return jnp.dot(xb, w_refs[i][...], preferred_element_type=F32)

    rc, rs1, rs2 = rc_ref[...], rs1_ref[...], rs2_ref[...]
    for h, t in enumerate(_head_norm_rope(section(0), qnw_ref[...], rc, rs1, rs2)):
        q_ref[:, h * ATT_HD:(h + 1) * ATT_HD] = t
    for h, t in enumerate(_head_norm_rope(section(1), knw_ref[...], rc, rs1, rs2)):
        k_ref[:, h * ATT_HD:(h + 1) * ATT_HD] = t
    v_ref[...] = section(2)
    for j in range(N_BF16_SECTIONS):
        t = section(3 + j)
        if j == SEC_MK:
            t = t * (ML_HD ** -0.5)
        r_ref[:, j * d:(j + 1) * d] = t.astype(BF16)
    n_gate = 2 * ML_HEADS
    x_lo = (xn - xb.astype(F32)).astype(BF16)
    both = (jnp.dot(xb, wg_ref[...], preferred_element_type=F32)
            + jnp.dot(x_lo, wg_ref[...], preferred_element_type=F32))
    lane = lax.broadcasted_iota(jnp.int32, both.shape, 1)
    g = jnp.where(lane < n_gate, both + pltpu.roll(both, GATE_LANES - n_gate, axis=1), 0.0)
    g_ref[...] = g
    gt_ref[...] = g.T


def _split_gate_weight(w_gate):
    hi = w_gate.astype(BF16)
    lo = (w_gate - hi.astype(F32)).astype(BF16)
    return jnp.pad(jnp.concatenate([hi, lo], axis=1), ((0, 0), (0, GATE_LANES - 2 * w_gate.shape[1])))


def _inproj(page_table_flat, x, norm_w, w_wide, w_gate2, rope, q_norm_w, k_norm_w, cache_k, tm, n_pos_blocks,
            pages_step, first_page):
    t_rows, d = x.shape
    rc, rs1, rs2 = rope
    n_steps = t_rows // tm
    row = lambda i, pt: (i, 0)
    fixed = lambda i, pt: (0, 0)
    pos = lambda i, pt: (i % n_pos_blocks, 0)
    once = pl.Buffered(1)
    step_of = lambda i: i
    wide_section = lambda j: pl.BlockSpec((d, d), lambda i, pt: (0, j), pipeline_mode=once)
    out_specs = [
        pl.BlockSpec((tm, d), row),
        pl.BlockSpec((tm, d), row),
        pl.BlockSpec((tm, d), row),
        pl.BlockSpec((tm, N_BF16_SECTIONS * d), row),
        pl.BlockSpec((tm, GATE_LANES), row),
        pl.BlockSpec((GATE_LANES, tm), lambda i, pt: (0, i)),
    ]
    out_shape = [
        jax.ShapeDtypeStruct((t_rows, d), F32),
        jax.ShapeDtypeStruct((t_rows, d), F32),
        jax.ShapeDtypeStruct((t_rows, d), F32),
        jax.ShapeDtypeStruct((t_rows, N_BF16_SECTIONS * d), BF16),
        jax.ShapeDtypeStruct((t_rows, GATE_LANES), F32),
        jax.ShapeDtypeStruct((GATE_LANES, t_rows), F32),
    ]
    if pages_step:
        spec, shape = _means_out(pages_step, n_steps, step_of)
        out_specs.append(spec)
        out_shape.append(shape)
    return pl.pallas_call(
        functools.partial(_inproj_kernel, n_pages=pages_step),
        grid_spec=pltpu.PrefetchScalarGridSpec(
            num_scalar_prefetch=1,
            grid=(n_steps,),
            in_specs=[
                pl.BlockSpec((tm, d), row),
                pl.BlockSpec((1, d), fixed),
                pl.BlockSpec((d, GATE_LANES), fixed, pipeline_mode=once),
                pl.BlockSpec((tm, ATT_HD), pos),
                pl.BlockSpec((tm, ATT_HD), pos),
                pl.BlockSpec((tm, ATT_HD), pos),
                pl.BlockSpec((1, ATT_HD), fixed),
                pl.BlockSpec((1, ATT_HD), fixed),
            ] + [wide_section(j) for j in range(N_WIDE)] + _page_specs(pages_step, first_page, step_of),
            out_specs=out_specs,
        ),
        out_shape=out_shape,
        compiler_params=_params("arbitrary"),
        name="inproj",
    )(page_table_flat, x, norm_w, w_gate2, rc, rs1, rs2, q_norm_w, k_norm_w, *([w_wide] * N_WIDE),
      *([cache_k] * pages_step))


def _rope_tables(pos):
    half = ROT_DIM // 2
    inv = ROPE_THETA ** (-(jnp.arange(half, dtype=F32) * 2.0) / ROT_DIM)
    ang = pos.astype(F32)[:, None] * inv[None, :]
    cos, sin = jnp.cos(ang), jnp.sin(ang)
    n = pos.shape[0]
    zeros = jnp.zeros((n, half), F32)
    tail0 = jnp.zeros((n, ATT_HD - ROT_DIM), F32)
    rc = jnp.concatenate([cos, cos, jnp.ones((n, ATT_HD - ROT_DIM), F32)], axis=-1)
    rs1 = jnp.concatenate([-sin, zeros, tail0], axis=-1)
    rs2 = jnp.concatenate([zeros, sin, tail0], axis=-1)
    return rc, rs1, rs2


def _moba_kernel(pt_ref, q_ref, k_ref, v_ref, az_ref, *refs, n_pages):
    del pt_ref
    o_ref, means_ref, kmean_sc, kb_sc, vt_sc, s_sc, p_sc = refs[n_pages:]
    blk = MOBA_BLOCK
    nb = k_ref.shape[0] // blk
    rows = lambda j: slice(j * blk, (j + 1) * blk)

    def prep(j):
        kj = k_ref[rows(j), :]
        kmean_sc[j:j + 1, :] = jnp.sum(kj, axis=0, keepdims=True) * (1.0 / blk)
        kb_sc[rows(j), :] = kj.astype(BF16)
        yield
        vt_sc[:, rows(j)] = v_ref[rows(j), :].T.astype(BF16)
        yield

    kmean_sc[...] = jnp.zeros(kmean_sc.shape, F32)

    blk_id = lax.broadcasted_iota(jnp.int32, (nb, blk), 0)
    key_ix = lax.broadcasted_iota(jnp.int32, (blk, blk), 0)
    qry_ix = lax.broadcasted_iota(jnp.int32, (blk, blk), 1)
    causal = key_ix <= qry_ix

    tiles_per_blk = blk // KEY_TILE
    fold = lambda t: t.reshape(KEY_TILE // 8, 8, blk)
    state = {}

    def pass1(c):
        q = q_ref[rows(c), :]
        gate = lax.dot_general(kmean_sc[...], q, NT_DIMS, preferred_element_type=F32, precision=HIGHEST)
        past = blk_id < c
        gate = jnp.where(past, gate, NEG_INF)
        beaten = jnp.zeros(gate.shape, F32)
        for m in range(c):
            gm = gate[m:m + 1, :]
            wins = jnp.where(gm > gate, 1.0, jnp.where(gm == gate, jnp.where(blk_id > m, 1.0, 0.0), 0.0))
            beaten = beaten + wins
        bias = jnp.where(past, jnp.where(beaten < MOBA_TOPK, 0.0, NEG_INF), NEG_INF)
        qs = (q * (ATT_HD ** -0.5 * LOG2_E)).astype(BF16)
        yield
        for j in range(c + 1):
            for t in range(tiles_per_blk):
                keys = slice(j * blk + t * KEY_TILE, j * blk + (t + 1) * KEY_TILE)
                s = lax.dot_general(kb_sc[keys, :], qs, NT_DIMS, preferred_element_type=F32)
                if j == c:
                    s = jnp.where(causal[t * KEY_TILE:(t + 1) * KEY_TILE, :], s, NEG_INF)
                else:
                    s = s + bias[j:j + 1, :]
                s_sc[c % 2, keys, :] = s
                s_max = jnp.max(fold(s), axis=0)
                state[c] = s_max if c not in state else jnp.maximum(state[c], s_max)
                yield

    def pass2(c):
        m_col = jnp.max(state[c], axis=0, keepdims=True)
        n_keys = (c + 1) * blk
        l_acc = jnp.zeros((8, blk), F32)
        for t in range(n_keys // KEY_TILE):
            keys = slice(t * KEY_TILE, (t + 1) * KEY_TILE)
            p = jnp.exp2(s_sc[c % 2, keys, :] - m_col)
            l_acc = l_acc + jnp.sum(fold(p), axis=0)
            p_sc[c % 2, keys, :] = p.astype(BF16)
            yield
        l_col = jnp.sum(l_acc, axis=0, keepdims=True)
        acc = jnp.dot(vt_sc[:, :n_keys], p_sc[c % 2, :n_keys, :], preferred_element_type=F32)
        out = (acc / l_col).T
        o_ref[rows(c), :] = (out * _silu(az_ref[rows(c), :].astype(F32))).astype(BF16)
        yield

    ppb = MOBA_BLOCK // PAGE_SIZE
    page_refs = refs[:n_pages]
    per_round = -(-n_pages // (nb * ppb)) * ppb
    _emit_round_robin([prep(j) for j in range(min(2, nb))])
    _emit_round_robin([pass1(0)])
    for c in range(nb):
        streams = [pass2(c)]
        if c + 1 < nb:
            streams.append(pass1(c + 1))
        if c + 2 < nb:
            streams.append(prep(c + 2))
        lo, hi = min(c * per_round, n_pages), min((c + 1) * per_round, n_pages)
        if hi > lo:
            streams.append(_block_means_steps(page_refs[lo:hi], means_ref.at[pl.ds(lo // ppb, (hi - lo) // ppb)]))
        _emit_round_robin(streams)


def _moba(page_table_flat, q, k, v, r, cache_k, batch, seq, pages_step, first_page):
    t_rows, d = q.shape
    blk = MOBA_BLOCK
    nb = seq // blk
    head = lambda b, h, pt: (b, h)
    step_of = lambda b, h: b * ATT_HEADS + h
    means_spec, means_shape = _means_out(pages_step, batch * ATT_HEADS, step_of)
    return pl.pallas_call(
        functools.partial(_moba_kernel, n_pages=pages_step),
        grid_spec=pltpu.PrefetchScalarGridSpec(
            num_scalar_prefetch=1,
            grid=(batch, ATT_HEADS),
            in_specs=[
                pl.BlockSpec((seq, ATT_HD), head),
                pl.BlockSpec((seq, ATT_HD), head),
                pl.BlockSpec((seq, ATT_HD), head),
                pl.BlockSpec((seq, ATT_HD), lambda b, h, pt: (b, SEC_AZ * ATT_HEADS + h)),
            ] + _page_specs(pages_step, first_page, step_of),
            out_specs=[pl.BlockSpec((seq, ATT_HD), head), means_spec],
            scratch_shapes=[
                pltpu.VMEM((nb, ATT_HD), F32),
                pltpu.VMEM((seq, ATT_HD), BF16),
                pltpu.VMEM((ATT_HD, seq), BF16),
                pltpu.VMEM((2, seq, blk), F32),
                pltpu.VMEM((2, seq, blk), BF16),
            ],
        ),
        out_shape=[jax.ShapeDtypeStruct((t_rows, d), BF16), means_shape],
        compiler_params=_params("arbitrary", "arbitrary"),
        name="moba_prompt",
    )(page_table_flat, q, k, v, r, *([cache_k] * pages_step))


def _mlstm_post(h, mo, mz, nw):
    h = h * _sigmoid(mo)
    h = h * lax.rsqrt(jnp.mean(h * h, axis=-1, keepdims=True) + EPS) * nw
    return h * _silu(mz)


def _dot_rows_hi_lo(mat, row, transposed):
    rows = jnp.broadcast_to(row, (8, row.shape[1]))
    hi = rows.astype(BF16)
    lo = (rows - hi.astype(F32)).astype(BF16)
    if transposed:
        return (lax.dot_general(mat, hi, NT_DIMS, preferred_element_type=F32)
                + lax.dot_general(mat, lo, NT_DIMS, preferred_element_type=F32))
    return jnp.dot(hi, mat, preferred_element_type=F32) + jnp.dot(lo, mat, preferred_element_type=F32)


def _mlstm_kernel(pt_ref, q_ref, k_ref, v_ref, mo_ref, mz_ref, g_ref, gt_ref, bgr_ref, bgc_ref, nw_ref, *refs,
                  n_pages):
    del pt_ref
    hm_ref, c_out, n_out, m_out, means_ref, c_sc, n_sc, m_sc = refs[n_pages:]
    _block_means(refs[:n_pages], means_ref)
    c = pl.program_id(1)
    chunk = q_ref.shape[0]

    @pl.when(c == 0)
    def _():
        c_sc[...] = jnp.zeros(c_sc.shape, F32)
        n_sc[...] = jnp.zeros(n_sc.shape, F32)
        m_sc[...] = jnp.zeros(m_sc.shape, F32)

    gc = g_ref[...] + bgr_ref[...]
    lane = lax.broadcasted_iota(jnp.int32, gc.shape, 1)
    gc = jnp.where(lane >= ML_HEADS, _log_sigmoid(gc), gc)
    gr = gt_ref[0:2 * ML_HEADS, :] + bgc_ref[0:2 * ML_HEADS, :]
    sub = lax.broadcasted_iota(jnp.int32, gr.shape, 0)
    gr = jnp.where(sub >= ML_HEADS, _log_sigmoid(gr), gr)

    t_ix = lax.broadcasted_iota(jnp.int32, (chunk, chunk), 0)
    s_ix = lax.broadcasted_iota(jnp.int32, (chunk, chunk), 1)
    causal = s_ix <= t_ix
    tril = jnp.where(causal, 1.0, 0.0)
    bc = jnp.dot(tril, gc, preferred_element_type=F32, precision=HIGHEST)
    br = lax.dot_general(gr, tril, NT_DIMS, preferred_element_type=F32, precision=HIGHEST)

    for h in range(ML_HEADS):
        cols = slice(h * ML_HD, (h + 1) * ML_HD)
        q, k, v = q_ref[:, cols], k_ref[:, cols], v_ref[:, cols]
        ig_r, b_r = gr[h:h + 1, :], br[ML_HEADS + h:ML_HEADS + h + 1, :]
        ig_c, b_c = gc[:, h:h + 1], bc[:, ML_HEADS + h:ML_HEADS + h + 1]
        m_prev = m_sc[h:h + 1, 0:1]
        c_prev = c_sc[h]
        n_prev = n_sc[h:h + 1, :]

        log_w = jnp.where(causal, b_c - b_r + ig_r, NEG_INF)
        log_inter = b_c + m_prev
        m_t = jnp.maximum(log_inter, jnp.max(log_w, axis=-1, keepdims=True))
        w_intra = jnp.exp(log_w - m_t)
        w_inter = jnp.exp(log_inter - m_t)
        s = lax.dot_general(q, k, NT_DIMS, preferred_element_type=F32) * w_intra
        num = (w_inter * jnp.dot(q, c_prev.astype(BF16), preferred_element_type=F32)
               + jnp.dot(s.astype(BF16), v, preferred_element_type=F32))
        qn = jnp.sum(q.astype(F32) * n_prev, axis=-1, keepdims=True)
        den = w_inter * qn + jnp.sum(s, axis=-1, keepdims=True)
        hh = num / jnp.maximum(jnp.abs(den), jnp.exp(-m_t))

        m_new = m_t[chunk - 1:chunk, :]
        b_last = b_c[chunk - 1:chunk, :]
        a_prev = jnp.exp(b_last + m_prev - m_new)
        a_c = jnp.exp(b_last - b_c + ig_c - m_new)
        a_r = jnp.exp(b_last - b_r + ig_r - m_new)
        kf = k.astype(F32)
        ka = (kf * a_c).astype(BF16)
        c_sc[h] = a_prev * c_prev + lax.dot_general(ka, v, TN_DIMS, preferred_element_type=F32)
        a_rows = jnp.broadcast_to(a_r, (8, chunk))
        n_sc[h:h + 1, :] = a_prev * n_prev + jnp.dot(
            a_rows, kf, preferred_element_type=F32, precision=HIGHEST)[0:1, :]
        m_sc[h:h + 1, :] = jnp.broadcast_to(m_new, (1, m_sc.shape[1]))

        hm_ref[:, cols] = _mlstm_post(
            hh, mo_ref[:, cols].astype(F32), mz_ref[:, cols].astype(F32), nw_ref[...]).astype(BF16)

    @pl.when(c == pl.num_programs(1) - 1)
    def _():
        c_out[0] = c_sc[...]
        n_out[0] = n_sc[...]
        m_out[0] = m_sc[...]


def _mlstm(page_table_flat, r, g, gt, bg_row, bg_col, mh_norm_w, cache_k, batch, seq, chunk, pages_step, first_page):
    t_rows = r.shape[0]
    d = ML_HEADS * ML_HD
    nc = seq // chunk
    sec = lambda j: (lambda b, c, pt: (b * nc + c, j))
    fixed = lambda b, c, pt: (0, 0)
    state = lambda b, c, pt: (b, 0, 0)
    step_of = lambda b, c: b * nc + c
    means_spec, means_shape = _means_out(pages_step, batch * nc, step_of)
    return pl.pallas_call(
        functools.partial(_mlstm_kernel, n_pages=pages_step),
        grid_spec=pltpu.PrefetchScalarGridSpec(
            num_scalar_prefetch=1,
            grid=(batch, nc),
            in_specs=[
                pl.BlockSpec((chunk, d), sec(SEC_MQ)),
                pl.BlockSpec((chunk, d), sec(SEC_MK)),
                pl.BlockSpec((chunk, d), sec(SEC_MV)),
                pl.BlockSpec((chunk, d), sec(SEC_MO)),
                pl.BlockSpec((chunk, d), sec(SEC_MZ)),
                pl.BlockSpec((chunk, GATE_LANES), lambda b, c, pt: (b * nc + c, 0)),
                pl.BlockSpec((GATE_LANES, chunk), lambda b, c, pt: (0, b * nc + c)),
                pl.BlockSpec((1, GATE_LANES), fixed),
                pl.BlockSpec((GATE_LANES, 1), fixed),
                pl.BlockSpec((1, ML_HD), fixed),
            ] + _page_specs(pages_step, first_page, step_of),
            out_specs=[
                pl.BlockSpec((chunk, d), lambda b, c, pt: (b * nc + c, 0)),
                pl.BlockSpec((1, ML_HEADS, ML_HD, ML_HD), lambda b, c, pt: (b, 0, 0, 0)),
                pl.BlockSpec((1, ML_HEADS, ML_HD), state),
                pl.BlockSpec((1, ML_HEADS, GATE_LANES), state),
                means_spec,
            ],
            scratch_shapes=[
                pltpu.VMEM((ML_HEADS, ML_HD, ML_HD), F32),
                pltpu.VMEM((ML_HEADS, ML_HD), F32),
                pltpu.VMEM((ML_HEADS, GATE_LANES), F32),
            ],
        ),
        out_shape=[
            jax.ShapeDtypeStruct((t_rows, d), BF16),
            jax.ShapeDtypeStruct((batch, ML_HEADS, ML_HD, ML_HD), F32),
            jax.ShapeDtypeStruct((batch, ML_HEADS, ML_HD), F32),
            jax.ShapeDtypeStruct((batch, ML_HEADS, GATE_LANES), F32),
            means_shape,
        ],
        compiler_params=_params("arbitrary", "arbitrary"),
        name="mlstm_prompt",
    )(page_table_flat, r, r, r, r, r, g, gt, bg_row, bg_col, mh_norm_w, *([cache_k] * pages_step))


def _sstep_kernel(q_ref, k_ref, v_ref, mo_ref, mz_ref, g_ref, bgr_ref, nw_ref, c_ref, n_ref, m_ref,
                  hm_ref, c_out, n_out, m_out):
    for b in range(q_ref.shape[0]):
        gates = g_ref[b] + bgr_ref[...]
        for h in range(ML_HEADS):
            cols = slice(h * ML_HD, (h + 1) * ML_HD)
            q = q_ref[b, :, cols].astype(F32)
            k = k_ref[b, :, cols].astype(F32)
            v = v_ref[b, :, cols].astype(F32)
            ig = gates[:, h:h + 1]
            lf = _log_sigmoid(gates[:, ML_HEADS + h:ML_HEADS + h + 1])
            m_prev = m_ref[b, :, h:h + 1]
            c_prev = c_ref[b, h]
            n_prev = n_ref[b, h:h + 1, :]

            log_inter = lf + m_prev
            m_t = jnp.maximum(log_inter, ig)
            w_intra = jnp.exp(ig - m_t)
            w_inter = jnp.exp(log_inter - m_t)
            s = jnp.sum(q * k, axis=-1, keepdims=True) * w_intra
            qc = _dot_rows_hi_lo(c_prev.astype(BF16), q, transposed=False)[0:1, :]
            num = w_inter * qc + s * v
            den = w_inter * jnp.sum(q * n_prev, axis=-1, keepdims=True) + s
            hh = num / jnp.maximum(jnp.abs(den), jnp.exp(-m_t))
            hm_ref[b, :, cols] = _mlstm_post(
                hh, mo_ref[b, :, cols].astype(F32), mz_ref[b, :, cols].astype(F32), nw_ref[...]).astype(BF16)

            sub = lax.broadcasted_iota(jnp.int32, (8, ML_HD), 0)
            k8 = jnp.where(sub == 0, jnp.broadcast_to(k, (8, ML_HD)), 0.0)
            v8 = jnp.broadcast_to(v, (8, ML_HD))
            kv = lax.dot_general(k8, v8, TN_DIMS, preferred_element_type=F32, precision=HIGHEST)
            c_out[b, h] = w_inter * c_prev + w_intra * kv
            n_out[b, h:h + 1, :] = w_inter * n_prev + w_intra * k
            m_out[b, :, h:h + 1] = m_t


def _sstep(r3, g3, bg_row, mh_norm_w, c_state, n_state, m_state3, rows_step):
    nb = r3.shape[0]
    d = ML_HEADS * ML_HD
    sec = lambda j: (lambda b: (b, 0, j))
    fixed = lambda b: (0, 0)
    row3 = lambda b: (b, 0, 0)
    state4 = lambda b: (b, 0, 0, 0)
    return pl.pallas_call(
        _sstep_kernel,
        grid=(nb // rows_step,),
        in_specs=[
            pl.BlockSpec((rows_step, 1, d), sec(SEC_MQ)),
            pl.BlockSpec((rows_step, 1, d), sec(SEC_MK)),
            pl.BlockSpec((rows_step, 1, d), sec(SEC_MV)),
            pl.BlockSpec((rows_step, 1, d), sec(SEC_MO)),
            pl.BlockSpec((rows_step, 1, d), sec(SEC_MZ)),
            pl.BlockSpec((rows_step, 1, GATE_LANES), row3),
            pl.BlockSpec((1, GATE_LANES), fixed),
            pl.BlockSpec((1, ML_HD), fixed),
            pl.BlockSpec((rows_step, ML_HEADS, ML_HD, ML_HD), state4),
            pl.BlockSpec((rows_step, ML_HEADS, ML_HD), row3),
            pl.BlockSpec((rows_step, 1, ML_HEADS), row3),
        ],
        out_specs=[
            pl.BlockSpec((rows_step, 1, d), row3),
            pl.BlockSpec((rows_step, ML_HEADS, ML_HD, ML_HD), state4),
            pl.BlockSpec((rows_step, ML_HEADS, ML_HD), row3),
            pl.BlockSpec((rows_step, 1, ML_HEADS), row3),
        ],
        out_shape=[
            jax.ShapeDtypeStruct((nb, 1, d), BF16),
            jax.ShapeDtypeStruct(c_state.shape, F32),
            jax.ShapeDtypeStruct(n_state.shape, F32),
            jax.ShapeDtypeStruct(m_state3.shape, F32),
        ],
        compiler_params=_params("arbitrary"),
        name="mlstm_sample",
    )(r3, r3, r3, r3, r3, g3, bg_row, mh_norm_w, c_state, n_state, m_state3)


def _sgate_kernel(q_ref, kmean_ref, sel_ref):
    n_blocks = kmean_ref.shape[1]
    gate = jnp.sum(kmean_ref[0] * q_ref[...], axis=-1, keepdims=True)
    row = lax.broadcasted_iota(jnp.int32, gate.shape, 0).astype(F32)
    lane = lax.broadcasted_iota(jnp.int32, (ATT_HEADS, GATE_LANES), 1)
    picked = jnp.zeros((ATT_HEADS, GATE_LANES), F32)
    for j in range(MOBA_TOPK):
        best = jnp.max(gate, axis=0, keepdims=True)
        idx = jnp.min(jnp.where(gate == best, row, float(n_blocks)), axis=0, keepdims=True)
        picked = jnp.where(lane == j, idx[0], picked)
        gate = jnp.where(row == idx, NEG_INF, gate)
    sel_ref[0] = picked.astype(jnp.int32)


def _sgate(q_heads, kmean):
    nb, n_blocks = kmean.shape[:2]
    return pl.pallas_call(
        _sgate_kernel,
        grid=(nb,),
        in_specs=[pl.BlockSpec((1, ATT_HEADS, ATT_HD), lambda b: (b, 0, 0)),
                  pl.BlockSpec((1, n_blocks, ATT_HEADS, ATT_HD), lambda b: (b, 0, 0, 0))],
        out_specs=pl.BlockSpec((1, ATT_HEADS, GATE_LANES), lambda b: (b, 0, 0)),
        out_shape=jax.ShapeDtypeStruct((nb, ATT_HEADS, GATE_LANES), jnp.int32),
        compiler_params=_params("arbitrary"),
        name="moba_sample_gate",
    )(q_heads, kmean)


def _sattn_copies(pt_ref, sel_ref, ck_ref, cv_ref, kbuf, vbuf, sems, b, slot):
    ppb = MOBA_BLOCK // PAGE_SIZE
    copies = []
    for h in range(ATT_HEADS):
        for j in range(MOBA_TOPK):
            block = sel_ref[b, h * MOBA_TOPK + j]
            for p in range(ppb):
                page = pt_ref[b, block * ppb + p]
                rows = pl.ds((j * ppb + p) * PAGE_SIZE, PAGE_SIZE)
                copies.append(pltpu.make_async_copy(ck_ref.at[page, :, h, :], kbuf.at[slot, h, rows, :], sems.at[0, slot]))
                copies.append(pltpu.make_async_copy(cv_ref.at[page, :, h, :], vbuf.at[slot, h, rows, :], sems.at[1, slot]))
    return copies


def _sattn_row(q, kn, vn, az, k_sel, v_sel):
    qs = q * (ATT_HD ** -0.5)
    qb = qs.astype(BF16)
    s_new = jnp.sum(qs * kn, axis=-1, keepdims=True)
    sub = lax.broadcasted_iota(jnp.int32, (ATT_HEADS, ATT_HD), 0)
    out = jnp.zeros((ATT_HEADS, ATT_HD), F32)
    for h in range(ATT_HEADS):
        s = lax.dot_general(qb, k_sel[h].astype(BF16), NT_DIMS, preferred_element_type=F32)[h:h + 1, :]
        sn = s_new[h:h + 1, :]
        m = jnp.maximum(sn, jnp.max(s, axis=-1, keepdims=True))
        p = jnp.exp(s - m)
        pn = jnp.exp(sn - m)
        l = pn + jnp.sum(p, axis=-1, keepdims=True)
        p8 = jnp.broadcast_to(p, (8, p.shape[1])).astype(BF16)
        acc = pn * vn[h:h + 1, :] + jnp.dot(p8, v_sel[h].astype(BF16), preferred_element_type=F32)[0:1, :]
        out = jnp.where(sub == h, acc / l, out)
    return (out * _silu(az.astype(F32))).astype(BF16)


def _merge_kernel(*refs, n_side_rows, rows_step):
    if n_side_rows:
        pt_ref, sel_ref, refs = refs[0], refs[1], refs[2:]
    x_ref, am_ref, hm_ref, ga_ref, gm_ref, wa_ref, wm_ref, wo_ref = refs[:8]
    if n_side_rows:
        q_ref, kn_ref, vn_ref, az_ref, ck_ref, cv_ref, y_ref, o_ref, kbuf, vbuf, sems = refs[8:]
        fetch = functools.partial(_sattn_copies, pt_ref, sel_ref, ck_ref, cv_ref, kbuf, vbuf, sems)
        step = pl.program_id(0)
        last_block = n_side_rows // rows_step - 1

        def row_of(s, j):
            return jnp.minimum(s, last_block) * rows_step + j

        def after(s, j):
            return (s, j + 1) if j + 1 < rows_step else (s + 1, 0)

        @pl.when(step == 0)
        def _():
            for cp in fetch(row_of(0, 0), 0):
                cp.start()

        for j in range(rows_step):
            cnt = step * rows_step + j
            for cp in fetch(row_of(*after(step, j)), (cnt + 1) % 2):
                cp.start()
            for cp in fetch(row_of(step, j), cnt % 2):
                cp.wait()
            o_ref[j] = _sattn_row(q_ref[j], kn_ref[j], vn_ref[j], az_ref[j], kbuf.at[cnt % 2], vbuf.at[cnt % 2])

        @pl.when(step == pl.num_programs(0) - 1)
        def _():
            for cp in fetch(row_of(step + 1, 0), ((step + 1) * rows_step) % 2):
                cp.wait()
    else:
        y_ref = refs[8]

    y_att = jnp.dot(am_ref[...], wa_ref[...], preferred_element_type=F32)
    y_ml = jnp.dot(hm_ref[...], wm_ref[...], preferred_element_type=F32)
    mix = _sigmoid(ga_ref[...].astype(F32)) * y_att + _sigmoid(gm_ref[...].astype(F32)) * y_ml
    y_ref[...] = x_ref[...] + jnp.dot(mix.astype(BF16), wo_ref[...], preferred_element_type=F32)


def _merge(x, am, hm, r, wa, wm, wo, tm, side=None):
    t_rows, d = x.shape
    n_steps = t_rows // tm
    row = lambda i, *_: (i, 0)
    fixed = lambda i, *_: (0, 0)
    in_specs = [
        pl.BlockSpec((tm, d), row),
        pl.BlockSpec((tm, d), row),
        pl.BlockSpec((tm, d), row),
        pl.BlockSpec((tm, d), lambda i, *_: (i, SEC_GA)),
        pl.BlockSpec((tm, d), lambda i, *_: (i, SEC_GM)),
        pl.BlockSpec((d, d), fixed),
        pl.BlockSpec((d, d), fixed),
        pl.BlockSpec((d, d), fixed),
    ]
    y_spec = pl.BlockSpec((tm, d), row)
    y_shape = jax.ShapeDtypeStruct((t_rows, d), F32)
    if side is None:
        return pl.pallas_call(
            functools.partial(_merge_kernel, n_side_rows=0, rows_step=0),
            grid=(n_steps,), in_specs=in_specs, out_specs=y_spec, out_shape=y_shape,
            compiler_params=_params("arbitrary"), name="merge",
        )(x, am, hm, r, r, wa, wm, wo)

    page_table, sel, q_heads, k_heads, v_heads, az_heads, cache_k, cache_v = side
    n_rows = q_heads.shape[0]
    rows_step = next(c for c in range(1, n_rows + 1) if n_rows % c == 0 and n_rows // c <= n_steps)
    n_blocks = n_rows // rows_step
    n_keys = MOBA_TOPK * MOBA_BLOCK
    head_spec = pl.BlockSpec((rows_step, ATT_HEADS, ATT_HD), lambda i, *_: (jnp.minimum(i, n_blocks - 1), 0, 0))
    return pl.pallas_call(
        functools.partial(_merge_kernel, n_side_rows=n_rows, rows_step=rows_step),
        grid_spec=pltpu.PrefetchScalarGridSpec(
            num_scalar_prefetch=2,
            grid=(n_steps,),
            in_specs=in_specs + [head_spec, head_spec, head_spec, head_spec,
                                 pl.BlockSpec(memory_space=pl.ANY), pl.BlockSpec(memory_space=pl.ANY)],
            out_specs=[y_spec, head_spec],
            scratch_shapes=[
                pltpu.VMEM((2, ATT_HEADS, n_keys, ATT_HD), F32),
                pltpu.VMEM((2, ATT_HEADS, n_keys, ATT_HD), F32),
                pltpu.SemaphoreType.DMA((2, 2)),
            ],
        ),
        out_shape=[y_shape, jax.ShapeDtypeStruct((n_rows, ATT_HEADS, ATT_HD), BF16)],
        compiler_params=_params("arbitrary"),
        name="merge_with_sample_attend",
    )(page_table, sel, x, am, hm, r, r, wa, wm, wo, q_heads, k_heads, v_heads, az_heads, cache_k, cache_v)


def _row_tile(rows, target):
    tm = min(rows, target)
    while rows % tm:
        tm //= 2
    return tm


def kernel(x_prompt, x_sample, cache_k, cache_v, page_table, state_mlstm_C, state_mlstm_n, state_mlstm_m,
           norm_w, w_in, b_gates, q_norm_w, k_norm_w, mh_norm_w, w_proj_attn, w_proj_mlstm, w_out):
    batch, seq, d = x_prompt.shape
    dec_batch, dec_seq, _ = x_sample.shape
    assert dec_seq == 1 and d == ATT_HEADS * ATT_HD == ML_HEADS * ML_HD
    depth = w_in.shape[0]
    n_pages = page_table.shape[1]
    past = n_pages * PAGE_SIZE
    assert past % MOBA_BLOCK == 0 and seq % MOBA_BLOCK == 0
    t_p, t_s = batch * seq, dec_batch * dec_seq

    tm_proj = _row_tile(seq, 256)
    chunk = _row_tile(seq, 256)
    rope_p = _rope_tables(jnp.arange(seq, dtype=jnp.int32))
    rope_s = _rope_tables(jnp.tile(past + jnp.arange(dec_seq, dtype=jnp.int32), dec_batch))

    ppb = MOBA_BLOCK // PAGE_SIZE
    steps = (t_p // tm_proj, batch * ATT_HEADS, batch * (seq // chunk))
    weights = (2, 1, 1)
    total_pages = dec_batch * n_pages
    unit = -(-total_pages // (sum(s * w for s, w in zip(steps, weights)) * ppb)) * ppb
    share = tuple(unit * w for w in weights)
    first = (0, steps[0] * share[0], steps[0] * share[0] + steps[1] * share[1])
    capacity = first[2] + steps[2] * share[2]
    pt_flat = jnp.pad(page_table.reshape(-1), (0, capacity - total_pages))

    y_p = x_prompt.reshape(t_p, d)
    y_s = x_sample.reshape(t_s, d)
    outs = [[] for _ in range(10)]
    for l in range(depth):
        w_wide = _cast_wide_weight(w_in, l, d)
        w_gate2 = _split_gate_weight(w_in[l][:, N_WIDE * d:])
        bg_row = jnp.pad(b_gates[l].astype(F32), (0, GATE_LANES - 2 * ML_HEADS)).reshape(1, GATE_LANES)
        bg_col = bg_row.reshape(GATE_LANES, 1)
        nw = norm_w[l].reshape(1, d)
        qnw, knw = q_norm_w[l].reshape(1, ATT_HD), k_norm_w[l].reshape(1, ATT_HD)
        mhw = mh_norm_w[l].reshape(1, ML_HD)
        wa, wm, wo = w_proj_attn[l].astype(BF16), w_proj_mlstm[l].astype(BF16), w_out[l].astype(BF16)

        q, k, v, r, g, gt, means0 = _inproj(pt_flat, y_p, nw, w_wide, w_gate2, rope_p, qnw, knw, cache_k[l],
                                            tm_proj, seq // tm_proj, share[0], first[0])
        am, means1 = _moba(pt_flat, q, k, v, r, cache_k[l], batch, seq, share[1], first[1])
        hm, c_new, n_new, m_new, means2 = _mlstm(pt_flat, r, g, gt, bg_row, bg_col, mhw, cache_k[l], batch, seq, chunk,
                                                 share[2], first[2])
        for dst, val in zip(outs[:5], (k.reshape(batch, seq, ATT_HEADS, ATT_HD), v.reshape(batch, seq, ATT_HEADS, ATT_HD),
                                       c_new, n_new, m_new[:, :, 0])):
            dst.append(val)
        kmean = jnp.concatenate([means0, means1, means2], axis=0)[:total_pages // ppb]
        kmean = kmean.reshape(dec_batch, n_pages // ppb, ATT_HEADS, ATT_HD)
        r_p = r

        q, k, v, r, g, _ = _inproj(pt_flat, y_s, nw, w_wide, w_gate2, rope_s, qnw, knw, cache_k[l], t_s, 1, 0, 0)
        r3 = r.reshape(t_s, 1, r.shape[-1])
        by_head = lambda t: t.reshape(t_s, ATT_HEADS, ATT_HD)
        sel = _sgate(by_head(q), kmean)
        sel = sel[:, :, :MOBA_TOPK].reshape(t_s, ATT_HEADS * MOBA_TOPK)
        y_p, am_s = _merge(y_p, am, hm, r_p, wa, wm, wo, _row_tile(t_p, 512),
                           side=(page_table, sel, by_head(q), by_head(k), by_head(v),
                                 by_head(r[:, SEC_AZ * d:(SEC_AZ + 1) * d]), cache_k[l], cache_v[l]))
        am = am_s
        hm, c_new, n_new, m_new = _sstep(
            r3, g.reshape(t_s, 1, GATE_LANES), bg_row, mhw,
            state_mlstm_C[l], state_mlstm_n[l], state_mlstm_m[l].reshape(dec_batch, 1, ML_HEADS), _row_tile(t_s, 4))
        y_s = _merge(y_s, am.reshape(t_s, d), hm.reshape(t_s, d), r, wa, wm, wo, t_s)
        for dst, val in zip(outs[5:], (k.reshape(dec_batch, dec_seq, ATT_HEADS, ATT_HD),
                                       v.reshape(dec_batch, dec_seq, ATT_HEADS, ATT_HD),
                                       c_new, n_new, m_new.reshape(dec_batch, ML_HEADS))):
            dst.append(val)

    st = state_mlstm_C.dtype
    k_p, v_p, c_p, n_p, m_p, k_s, v_s, c_s, n_s, m_s = (jnp.stack(o) for o in outs)
    return (y_p.reshape(batch, seq, d), y_s.reshape(dec_batch, dec_seq, d),
            k_p, v_p, c_p.astype(st), n_p.astype(st), m_p.astype(st),
            k_s, v_s, c_s.astype(st), n_s.astype(st), m_s.astype(st))
```

```python
import functools

import jax
import jax.numpy as jnp
from jax import lax
from jax.experimental import pallas as pl
from jax.experimental.pallas import tpu as pltpu

F32 = jnp.float32
BF16 = jnp.bfloat16
HIGHEST = lax.Precision.HIGHEST
NEG_INF = float("-inf")

ATT_HEADS = 8
ATT_HD = 128
ROT_DIM = ATT_HD // 4
ROPE_THETA = 500000.0
MOBA_BLOCK = 256
MOBA_TOPK = 3
ML_HEADS = 4
ML_HD = 256
PAGE_SIZE = 128
EPS = 1e-6
N_WIDE = 11
N_BF16_SECTIONS = 8
SEC_AZ, SEC_MQ, SEC_MK, SEC_MV, SEC_MO, SEC_MZ, SEC_GA, SEC_GM = range(8)
GATE_LANES = 128
LOG2_E = 1.4426950408889634
KEY_TILE = 256

VMEM_LIMIT_BYTES = 60000 * 1024

NT_DIMS = (((1,), (1,)), ((), ()))
TN_DIMS = (((0,), (0,)), ((), ()))


def _sigmoid(x):
    return 1.0 / (1.0 + jnp.exp(-x))


def _silu(x):
    return x * _sigmoid(x)


def _log_sigmoid(x):
    return jnp.minimum(x, 0.0) - jnp.log1p(jnp.exp(-jnp.abs(x)))


def _params(*semantics):
    return pltpu.CompilerParams(dimension_semantics=semantics, vmem_limit_bytes=VMEM_LIMIT_BYTES)


def _head_norm_rope(t, w, rc, rs1, rs2):
    half = ROT_DIM // 2
    outs = []
    for h in range(ATT_HEADS):
        th = t[:, h * ATT_HD:(h + 1) * ATT_HD]
        y = th * lax.rsqrt(jnp.mean(th * th, axis=-1, keepdims=True) + EPS) * w
        up = pltpu.roll(y, ATT_HD - half, axis=1)
        down = pltpu.roll(y, half, axis=1)
        outs.append(y * rc + up * rs1 + down * rs2)
    return outs


def _cast_kernel(x_ref, o_ref):
    o_ref[...] = x_ref[...].T.astype(o_ref.dtype)


def _cast_wide_weight(w_in, layer, d):
    return pl.pallas_call(
        _cast_kernel,
        grid=(N_WIDE,),
        in_specs=[pl.BlockSpec((None, d, d), lambda j: (layer, j, 0))],
        out_specs=pl.BlockSpec((d, d), lambda j: (0, j)),
        out_shape=jax.ShapeDtypeStruct((d, N_WIDE * d), BF16),
        compiler_params=_params("arbitrary"),
        name="cast_weight",
    )(jnp.swapaxes(w_in, 1, 2))


def _page_specs(n_step, first_page, step_of):
    def spec(p):
        return pl.BlockSpec((1, PAGE_SIZE, ATT_HEADS, ATT_HD),
                            lambda *ids: (ids[-1][first_page + step_of(*ids[:-1]) * n_step + p], 0, 0, 0))
    return [spec(p) for p in range(n_step)]


def _block_means_steps(page_refs, out_ref):
    ppb = MOBA_BLOCK // PAGE_SIZE
    for j in range(len(page_refs) // ppb):
        tot = jnp.sum(page_refs[ppb * j][0], axis=0)
        for p in range(1, ppb):
            tot = tot + jnp.sum(page_refs[ppb * j + p][0], axis=0)
        out_ref[j] = tot * (1.0 / MOBA_BLOCK)
        yield


def _block_means(page_refs, out_ref):
    for _ in _block_means_steps(page_refs, out_ref):
        pass


def _emit_round_robin(streams):
    streams = list(streams)
    while streams:
        for g in list(streams):
            if next(g, "done") == "done":
                streams.remove(g)


def _means_out(n_step, n_steps, step_of):
    ppb = MOBA_BLOCK // PAGE_SIZE
    spec = pl.BlockSpec((n_step // ppb, ATT_HEADS, ATT_HD), lambda *ids: (step_of(*ids[:-1]), 0, 0))
    return spec, jax.ShapeDtypeStruct((n_steps * n_step // ppb, ATT_HEADS, ATT_HD), F32)


def _pin_before_readers(anchor_ref, value):
    zero = pltpu.bitcast(lax.shift_right_logical(pltpu.bitcast(value, jnp.uint32), jnp.uint32(32)), F32)
    zero = jnp.concatenate([zero, zero], axis=0).astype(anchor_ref.dtype)
    anchor_ref[0:16, 0:GATE_LANES] = anchor_ref[0:16, 0:GATE_LANES] + zero


def _inproj_kernel(pt_ref, x_ref, nw_ref, wg_ref, rc_ref, rs1_ref, rs2_ref, qnw_ref, knw_ref, *refs, n_pages):
    del pt_ref
    w_refs, refs = refs[:N_WIDE], refs[N_WIDE:]
    q_ref, k_ref, v_ref, r_ref, g_ref, gt_ref = refs[n_pages:n_pages + 6]
    xb_sc = refs[-1]
    d = x_ref.shape[1]
    x = x_ref[...]
    xn = x * lax.rsqrt(jnp.mean(x * x, axis=-1, keepdims=True) + EPS) * nw_ref[...]
    xb = xn.astype(BF16)
    xb_sc[...] = xb

    ppb = MOBA_BLOCK // PAGE_SIZE
    page_blocks = [(refs[ppb * j:ppb * (j + 1)], refs[n_pages + 6].at[pl.ds(j, 1)]) for j in range(n_pages // ppb)]

    def section(i):
        share = -(-len(page_blocks) // N_WIDE)
        for pages, out in page_blocks[i * share:(i + 1) * share]:
            _block_means(pages, out)
            _pin_before_readers(xb_sc, out[0])
        return jnp.dot(xb_sc[...], w_refs[i][...], preferred_element_type=F32)

    rc, rs1, rs2 = rc_ref[...], rs1_ref[...], rs2_ref[...]
    for h, t in enumerate(_head_norm_rope(section(0), qnw_ref[...], rc, rs1, rs2)):
        q_ref[:, h * ATT_HD:(h + 1) * ATT_HD] = t
    for h, t in enumerate(_head_norm_rope(section(1), knw_ref[...], rc, rs1, rs2)):
        k_ref[:, h * ATT_HD:(h + 1) * ATT_HD] = t
    v_ref[...] = section(2)
    for j in range(N_BF16_SECTIONS):
        t = section(3 + j)
        if j == SEC_MK:
            t = t * (ML_HD ** -0.5)
        r_ref[:, j * d:(j + 1) * d] = t.astype(BF16)
    n_gate = 2 * ML_HEADS
    x_lo = (xn - xb.astype(F32)).astype(BF16)
    both = (jnp.dot(xb_sc[...], wg_ref[...], preferred_element_type=F32)
            + jnp.dot(x_lo, wg_ref[...], preferred_element_type=F32))
    lane = lax.broadcasted_iota(jnp.int32, both.shape, 1)
    g = jnp.where(lane < n_gate, both + pltpu.roll(both, GATE_LANES - n_gate, axis=1), 0.0)
    g_ref[...] = g
    gt_ref[...] = g.T


def _split_gate_weight(w_gate):
    hi = w_gate.astype(BF16)
    lo = (w_gate - hi.astype(F32)).astype(BF16)
    return jnp.pad(jnp.concatenate([hi, lo], axis=1), ((0, 0), (0, GATE_LANES - 2 * w_gate.shape[1])))


def _inproj(page_table_flat, x, norm_w, w_wide, w_gate2, rope, q_norm_w, k_norm_w, cache_k, tm, n_pos_blocks,
            pages_step, first_page):
    t_rows, d = x.shape
    rc, rs1, rs2 = rope
    n_steps = t_rows // tm
    row = lambda i, pt: (i, 0)
    fixed = lambda i, pt: (0, 0)
    pos = lambda i, pt: (i % n_pos_blocks, 0)
    once = pl.Buffered(1)
    step_of = lambda i: i
    wide_section = lambda j: pl.BlockSpec((d, d), lambda i, pt: (0, j), pipeline_mode=once)
    out_specs = [
        pl.BlockSpec((tm, d), row),
        pl.BlockSpec((tm, d), row),
        pl.BlockSpec((tm, d), row),
        pl.BlockSpec((tm, N_BF16_SECTIONS * d), row),
        pl.BlockSpec((tm, GATE_LANES), row),
        pl.BlockSpec((GATE_LANES, tm), lambda i, pt: (0, i)),
    ]
    out_shape = [
        jax.ShapeDtypeStruct((t_rows, d), F32),
        jax.ShapeDtypeStruct((t_rows, d), F32),
        jax.ShapeDtypeStruct((t_rows, d), F32),
        jax.ShapeDtypeStruct((t_rows, N_BF16_SECTIONS * d), BF16),
        jax.ShapeDtypeStruct((t_rows, GATE_LANES), F32),
        jax.ShapeDtypeStruct((GATE_LANES, t_rows), F32),
    ]
    if pages_step:
        spec, shape = _means_out(pages_step, n_steps, step_of)
        out_specs.append(spec)
        out_shape.append(shape)
    return pl.pallas_call(
        functools.partial(_inproj_kernel, n_pages=pages_step),
        grid_spec=pltpu.PrefetchScalarGridSpec(
            num_scalar_prefetch=1,
            grid=(n_steps,),
            in_specs=[
                pl.BlockSpec((tm, d), row),
                pl.BlockSpec((1, d), fixed),
                pl.BlockSpec((d, GATE_LANES), fixed, pipeline_mode=once),
                pl.BlockSpec((tm, ATT_HD), pos),
                pl.BlockSpec((tm, ATT_HD), pos),
                pl.BlockSpec((tm, ATT_HD), pos),
                pl.BlockSpec((1, ATT_HD), fixed),
                pl.BlockSpec((1, ATT_HD), fixed),
            ] + [wide_section(j) for j in range(N_WIDE)] + _page_specs(pages_step, first_page, step_of),
            out_specs=out_specs,
            scratch_shapes=[pltpu.VMEM((tm, d), BF16)],
        ),
        out_shape=out_shape,
        compiler_params=_params("arbitrary"),
        name="inproj",
    )(page_table_flat, x, norm_w, w_gate2, rc, rs1, rs2, q_norm_w, k_norm_w, *([w_wide] * N_WIDE),
      *([cache_k] * pages_step))


def _rope_tables(pos):
    half = ROT_DIM // 2
    inv = ROPE_THETA ** (-(jnp.arange(half, dtype=F32) * 2.0) / ROT_DIM)
    ang = pos.astype(F32)[:, None] * inv[None, :]
    cos, sin = jnp.cos(ang), jnp.sin(ang)
    n = pos.shape[0]
    zeros = jnp.zeros((n, half), F32)
    tail0 = jnp.zeros((n, ATT_HD - ROT_DIM), F32)
    rc = jnp.concatenate([cos, cos, jnp.ones((n, ATT_HD - ROT_DIM), F32)], axis=-1)
    rs1 = jnp.concatenate([-sin, zeros, tail0], axis=-1)
    rs2 = jnp.concatenate([zeros, sin, tail0], axis=-1)
    return rc, rs1, rs2


def _moba_kernel(pt_ref, q_ref, k_ref, v_ref, az_ref, *refs, n_pages):
    del pt_ref
    o_ref, means_ref, kmean_sc, kb_sc, vt_sc, s_sc, p_sc = refs[n_pages:]
    blk = MOBA_BLOCK
    nb = k_ref.shape[0] // blk
    rows = lambda j: slice(j * blk, (j + 1) * blk)

    def prep(j):
        kj = k_ref[rows(j), :]
        kmean_sc[j:j + 1, :] = jnp.sum(kj, axis=0, keepdims=True) * (1.0 / blk)
        kb_sc[rows(j), :] = kj.astype(BF16)
        yield
        vt_sc[:, rows(j)] = v_ref[rows(j), :].T.astype(BF16)
        yield

    kmean_sc[...] = jnp.zeros(kmean_sc.shape, F32)

    blk_id = lax.broadcasted_iota(jnp.int32, (nb, blk), 0)
    key_ix = lax.broadcasted_iota(jnp.int32, (blk, blk), 0)
    qry_ix = lax.broadcasted_iota(jnp.int32, (blk, blk), 1)
    causal = key_ix <= qry_ix

    tiles_per_blk = blk // KEY_TILE
    fold = lambda t: t.reshape(KEY_TILE // 8, 8, blk)
    state = {}

    def pass1(c):
        q = q_ref[rows(c), :]
        gate = lax.dot_general(kmean_sc[...], q, NT_DIMS, preferred_element_type=F32, precision=HIGHEST)
        past = blk_id < c
        gate = jnp.where(past, gate, NEG_INF)
        beaten = jnp.zeros(gate.shape, F32)
        for m in range(c):
            gm = gate[m:m + 1, :]
            wins = jnp.where(gm > gate, 1.0, jnp.where(gm == gate, jnp.where(blk_id > m, 1.0, 0.0), 0.0))
            beaten = beaten + wins
        bias = jnp.where(past, jnp.where(beaten < MOBA_TOPK, 0.0, NEG_INF), NEG_INF)
        qs = (q * (ATT_HD ** -0.5 * LOG2_E)).astype(BF16)
        yield
        for j in range(c + 1):
            for t in range(tiles_per_blk):
                keys = slice(j * blk + t * KEY_TILE, j * blk + (t + 1) * KEY_TILE)
                s = lax.dot_general(kb_sc[keys, :], qs, NT_DIMS, preferred_element_type=F32)
                if j == c:
                    s = jnp.where(causal[t * KEY_TILE:(t + 1) * KEY_TILE, :], s, NEG_INF)
                else:
                    s = s + bias[j:j + 1, :]
                s_sc[c % 2, keys, :] = s
                s_max = jnp.max(fold(s), axis=0)
                state[c] = s_max if c not in state else jnp.maximum(state[c], s_max)
                yield

    def pass2(c):
        m_col = jnp.max(state[c], axis=0, keepdims=True)
        n_keys = (c + 1) * blk
        l_acc = jnp.zeros((8, blk), F32)
        for t in range(n_keys // KEY_TILE):
            keys = slice(t * KEY_TILE, (t + 1) * KEY_TILE)
            p = jnp.exp2(s_sc[c % 2, keys, :] - m_col)
            l_acc = l_acc + jnp.sum(fold(p), axis=0)
            p_sc[c % 2, keys, :] = p.astype(BF16)
            yield
        l_col = jnp.sum(l_acc, axis=0, keepdims=True)
        acc = jnp.dot(vt_sc[:, :n_keys], p_sc[c % 2, :n_keys, :], preferred_element_type=F32)
        out = (acc / l_col).T
        o_ref[rows(c), :] = (out * _silu(az_ref[rows(c), :].astype(F32))).astype(BF16)
        yield

    ppb = MOBA_BLOCK // PAGE_SIZE
    page_refs = refs[:n_pages]
    per_round = -(-n_pages // (nb * ppb)) * ppb

    def pinned_means(lo, hi):
        for j in range(lo // ppb, hi // ppb):
            out = means_ref.at[pl.ds(j, 1)]
            _block_means(page_refs[ppb * j:ppb * (j + 1)], out)
            _pin_before_readers(kb_sc, out[0])
            yield

    _emit_round_robin([prep(j) for j in range(min(2, nb))])
    _emit_round_robin([pass1(0)])
    for c in range(nb):
        streams = [pass2(c)]
        if c + 1 < nb:
            streams.append(pass1(c + 1))
        if c + 2 < nb:
            streams.append(prep(c + 2))
        lo, hi = min(c * per_round, n_pages), min((c + 1) * per_round, n_pages)
        if hi > lo:
            streams.append(pinned_means(lo, hi))
        _emit_round_robin(streams)


def _moba(page_table_flat, q, k, v, r, cache_k, batch, seq, pages_step, first_page):
    t_rows, d = q.shape
    blk = MOBA_BLOCK
    nb = seq // blk
    head = lambda b, h, pt: (b, h)
    step_of = lambda b, h: b * ATT_HEADS + h
    means_spec, means_shape = _means_out(pages_step, batch * ATT_HEADS, step_of)
    return pl.pallas_call(
        functools.partial(_moba_kernel, n_pages=pages_step),
        grid_spec=pltpu.PrefetchScalarGridSpec(
            num_scalar_prefetch=1,
            grid=(batch, ATT_HEADS),
            in_specs=[
                pl.BlockSpec((seq, ATT_HD), head),
                pl.BlockSpec((seq, ATT_HD), head),
                pl.BlockSpec((seq, ATT_HD), head),
                pl.BlockSpec((seq, ATT_HD), lambda b, h, pt: (b, SEC_AZ * ATT_HEADS + h)),
            ] + _page_specs(pages_step, first_page, step_of),
            out_specs=[pl.BlockSpec((seq, ATT_HD), head), means_spec],
            scratch_shapes=[
                pltpu.VMEM((nb, ATT_HD), F32),
                pltpu.VMEM((seq, ATT_HD), BF16),
                pltpu.VMEM((ATT_HD, seq), BF16),
                pltpu.VMEM((2, seq, blk), F32),
                pltpu.VMEM((2, seq, blk), BF16),
            ],
        ),
        out_shape=[jax.ShapeDtypeStruct((t_rows, d), BF16), means_shape],
        compiler_params=_params("arbitrary", "arbitrary"),
        name="moba_prompt",
    )(page_table_flat, q, k, v, r, *([cache_k] * pages_step))


def _mlstm_post(h, mo, mz, nw):
    h = h * _sigmoid(mo)
    h = h * lax.rsqrt(jnp.mean(h * h, axis=-1, keepdims=True) + EPS) * nw
    return h * _silu(mz)


def _dot_rows_hi_lo(mat, row, transposed):
    rows = jnp.broadcast_to(row, (8, row.shape[1]))
    hi = rows.astype(BF16)
    lo = (rows - hi.astype(F32)).astype(BF16)
    if transposed:
        return (lax.dot_general(mat, hi, NT_DIMS, preferred_element_type=F32)
                + lax.dot_general(mat, lo, NT_DIMS, preferred_element_type=F32))
    return jnp.dot(hi, mat, preferred_element_type=F32) + jnp.dot(lo, mat, preferred_element_type=F32)


def _mlstm_kernel(pt_ref, q_ref, k_ref, v_ref, mo_ref, mz_ref, g_ref, gt_ref, bgr_ref, bgc_ref, nw_ref, *refs,
                  n_pages):
    del pt_ref
    hm_ref, c_out, n_out, m_out, means_ref, c_sc, n_sc, m_sc = refs[n_pages:]
    c = pl.program_id(1)
    chunk = q_ref.shape[0]

    @pl.when(c == 0)
    def _():
        c_sc[...] = jnp.zeros(c_sc.shape, F32)
        n_sc[...] = jnp.zeros(n_sc.shape, F32)
        m_sc[...] = jnp.zeros(m_sc.shape, F32)

    _block_means(refs[:n_pages], means_ref)

    gc = g_ref[...] + bgr_ref[...]
    lane = lax.broadcasted_iota(jnp.int32, gc.shape, 1)
    gc = jnp.where(lane >= ML_HEADS, _log_sigmoid(gc), gc)
    gr = gt_ref[0:2 * ML_HEADS, :] + bgc_ref[0:2 * ML_HEADS, :]
    sub = lax.broadcasted_iota(jnp.int32, gr.shape, 0)
    gr = jnp.where(sub >= ML_HEADS, _log_sigmoid(gr), gr)

    t_ix = lax.broadcasted_iota(jnp.int32, (chunk, chunk), 0)
    s_ix = lax.broadcasted_iota(jnp.int32, (chunk, chunk), 1)
    causal = s_ix <= t_ix
    tril = jnp.where(causal, 1.0, 0.0)
    bc = jnp.dot(tril, gc, preferred_element_type=F32, precision=HIGHEST)
    br = lax.dot_general(gr, tril, NT_DIMS, preferred_element_type=F32, precision=HIGHEST)

    for h in range(ML_HEADS):
        cols = slice(h * ML_HD, (h + 1) * ML_HD)
        q, k, v = q_ref[:, cols], k_ref[:, cols], v_ref[:, cols]
        ig_r, b_r = gr[h:h + 1, :], br[ML_HEADS + h:ML_HEADS + h + 1, :]
        ig_c, b_c = gc[:, h:h + 1], bc[:, ML_HEADS + h:ML_HEADS + h + 1]
        m_prev = m_sc[h:h + 1, 0:1]
        c_prev = c_sc[h]
        n_prev = n_sc[h:h + 1, :]

        log_w = jnp.where(causal, b_c - b_r + ig_r, NEG_INF)
        log_inter = b_c + m_prev
        m_t = jnp.maximum(log_inter, jnp.max(log_w, axis=-1, keepdims=True))
        w_intra = jnp.exp(log_w - m_t)
        w_inter = jnp.exp(log_inter - m_t)
        s = lax.dot_general(q, k, NT_DIMS, preferred_element_type=F32) * w_intra
        num = (w_inter * jnp.dot(q, c_prev.astype(BF16), preferred_element_type=F32)
               + jnp.dot(s.astype(BF16), v, preferred_element_type=F32))
        qn = jnp.sum(q.astype(F32) * n_prev, axis=-1, keepdims=True)
        den = w_inter * qn + jnp.sum(s, axis=-1, keepdims=True)
        hh = num / jnp.maximum(jnp.abs(den), jnp.exp(-m_t))

        m_new = m_t[chunk - 1:chunk, :]
        b_last = b_c[chunk - 1:chunk, :]
        a_prev = jnp.exp(b_last + m_prev - m_new)
        a_c = jnp.exp(b_last - b_c + ig_c - m_new)
        a_r = jnp.exp(b_last - b_r + ig_r - m_new)
        kf = k.astype(F32)
        ka = (kf * a_c).astype(BF16)
        c_sc[h] = a_prev * c_prev + lax.dot_general(ka, v, TN_DIMS, preferred_element_type=F32)
        a_rows = jnp.broadcast_to(a_r, (8, chunk))
        n_sc[h:h + 1, :] = a_prev * n_prev + jnp.dot(
            a_rows, kf, preferred_element_type=F32, precision=HIGHEST)[0:1, :]
        m_sc[h:h + 1, :] = jnp.broadcast_to(m_new, (1, m_sc.shape[1]))

        hm_ref[:, cols] = _mlstm_post(
            hh, mo_ref[:, cols].astype(F32), mz_ref[:, cols].astype(F32), nw_ref[...]).astype(BF16)

    @pl.when(c == pl.num_programs(1) - 1)
    def _():
        c_out[0] = c_sc[...]
        n_out[0] = n_sc[...]
        m_out[0] = m_sc[...]


def _mlstm(page_table_flat, r, g, gt, bg_row, bg_col, mh_norm_w, cache_k, batch, seq, chunk, pages_step, first_page):
    t_rows = r.shape[0]
    d = ML_HEADS * ML_HD
    nc = seq // chunk
    sec = lambda j: (lambda b, c, pt: (b * nc + c, j))
    fixed = lambda b, c, pt: (0, 0)
    state = lambda b, c, pt: (b, 0, 0)
    step_of = lambda b, c: b * nc + c
    means_spec, means_shape = _means_out(pages_step, batch * nc, step_of)
    return pl.pallas_call(
        functools.partial(_mlstm_kernel, n_pages=pages_step),
        grid_spec=pltpu.PrefetchScalarGridSpec(
            num_scalar_prefetch=1,
            grid=(batch, nc),
            in_specs=[
                pl.BlockSpec((chunk, d), sec(SEC_MQ)),
                pl.BlockSpec((chunk, d), sec(SEC_MK)),
                pl.BlockSpec((chunk, d), sec(SEC_MV)),
                pl.BlockSpec((chunk, d), sec(SEC_MO)),
                pl.BlockSpec((chunk, d), sec(SEC_MZ)),
                pl.BlockSpec((chunk, GATE_LANES), lambda b, c, pt: (b * nc + c, 0)),
                pl.BlockSpec((GATE_LANES, chunk), lambda b, c, pt: (0, b * nc + c)),
                pl.BlockSpec((1, GATE_LANES), fixed),
                pl.BlockSpec((GATE_LANES, 1), fixed),
                pl.BlockSpec((1, ML_HD), fixed),
            ] + _page_specs(pages_step, first_page, step_of),
            out_specs=[
                pl.BlockSpec((chunk, d), lambda b, c, pt: (b * nc + c, 0)),
                pl.BlockSpec((1, ML_HEADS, ML_HD, ML_HD), lambda b, c, pt: (b, 0, 0, 0)),
                pl.BlockSpec((1, ML_HEADS, ML_HD), state),
                pl.BlockSpec((1, ML_HEADS, GATE_LANES), state),
                means_spec,
            ],
            scratch_shapes=[
                pltpu.VMEM((ML_HEADS, ML_HD, ML_HD), F32),
                pltpu.VMEM((ML_HEADS, ML_HD), F32),
                pltpu.VMEM((ML_HEADS, GATE_LANES), F32),
            ],
        ),
        out_shape=[
            jax.ShapeDtypeStruct((t_rows, d), BF16),
            jax.ShapeDtypeStruct((batch, ML_HEADS, ML_HD, ML_HD), F32),
            jax.ShapeDtypeStruct((batch, ML_HEADS, ML_HD), F32),
            jax.ShapeDtypeStruct((batch, ML_HEADS, GATE_LANES), F32),
            means_shape,
        ],
        compiler_params=_params("arbitrary", "arbitrary"),
        name="mlstm_prompt",
    )(page_table_flat, r, r, r, r, r, g, gt, bg_row, bg_col, mh_norm_w, *([cache_k] * pages_step))


def _sstep_kernel(q_ref, k_ref, v_ref, mo_ref, mz_ref, g_ref, bgr_ref, nw_ref, c_ref, n_ref, m_ref,
                  hm_ref, c_out, n_out, m_out):
    for b in range(q_ref.shape[0]):
        gates = g_ref[b] + bgr_ref[...]
        for h in range(ML_HEADS):
            cols = slice(h * ML_HD, (h + 1) * ML_HD)
            q = q_ref[b, :, cols].astype(F32)
            k = k_ref[b, :, cols].astype(F32)
            v = v_ref[b, :, cols].astype(F32)
            ig = gates[:, h:h + 1]
            lf = _log_sigmoid(gates[:, ML_HEADS + h:ML_HEADS + h + 1])
            m_prev = m_ref[b, :, h:h + 1]
            c_prev = c_ref[b, h]
            n_prev = n_ref[b, h:h + 1, :]

            log_inter = lf + m_prev
            m_t = jnp.maximum(log_inter, ig)
            w_intra = jnp.exp(ig - m_t)
            w_inter = jnp.exp(log_inter - m_t)
            s = jnp.sum(q * k, axis=-1, keepdims=True) * w_intra
            qc = _dot_rows_hi_lo(c_prev.astype(BF16), q, transposed=False)[0:1, :]
            num = w_inter * qc + s * v
            den = w_inter * jnp.sum(q * n_prev, axis=-1, keepdims=True) + s
            hh = num / jnp.maximum(jnp.abs(den), jnp.exp(-m_t))
            hm_ref[b, :, cols] = _mlstm_post(
                hh, mo_ref[b, :, cols].astype(F32), mz_ref[b, :, cols].astype(F32), nw_ref[...]).astype(BF16)

            sub = lax.broadcasted_iota(jnp.int32, (8, ML_HD), 0)
            k8 = jnp.where(sub == 0, jnp.broadcast_to(k, (8, ML_HD)), 0.0)
            v8 = jnp.broadcast_to(v, (8, ML_HD))
            kv = lax.dot_general(k8, v8, TN_DIMS, preferred_element_type=F32, precision=HIGHEST)
            c_out[b, h] = w_inter * c_prev + w_intra * kv
            n_out[b, h:h + 1, :] = w_inter * n_prev + w_intra * k
            m_out[b, :, h:h + 1] = m_t


def _sstep(r3, g3, bg_row, mh_norm_w, c_state, n_state, m_state3, rows_step):
    nb = r3.shape[0]
    d = ML_HEADS * ML_HD
    sec = lambda j: (lambda b: (b, 0, j))
    fixed = lambda b: (0, 0)
    row3 = lambda b: (b, 0, 0)
    state4 = lambda b: (b, 0, 0, 0)
    return pl.pallas_call(
        _sstep_kernel,
        grid=(nb // rows_step,),
        in_specs=[
            pl.BlockSpec((rows_step, 1, d), sec(SEC_MQ)),
            pl.BlockSpec((rows_step, 1, d), sec(SEC_MK)),
            pl.BlockSpec((rows_step, 1, d), sec(SEC_MV)),
            pl.BlockSpec((rows_step, 1, d), sec(SEC_MO)),
            pl.BlockSpec((rows_step, 1, d), sec(SEC_MZ)),
            pl.BlockSpec((rows_step, 1, GATE_LANES), row3),
            pl.BlockSpec((1, GATE_LANES), fixed),
            pl.BlockSpec((1, ML_HD), fixed),
            pl.BlockSpec((rows_step, ML_HEADS, ML_HD, ML_HD), state4),
            pl.BlockSpec((rows_step, ML_HEADS, ML_HD), row3),
            pl.BlockSpec((rows_step, 1, ML_HEADS), row3),
        ],
        out_specs=[
            pl.BlockSpec((rows_step, 1, d), row3),
            pl.BlockSpec((rows_step, ML_HEADS, ML_HD, ML_HD), state4),
            pl.BlockSpec((rows_step, ML_HEADS, ML_HD), row3),
            pl.BlockSpec((rows_step, 1, ML_HEADS), row3),
        ],
        out_shape=[
            jax.ShapeDtypeStruct((nb, 1, d), BF16),
            jax.ShapeDtypeStruct(c_state.shape, F32),
            jax.ShapeDtypeStruct(n_state.shape, F32),
            jax.ShapeDtypeStruct(m_state3.shape, F32),
        ],
        compiler_params=_params("arbitrary"),
        name="mlstm_sample",
    )(r3, r3, r3, r3, r3, g3, bg_row, mh_norm_w, c_state, n_state, m_state3)


def _sgate_kernel(q_ref, kmean_ref, sel_ref):
    n_blocks = kmean_ref.shape[1]
    gate = jnp.sum(kmean_ref[0] * q_ref[...], axis=-1, keepdims=True)
    row = lax.broadcasted_iota(jnp.int32, gate.shape, 0).astype(F32)
    lane = lax.broadcasted_iota(jnp.int32, (ATT_HEADS, GATE_LANES), 1)
    picked = jnp.zeros((ATT_HEADS, GATE_LANES), F32)
    for j in range(MOBA_TOPK):
        best = jnp.max(gate, axis=0, keepdims=True)
        idx = jnp.min(jnp.where(gate == best, row, float(n_blocks)), axis=0, keepdims=True)
        picked = jnp.where(lane == j, idx[0], picked)
        gate = jnp.where(row == idx, NEG_INF, gate)
    sel_ref[0] = picked.astype(jnp.int32)


def _sgate(q_heads, kmean):
    nb, n_blocks = kmean.shape[:2]
    return pl.pallas_call(
        _sgate_kernel,
        grid=(nb,),
        in_specs=[pl.BlockSpec((1, ATT_HEADS, ATT_HD), lambda b: (b, 0, 0)),
                  pl.BlockSpec((1, n_blocks, ATT_HEADS, ATT_HD), lambda b: (b, 0, 0, 0))],
        out_specs=pl.BlockSpec((1, ATT_HEADS, GATE_LANES), lambda b: (b, 0, 0)),
        out_shape=jax.ShapeDtypeStruct((nb, ATT_HEADS, GATE_LANES), jnp.int32),
        compiler_params=_params("arbitrary"),
        name="moba_sample_gate",
    )(q_heads, kmean)


def _sattn_copies(pt_ref, sel_ref, ck_ref, cv_ref, kbuf, vbuf, sems, b, slot):
    ppb = MOBA_BLOCK // PAGE_SIZE
    copies = []
    for h in range(ATT_HEADS):
        for j in range(MOBA_TOPK):
            block = sel_ref[b, h * MOBA_TOPK + j]
            for p in range(ppb):
                page = pt_ref[b, block * ppb + p]
                rows = pl.ds((j * ppb + p) * PAGE_SIZE, PAGE_SIZE)
                copies.append(pltpu.make_async_copy(ck_ref.at[page, :, h, :], kbuf.at[slot, h, rows, :], sems.at[0, slot]))
                copies.append(pltpu.make_async_copy(cv_ref.at[page, :, h, :], vbuf.at[slot, h, rows, :], sems.at[1, slot]))
    return copies


def _sattn_row(q, kn, vn, az, k_sel, v_sel):
    qs = q * (ATT_HD ** -0.5)
    s_new = jnp.sum(qs * kn, axis=-1, keepdims=True)
    sub = lax.broadcasted_iota(jnp.int32, (ATT_HEADS, ATT_HD), 0)
    out = jnp.zeros((ATT_HEADS, ATT_HD), F32)
    for h in range(ATT_HEADS):
        s = jnp.sum(k_sel[h] * qs[h:h + 1, :], axis=-1, keepdims=True)
        sn = s_new[h:h + 1, :]
        m = jnp.maximum(sn, jnp.max(s, axis=0, keepdims=True))
        p = jnp.exp(s - m)
        pn = jnp.exp(sn - m)
        l = pn + jnp.sum(p, axis=0, keepdims=True)
        acc = pn * vn[h:h + 1, :] + jnp.sum(p * v_sel[h], axis=0, keepdims=True)
        out = jnp.where(sub == h, acc / l, out)
    return out * _silu(az.astype(F32))


def _merge_kernel(*refs, n_side_rows, rows_step):
    if n_side_rows:
        pt_ref, sel_ref, refs = refs[0], refs[1], refs[2:]
    x_ref, am_ref, hm_ref, ga_ref, gm_ref, wa_ref, wm_ref, wo_ref = refs[:8]
    if n_side_rows:
        q_ref, kn_ref, vn_ref, az_ref, ck_ref, cv_ref, y_ref, o_ref, kbuf, vbuf, sems, mix_sc = refs[8:]
        side_tiles = []
        fetch = functools.partial(_sattn_copies, pt_ref, sel_ref, ck_ref, cv_ref, kbuf, vbuf, sems)
        step = pl.program_id(0)
        last_block = n_side_rows // rows_step - 1

        def row_of(s, j):
            return jnp.minimum(s, last_block) * rows_step + j

        def after(s, j):
            return (s, j + 1) if j + 1 < rows_step else (s + 1, 0)

        @pl.when(step == 0)
        def _():
            for cp in fetch(row_of(0, 0), 0):
                cp.start()

        for j in range(rows_step):
            cnt = step * rows_step + j
            for cp in fetch(row_of(*after(step, j)), (cnt + 1) % 2):
                cp.start()
            for cp in fetch(row_of(step, j), cnt % 2):
                cp.wait()
            side_out = _sattn_row(q_ref[j], kn_ref[j], vn_ref[j], az_ref[j], kbuf.at[cnt % 2], vbuf.at[cnt % 2])
            o_ref[j] = side_out.astype(BF16)
            side_tiles.append(side_out)
    else:
        y_ref = refs[8]

    y_att = jnp.dot(am_ref[...], wa_ref[...], preferred_element_type=F32)
    y_ml = jnp.dot(hm_ref[...], wm_ref[...], preferred_element_type=F32)
    mix = _sigmoid(ga_ref[...].astype(F32)) * y_att + _sigmoid(gm_ref[...].astype(F32)) * y_ml
    if n_side_rows:
        mix_sc[...] = mix.astype(BF16)
        for tile in side_tiles:
            _pin_before_readers(mix_sc, tile)
        y_ref[...] = x_ref[...] + jnp.dot(mix_sc[...], wo_ref[...], preferred_element_type=F32)

        @pl.when(step == pl.num_programs(0) - 1)
        def _():
            for cp in fetch(row_of(step + 1, 0), ((step + 1) * rows_step) % 2):
                cp.wait()
    else:
        y_ref[...] = x_ref[...] + jnp.dot(mix.astype(BF16), wo_ref[...], preferred_element_type=F32)


def _merge(x, am, hm, r, wa, wm, wo, tm, side=None):
    t_rows, d = x.shape
    n_steps = t_rows // tm
    row = lambda i, *_: (i, 0)
    fixed = lambda i, *_: (0, 0)
    in_specs = [
        pl.BlockSpec((tm, d), row),
        pl.BlockSpec((tm, d), row),
        pl.BlockSpec((tm, d), row),
        pl.BlockSpec((tm, d), lambda i, *_: (i, SEC_GA)),
        pl.BlockSpec((tm, d), lambda i, *_: (i, SEC_GM)),
        pl.BlockSpec((d, d), fixed),
        pl.BlockSpec((d, d), fixed),
        pl.BlockSpec((d, d), fixed),
    ]
    y_spec = pl.BlockSpec((tm, d), row)
    y_shape = jax.ShapeDtypeStruct((t_rows, d), F32)
    if side is None:
        return pl.pallas_call(
            functools.partial(_merge_kernel, n_side_rows=0, rows_step=0),
            grid=(n_steps,), in_specs=in_specs, out_specs=y_spec, out_shape=y_shape,
            compiler_params=_params("arbitrary"), name="merge",
        )(x, am, hm, r, r, wa, wm, wo)

    page_table, sel, q_heads, k_heads, v_heads, az_heads, cache_k, cache_v = side
    n_rows = q_heads.shape[0]
    rows_step = next(c for c in range(1, n_rows + 1) if n_rows % c == 0 and n_rows // c <= n_steps)
    n_blocks = n_rows // rows_step
    n_keys = MOBA_TOPK * MOBA_BLOCK
    head_spec = pl.BlockSpec((rows_step, ATT_HEADS, ATT_HD), lambda i, *_: (jnp.minimum(i, n_blocks - 1), 0, 0))
    return pl.pallas_call(
        functools.partial(_merge_kernel, n_side_rows=n_rows, rows_step=rows_step),
        grid_spec=pltpu.PrefetchScalarGridSpec(
            num_scalar_prefetch=2,
            grid=(n_steps,),
            in_specs=in_specs + [head_spec, head_spec, head_spec, head_spec,
                                 pl.BlockSpec(memory_space=pl.ANY), pl.BlockSpec(memory_space=pl.ANY)],
            out_specs=[y_spec, head_spec],
            scratch_shapes=[
                pltpu.VMEM((2, ATT_HEADS, n_keys, ATT_HD), F32),
                pltpu.VMEM((2, ATT_HEADS, n_keys, ATT_HD), F32),
                pltpu.SemaphoreType.DMA((2, 2)),
                pltpu.VMEM((tm, d), BF16),
            ],
        ),
        out_shape=[y_shape, jax.ShapeDtypeStruct((n_rows, ATT_HEADS, ATT_HD), BF16)],
        compiler_params=_params("arbitrary"),
        name="merge_with_sample_attend",
    )(page_table, sel, x, am, hm, r, r, wa, wm, wo, q_heads, k_heads, v_heads, az_heads, cache_k, cache_v)


def _row_tile(rows, target):
    tm = min(rows, target)
    while rows % tm:
        tm //= 2
    return tm


def kernel(x_prompt, x_sample, cache_k, cache_v, page_table, state_mlstm_C, state_mlstm_n, state_mlstm_m,
           norm_w, w_in, b_gates, q_norm_w, k_norm_w, mh_norm_w, w_proj_attn, w_proj_mlstm, w_out):
    batch, seq, d = x_prompt.shape
    dec_batch, dec_seq, _ = x_sample.shape
    assert dec_seq == 1 and d == ATT_HEADS * ATT_HD == ML_HEADS * ML_HD
    depth = w_in.shape[0]
    n_pages = page_table.shape[1]
    past = n_pages * PAGE_SIZE
    assert past % MOBA_BLOCK == 0 and seq % MOBA_BLOCK == 0
    t_p, t_s = batch * seq, dec_batch * dec_seq

    tm_proj = _row_tile(seq, 256)
    chunk = _row_tile(seq, 256)
    rope_p = _rope_tables(jnp.arange(seq, dtype=jnp.int32))
    rope_s = _rope_tables(jnp.tile(past + jnp.arange(dec_seq, dtype=jnp.int32), dec_batch))

    ppb = MOBA_BLOCK // PAGE_SIZE
    steps = (t_p // tm_proj, batch * ATT_HEADS, batch * (seq // chunk))
    weights = (3, 1, 4)
    total_pages = dec_batch * n_pages
    unit = -(-total_pages // (sum(s * w for s, w in zip(steps, weights)) * ppb)) * ppb
    share = tuple(unit * w for w in weights)
    first = (0, steps[0] * share[0], steps[0] * share[0] + steps[1] * share[1])
    capacity = first[2] + steps[2] * share[2]
    pt_flat = jnp.pad(page_table.reshape(-1), (0, capacity - total_pages))

    y_p = x_prompt.reshape(t_p, d)
    y_s = x_sample.reshape(t_s, d)
    outs = [[] for _ in range(10)]
    for l in range(depth):
        w_wide = _cast_wide_weight(w_in, l, d)
        w_gate2 = _split_gate_weight(w_in[l][:, N_WIDE * d:])
        bg_row = jnp.pad(b_gates[l].astype(F32), (0, GATE_LANES - 2 * ML_HEADS)).reshape(1, GATE_LANES)
        bg_col = bg_row.reshape(GATE_LANES, 1)
        nw = norm_w[l].reshape(1, d)
        qnw, knw = q_norm_w[l].reshape(1, ATT_HD), k_norm_w[l].reshape(1, ATT_HD)
        mhw = mh_norm_w[l].reshape(1, ML_HD)
        wa, wm, wo = w_proj_attn[l].astype(BF16), w_proj_mlstm[l].astype(BF16), w_out[l].astype(BF16)

        q, k, v, r, g, gt, means0 = _inproj(pt_flat, y_p, nw, w_wide, w_gate2, rope_p, qnw, knw, cache_k[l],
                                            tm_proj, seq // tm_proj, share[0], first[0])
        am, means1 = _moba(pt_flat, q, k, v, r, cache_k[l], batch, seq, share[1], first[1])
        hm, c_new, n_new, m_new, means2 = _mlstm(pt_flat, r, g, gt, bg_row, bg_col, mhw, cache_k[l], batch, seq, chunk,
                                                 share[2], first[2])
        for dst, val in zip(outs[:5], (k.reshape(batch, seq, ATT_HEADS, ATT_HD), v.reshape(batch, seq, ATT_HEADS, ATT_HD),
                                       c_new, n_new, m_new[:, :, 0])):
            dst.append(val)
        kmean = jnp.concatenate([means0, means1, means2], axis=0)[:total_pages // ppb]
        kmean = kmean.reshape(dec_batch, n_pages // ppb, ATT_HEADS, ATT_HD)
        r_p = r

        q, k, v, r, g, _ = _inproj(pt_flat, y_s, nw, w_wide, w_gate2, rope_s, qnw, knw, cache_k[l], t_s, 1, 0, 0)
        r3 = r.reshape(t_s, 1, r.shape[-1])
        by_head = lambda t: t.reshape(t_s, ATT_HEADS, ATT_HD)
        sel = _sgate(by_head(q), kmean)
        sel = sel[:, :, :MOBA_TOPK].reshape(t_s, ATT_HEADS * MOBA_TOPK)
        y_p, am_s = _merge(y_p, am, hm, r_p, wa, wm, wo, _row_tile(t_p, 512),
                           side=(page_table, sel, by_head(q), by_head(k), by_head(v),
                                 by_head(r[:, SEC_AZ * d:(SEC_AZ + 1) * d]), cache_k[l], cache_v[l]))
        am = am_s
        hm, c_new, n_new, m_new = _sstep(
            r3, g.reshape(t_s, 1, GATE_LANES), bg_row, mhw,
            state_mlstm_C[l], state_mlstm_n[l], state_mlstm_m[l].reshape(dec_batch, 1, ML_HEADS), _row_tile(t_s, 4))
        y_s = _merge(y_s, am.reshape(t_s, d), hm.reshape(t_s, d), r, wa, wm, wo, t_s)
        for dst, val in zip(outs[5:], (k.reshape(dec_batch, dec_seq, ATT_HEADS, ATT_HD),
                                       v.reshape(dec_batch, dec_seq, ATT_HEADS, ATT_HD),
                                       c_new, n_new, m_new.reshape(dec_batch, ML_HEADS))):
            dst.append(val)

    st = state_mlstm_C.dtype
    k_p, v_p, c_p, n_p, m_p, k_s, v_s, c_s, n_s, m_s = (jnp.stack(o) for o in outs)
    return (y_p.reshape(batch, seq, d), y_s.reshape(dec_batch, dec_seq, d),
            k_p, v_p, c_p.astype(st), n_p.astype(st), m_p.astype(st),
            k_s, v_s, c_s.astype(st), n_s.astype(st), m_s.astype(st))
```

```python
import functools

import jax
import jax.numpy as jnp
from jax import lax
from jax.experimental import pallas as pl
from jax.experimental.pallas import tpu as pltpu

F32 = jnp.float32
BF16 = jnp.bfloat16
HIGHEST = lax.Precision.HIGHEST
NEG_INF = float("-inf")

ATT_HEADS = 8
ATT_HD = 128
ROT_DIM = ATT_HD // 4
ROPE_THETA = 500000.0
MOBA_BLOCK = 256
MOBA_TOPK = 3
ML_HEADS = 4
ML_HD = 256
PAGE_SIZE = 128
EPS = 1e-6
N_WIDE = 11
N_ROW_SECTIONS = 7
SEC_MQ, SEC_MK, SEC_MV, SEC_MO, SEC_MZ, SEC_GA, SEC_GM = range(N_ROW_SECTIONS)
GATE_LANES = 128
N_INPROJ_OUTS = 10
LOG2_E = 1.4426950408889634
KEY_TILE = 256

VMEM_LIMIT_BYTES = 60000 * 1024

NT_DIMS = (((1,), (1,)), ((), ()))
TN_DIMS = (((0,), (0,)), ((), ()))


def _sigmoid(x):
    return 1.0 / (1.0 + jnp.exp(-x))


def _silu(x):
    return x * _sigmoid(x)


def _log_sigmoid(x):
    return jnp.minimum(x, 0.0) - jnp.log1p(jnp.exp(-jnp.abs(x)))


def _params(*semantics):
    return pltpu.CompilerParams(dimension_semantics=semantics, vmem_limit_bytes=VMEM_LIMIT_BYTES)


def _head_norm_rope(t, w, rc, rs1, rs2):
    half = ROT_DIM // 2
    outs = []
    for h in range(ATT_HEADS):
        th = t[:, h * ATT_HD:(h + 1) * ATT_HD]
        y = th * lax.rsqrt(jnp.mean(th * th, axis=-1, keepdims=True) + EPS) * w
        up = pltpu.roll(y, ATT_HD - half, axis=1)
        down = pltpu.roll(y, half, axis=1)
        outs.append(y * rc + up * rs1 + down * rs2)
    return outs


def _cast_kernel(x_ref, o_ref):
    o_ref[...] = x_ref[...].T.astype(o_ref.dtype)


def _cast_wide_weight(w_in, layer, d):
    return pl.pallas_call(
        _cast_kernel,
        grid=(N_WIDE,),
        in_specs=[pl.BlockSpec((None, d, d), lambda j: (layer, j, 0))],
        out_specs=pl.BlockSpec((None, d, d), lambda j: (j, 0, 0)),
        out_shape=jax.ShapeDtypeStruct((N_WIDE, d, d), BF16),
        compiler_params=_params("arbitrary"),
        name="cast_weight",
    )(jnp.swapaxes(w_in, 1, 2))


def _page_specs(n_step, first_page, step_of):
    def spec(p):
        return pl.BlockSpec((1, PAGE_SIZE, ATT_HEADS, ATT_HD),
                            lambda *ids: (ids[-1][first_page + step_of(*ids[:-1]) * n_step + p], 0, 0, 0))
    return [spec(p) for p in range(n_step)]


def _block_means_steps(page_refs, out_ref):
    ppb = MOBA_BLOCK // PAGE_SIZE
    for j in range(len(page_refs) // ppb):
        tot = jnp.sum(page_refs[ppb * j][0], axis=0)
        for p in range(1, ppb):
            tot = tot + jnp.sum(page_refs[ppb * j + p][0], axis=0)
        out_ref[j] = tot * (1.0 / MOBA_BLOCK)
        yield


def _block_means(page_refs, out_ref):
    for _ in _block_means_steps(page_refs, out_ref):
        pass


def _emit_round_robin(streams):
    streams = list(streams)
    while streams:
        for g in list(streams):
            if next(g, "done") == "done":
                streams.remove(g)


def _means_out(n_step, n_steps, step_of):
    ppb = MOBA_BLOCK // PAGE_SIZE
    spec = pl.BlockSpec((n_step // ppb, ATT_HEADS, ATT_HD), lambda *ids: (step_of(*ids[:-1]), 0, 0))
    return spec, jax.ShapeDtypeStruct((n_steps * n_step // ppb, ATT_HEADS, ATT_HD), F32)


def _pin_before_readers(anchor_ref, value):
    zero = pltpu.bitcast(lax.shift_right_logical(pltpu.bitcast(value, jnp.uint32), jnp.uint32(32)), F32)
    zero = jnp.concatenate([zero, zero], axis=0).astype(anchor_ref.dtype)
    anchor_ref[0:16, 0:GATE_LANES] = anchor_ref[0:16, 0:GATE_LANES] + zero


def _inproj_kernel(pt_ref, x_ref, nw_ref, wg_ref, rc_ref, rs1_ref, rs2_ref, qnw_ref, knw_ref, *refs, n_pages):
    del pt_ref
    w_refs, refs = refs[:N_WIDE], refs[N_WIDE:]
    q_ref, k_ref, kb_ref, km_ref, v_ref, vt_ref, az_ref, r_ref, g_ref, gt_ref = refs[n_pages:n_pages + N_INPROJ_OUTS]
    xb_sc = refs[-1]
    d = x_ref.shape[1]
    x = x_ref[...]
    xn = x * lax.rsqrt(jnp.mean(x * x, axis=-1, keepdims=True) + EPS) * nw_ref[...]
    xb = xn.astype(BF16)
    xb_sc[...] = xb

    ppb = MOBA_BLOCK // PAGE_SIZE
    page_blocks = [(refs[ppb * j:ppb * (j + 1)], refs[n_pages + N_INPROJ_OUTS].at[pl.ds(j, 1)])
                   for j in range(n_pages // ppb)]

    def section(i):
        share = -(-len(page_blocks) // N_WIDE)
        for pages, out in page_blocks[i * share:(i + 1) * share]:
            _block_means(pages, out)
            _pin_before_readers(xb_sc, out[0])
        return jnp.dot(xb_sc[...], w_refs[i][...], preferred_element_type=F32)

    rc, rs1, rs2 = rc_ref[...], rs1_ref[...], rs2_ref[...]
    head = lambda h: slice(h * ATT_HD, (h + 1) * ATT_HD)
    for h, t in enumerate(_head_norm_rope(section(0), qnw_ref[...], rc, rs1, rs2)):
        q_ref[h] = t
    for h, t in enumerate(_head_norm_rope(section(1), knw_ref[...], rc, rs1, rs2)):
        k_ref[:, head(h)] = t
        kb_ref[h] = t.astype(BF16)
        km_ref[h, 0] = jnp.broadcast_to(jnp.sum(t, axis=0, keepdims=True) * (1.0 / t.shape[0]), (8, ATT_HD))
    v = section(2)
    v_ref[...] = v
    az = section(3)
    for h in range(ATT_HEADS):
        vt_ref[h] = v[:, head(h)].T.astype(BF16)
        az_ref[h] = az[:, head(h)].astype(BF16)
    for j in range(N_ROW_SECTIONS):
        t = section(4 + j)
        if j == SEC_MK:
            t = t * (ML_HD ** -0.5)
        r_ref[:, j * d:(j + 1) * d] = t.astype(BF16)
    n_gate = 2 * ML_HEADS
    x_lo = (xn - xb.astype(F32)).astype(BF16)
    both = (jnp.dot(xb_sc[...], wg_ref[...], preferred_element_type=F32)
            + jnp.dot(x_lo, wg_ref[...], preferred_element_type=F32))
    lane = lax.broadcasted_iota(jnp.int32, both.shape, 1)
    g = jnp.where(lane < n_gate, both + pltpu.roll(both, GATE_LANES - n_gate, axis=1), 0.0)
    g_ref[...] = g
    gt_ref[...] = g.T


def _split_gate_weight(w_gate):
    hi = w_gate.astype(BF16)
    lo = (w_gate - hi.astype(F32)).astype(BF16)
    return jnp.pad(jnp.concatenate([hi, lo], axis=1), ((0, 0), (0, GATE_LANES - 2 * w_gate.shape[1])))


def _inproj(page_table_flat, x, norm_w, w_wide, w_gate2, rope, q_norm_w, k_norm_w, cache_k, tm, n_pos_blocks,
            pages_step, first_page):
    t_rows, d = x.shape
    rc, rs1, rs2 = rope
    n_steps = t_rows // tm
    row = lambda i, pt: (i, 0)
    fixed = lambda i, pt: (0, 0)
    pos = lambda i, pt: (i % n_pos_blocks, 0)
    once = pl.Buffered(1)
    step_of = lambda i: i
    wide_section = lambda j: pl.BlockSpec((None, d, d), lambda i, pt: (j, 0, 0), pipeline_mode=once)
    by_head = pl.BlockSpec((ATT_HEADS, tm, ATT_HD), lambda i, pt: (0, i, 0))
    out_specs = [
        by_head,
        pl.BlockSpec((tm, d), row),
        by_head,
        pl.BlockSpec((ATT_HEADS, 1, 8, ATT_HD), lambda i, pt: (0, i, 0, 0)),
        pl.BlockSpec((tm, d), row),
        pl.BlockSpec((ATT_HEADS, ATT_HD, tm), lambda i, pt: (0, 0, i)),
        by_head,
        pl.BlockSpec((tm, N_ROW_SECTIONS * d), row),
        pl.BlockSpec((tm, GATE_LANES), row),
        pl.BlockSpec((GATE_LANES, tm), lambda i, pt: (0, i)),
    ]
    out_shape = [
        jax.ShapeDtypeStruct((ATT_HEADS, t_rows, ATT_HD), F32),
        jax.ShapeDtypeStruct((t_rows, d), F32),
        jax.ShapeDtypeStruct((ATT_HEADS, t_rows, ATT_HD), BF16),
        jax.ShapeDtypeStruct((ATT_HEADS, n_steps, 8, ATT_HD), F32),
        jax.ShapeDtypeStruct((t_rows, d), F32),
        jax.ShapeDtypeStruct((ATT_HEADS, ATT_HD, t_rows), BF16),
        jax.ShapeDtypeStruct((ATT_HEADS, t_rows, ATT_HD), BF16),
        jax.ShapeDtypeStruct((t_rows, N_ROW_SECTIONS * d), BF16),
        jax.ShapeDtypeStruct((t_rows, GATE_LANES), F32),
        jax.ShapeDtypeStruct((GATE_LANES, t_rows), F32),
    ]
    assert len(out_specs) == N_INPROJ_OUTS
    if pages_step:
        spec, shape = _means_out(pages_step, n_steps, step_of)
        out_specs.append(spec)
        out_shape.append(shape)
    return pl.pallas_call(
        functools.partial(_inproj_kernel, n_pages=pages_step),
        grid_spec=pltpu.PrefetchScalarGridSpec(
            num_scalar_prefetch=1,
            grid=(n_steps,),
            in_specs=[
                pl.BlockSpec((tm, d), row),
                pl.BlockSpec((1, d), fixed),
                pl.BlockSpec((d, GATE_LANES), fixed, pipeline_mode=once),
                pl.BlockSpec((tm, ATT_HD), pos),
                pl.BlockSpec((tm, ATT_HD), pos),
                pl.BlockSpec((tm, ATT_HD), pos),
                pl.BlockSpec((1, ATT_HD), fixed),
                pl.BlockSpec((1, ATT_HD), fixed),
            ] + [wide_section(j) for j in range(N_WIDE)] + _page_specs(pages_step, first_page, step_of),
            out_specs=out_specs,
            scratch_shapes=[pltpu.VMEM((tm, d), BF16)],
        ),
        out_shape=out_shape,
        compiler_params=_params("arbitrary"),
        name="inproj",
    )(page_table_flat, x, norm_w, w_gate2, rc, rs1, rs2, q_norm_w, k_norm_w, *([w_wide] * N_WIDE),
      *([cache_k] * pages_step))


def _rope_tables(pos):
    half = ROT_DIM // 2
    inv = ROPE_THETA ** (-(jnp.arange(half, dtype=F32) * 2.0) / ROT_DIM)
    ang = pos.astype(F32)[:, None] * inv[None, :]
    cos, sin = jnp.cos(ang), jnp.sin(ang)
    n = pos.shape[0]
    zeros = jnp.zeros((n, half), F32)
    tail0 = jnp.zeros((n, ATT_HD - ROT_DIM), F32)
    rc = jnp.concatenate([cos, cos, jnp.ones((n, ATT_HD - ROT_DIM), F32)], axis=-1)
    rs1 = jnp.concatenate([-sin, zeros, tail0], axis=-1)
    rs2 = jnp.concatenate([zeros, sin, tail0], axis=-1)
    return rc, rs1, rs2


def _moba_kernel(pt_ref, q_ref, kb_ref, vt_ref, az_ref, km_ref, *refs, n_pages):
    del pt_ref
    o_ref, means_ref, s_sc, p_sc, qs_sc = refs[n_pages:]
    blk = MOBA_BLOCK
    nb = q_ref.shape[0] // blk
    nb_pad = -(-nb // 8) * 8
    rows = lambda j: slice(j * blk, (j + 1) * blk)

    sub8 = lax.broadcasted_iota(jnp.int32, (8, ATT_HD), 0)
    groups = []
    for g0 in range(0, nb, 8):
        tile = jnp.zeros((8, ATT_HD), F32)
        for j in range(g0, min(g0 + 8, nb)):
            tile = jnp.where(sub8 == j - g0, km_ref[j], tile)
        groups.append(tile)
    kmean = groups[0] if len(groups) == 1 else jnp.concatenate(groups, axis=0)

    blk_id = lax.broadcasted_iota(jnp.int32, (nb_pad, blk), 0)
    key_ix = lax.broadcasted_iota(jnp.int32, (blk, blk), 0)
    qry_ix = lax.broadcasted_iota(jnp.int32, (blk, blk), 1)
    causal = key_ix <= qry_ix

    tiles_per_blk = blk // KEY_TILE
    fold = lambda t: t.reshape(KEY_TILE // 8, 8, blk)
    state = {}

    def pass1(c):
        q = q_ref[rows(c), :]
        gate = lax.dot_general(kmean, q, NT_DIMS, preferred_element_type=F32, precision=HIGHEST)
        past = blk_id < c
        gate = jnp.where(past, gate, NEG_INF)
        beaten = jnp.zeros(gate.shape, F32)
        for m in range(c):
            gm = gate[m:m + 1, :]
            wins = jnp.where(gm > gate, 1.0, jnp.where(gm == gate, jnp.where(blk_id > m, 1.0, 0.0), 0.0))
            beaten = beaten + wins
        bias = jnp.where(past, jnp.where(beaten < MOBA_TOPK, 0.0, NEG_INF), NEG_INF)
        qs_sc[c % 2] = (q * (ATT_HD ** -0.5 * LOG2_E)).astype(BF16)
        yield
        for j in range(c + 1):
            for t in range(tiles_per_blk):
                keys = slice(j * blk + t * KEY_TILE, j * blk + (t + 1) * KEY_TILE)
                s = lax.dot_general(kb_ref[keys, :], qs_sc[c % 2], NT_DIMS, preferred_element_type=F32)
                if j == c:
                    s = jnp.where(causal[t * KEY_TILE:(t + 1) * KEY_TILE, :], s, NEG_INF)
                else:
                    s = s + bias[j:j + 1, :]
                s_sc[c % 2, keys, :] = s
                s_max = jnp.max(fold(s), axis=0)
                state[c] = s_max if c not in state else jnp.maximum(state[c], s_max)
                yield

    def pass2(c):
        m_col = jnp.max(state[c], axis=0, keepdims=True)
        n_keys = (c + 1) * blk
        l_acc = jnp.zeros((8, blk), F32)
        for t in range(n_keys // KEY_TILE):
            keys = slice(t * KEY_TILE, (t + 1) * KEY_TILE)
            p = jnp.exp2(s_sc[c % 2, keys, :] - m_col)
            l_acc = l_acc + jnp.sum(fold(p), axis=0)
            p_sc[c % 2, keys, :] = p.astype(BF16)
            yield
        l_col = jnp.sum(l_acc, axis=0, keepdims=True)
        acc = jnp.dot(vt_ref[:, :n_keys], p_sc[c % 2, :n_keys, :], preferred_element_type=F32)
        out = (acc / l_col).T
        o_ref[rows(c), :] = (out * _silu(az_ref[rows(c), :].astype(F32))).astype(BF16)
        yield

    ppb = MOBA_BLOCK // PAGE_SIZE
    page_refs = refs[:n_pages]
    rounds = max(nb - 1, 1)
    per_round = -(-n_pages // (rounds * ppb)) * ppb

    def pinned_means(lo, hi, anchor):
        for j in range(lo // ppb, hi // ppb):
            out = means_ref.at[pl.ds(j, 1)]
            _block_means(page_refs[ppb * j:ppb * (j + 1)], out)
            _pin_before_readers(anchor, out[0])
            yield

    _emit_round_robin([pass1(0)])
    for c in range(nb):
        streams = [pass2(c)]
        if c + 1 < nb:
            streams.append(pass1(c + 1))
        lo, hi = min(c * per_round, n_pages), min((c + 1) * per_round, n_pages)
        if hi > lo:
            streams.append(pinned_means(lo, hi, qs_sc.at[(c + 1) % 2 if c + 1 < nb else c % 2]))
        _emit_round_robin(streams)


def _moba(page_table_flat, q_h, kb_h, vt_h, az_h, km, cache_k, batch, seq, pages_step, first_page):
    n_heads, t_rows, _ = q_h.shape
    blk = MOBA_BLOCK
    nb = seq // blk
    rows_of = lambda b, h, pt: (h, b, 0)
    step_of = lambda b, h: b * ATT_HEADS + h
    means_spec, means_shape = _means_out(pages_step, batch * ATT_HEADS, step_of)
    return pl.pallas_call(
        functools.partial(_moba_kernel, n_pages=pages_step),
        grid_spec=pltpu.PrefetchScalarGridSpec(
            num_scalar_prefetch=1,
            grid=(batch, ATT_HEADS),
            in_specs=[
                pl.BlockSpec((None, seq, ATT_HD), rows_of),
                pl.BlockSpec((None, seq, ATT_HD), rows_of),
                pl.BlockSpec((None, ATT_HD, seq), lambda b, h, pt: (h, 0, b)),
                pl.BlockSpec((None, seq, ATT_HD), rows_of),
                pl.BlockSpec((None, nb, 8, ATT_HD), lambda b, h, pt: (h, b, 0, 0)),
            ] + _page_specs(pages_step, first_page, step_of),
            out_specs=[pl.BlockSpec((None, seq, ATT_HD), rows_of), means_spec],
            scratch_shapes=[
                pltpu.VMEM((2, seq, blk), F32),
                pltpu.VMEM((2, seq, blk), BF16),
                pltpu.VMEM((2, blk, ATT_HD), BF16),
            ],
        ),
        out_shape=[jax.ShapeDtypeStruct((n_heads, t_rows, ATT_HD), BF16), means_shape],
        compiler_params=_params("arbitrary", "arbitrary"),
        name="moba_prompt",
    )(page_table_flat, q_h, kb_h, vt_h, az_h, km, *([cache_k] * pages_step))


def _mlstm_post(h, mo, mz, nw):
    h = h * _sigmoid(mo)
    h = h * lax.rsqrt(jnp.mean(h * h, axis=-1, keepdims=True) + EPS) * nw
    return h * _silu(mz)


def _dot_rows_hi_lo(mat, row, transposed):
    rows = jnp.broadcast_to(row, (8, row.shape[1]))
    hi = rows.astype(BF16)
    lo = (rows - hi.astype(F32)).astype(BF16)
    if transposed:
        return (lax.dot_general(mat, hi, NT_DIMS, preferred_element_type=F32)
                + lax.dot_general(mat, lo, NT_DIMS, preferred_element_type=F32))
    return jnp.dot(hi, mat, preferred_element_type=F32) + jnp.dot(lo, mat, preferred_element_type=F32)


def _mlstm_kernel(pt_ref, q_ref, k_ref, v_ref, mo_ref, mz_ref, g_ref, gt_ref, bgr_ref, bgc_ref, nw_ref, *refs,
                  n_pages):
    del pt_ref
    hm_ref, c_out, n_out, m_out, means_ref, c_sc, n_sc, m_sc = refs[n_pages:]
    c = pl.program_id(1)
    chunk = q_ref.shape[0]

    @pl.when(c == 0)
    def _():
        c_sc[...] = jnp.zeros(c_sc.shape, F32)
        n_sc[...] = jnp.zeros(n_sc.shape, F32)
        m_sc[...] = jnp.zeros(m_sc.shape, F32)

    _block_means(refs[:n_pages], means_ref)

    gc = g_ref[...] + bgr_ref[...]
    lane = lax.broadcasted_iota(jnp.int32, gc.shape, 1)
    gc = jnp.where(lane >= ML_HEADS, _log_sigmoid(gc), gc)
    gr = gt_ref[0:2 * ML_HEADS, :] + bgc_ref[0:2 * ML_HEADS, :]
    sub = lax.broadcasted_iota(jnp.int32, gr.shape, 0)
    gr = jnp.where(sub >= ML_HEADS, _log_sigmoid(gr), gr)

    t_ix = lax.broadcasted_iota(jnp.int32, (chunk, chunk), 0)
    s_ix = lax.broadcasted_iota(jnp.int32, (chunk, chunk), 1)
    causal = s_ix <= t_ix
    tril = jnp.where(causal, 1.0, 0.0)
    bc = jnp.dot(tril, gc, preferred_element_type=F32, precision=HIGHEST)
    br = lax.dot_general(gr, tril, NT_DIMS, preferred_element_type=F32, precision=HIGHEST)

    for h in range(ML_HEADS):
        cols = slice(h * ML_HD, (h + 1) * ML_HD)
        q, k, v = q_ref[:, cols], k_ref[:, cols], v_ref[:, cols]
        ig_r, b_r = gr[h:h + 1, :], br[ML_HEADS + h:ML_HEADS + h + 1, :]
        ig_c, b_c = gc[:, h:h + 1], bc[:, ML_HEADS + h:ML_HEADS + h + 1]
        m_prev = m_sc[h:h + 1, 0:1]
        c_prev = c_sc[h]
        n_prev = n_sc[h:h + 1, :]

        log_w = jnp.where(causal, b_c - b_r + ig_r, NEG_INF)
        log_inter = b_c + m_prev
        m_t = jnp.maximum(log_inter, jnp.max(log_w, axis=-1, keepdims=True))
        w_intra = jnp.exp(log_w - m_t)
        w_inter = jnp.exp(log_inter - m_t)
        s = lax.dot_general(q, k, NT_DIMS, preferred_element_type=F32) * w_intra
        num = (w_inter * jnp.dot(q, c_prev.astype(BF16), preferred_element_type=F32)
               + jnp.dot(s.astype(BF16), v, preferred_element_type=F32))
        qn = jnp.sum(q.astype(F32) * n_prev, axis=-1, keepdims=True)
        den = w_inter * qn + jnp.sum(s, axis=-1, keepdims=True)
        hh = num / jnp.maximum(jnp.abs(den), jnp.exp(-m_t))

        m_new = m_t[chunk - 1:chunk, :]
        b_last = b_c[chunk - 1:chunk, :]
        a_prev = jnp.exp(b_last + m_prev - m_new)
        a_c = jnp.exp(b_last - b_c + ig_c - m_new)
        a_r = jnp.exp(b_last - b_r + ig_r - m_new)
        kf = k.astype(F32)
        ka = (kf * a_c).astype(BF16)
        c_sc[h] = a_prev * c_prev + lax.dot_general(ka, v, TN_DIMS, preferred_element_type=F32)
        a_rows = jnp.broadcast_to(a_r, (8, chunk))
        n_sc[h:h + 1, :] = a_prev * n_prev + jnp.dot(
            a_rows, kf, preferred_element_type=F32, precision=HIGHEST)[0:1, :]
        m_sc[h:h + 1, :] = jnp.broadcast_to(m_new, (1, m_sc.shape[1]))

        hm_ref[:, cols] = _mlstm_post(
            hh, mo_ref[:, cols].astype(F32), mz_ref[:, cols].astype(F32), nw_ref[...]).astype(BF16)

    @pl.when(c == pl.num_programs(1) - 1)
    def _():
        c_out[0] = c_sc[...]
        n_out[0] = n_sc[...]
        m_out[0] = m_sc[...]


def _mlstm(page_table_flat, r, g, gt, bg_row, bg_col, mh_norm_w, cache_k, batch, seq, chunk, pages_step, first_page):
    t_rows = r.shape[0]
    d = ML_HEADS * ML_HD
    nc = seq // chunk
    sec = lambda j: (lambda b, c, pt: (b * nc + c, j))
    fixed = lambda b, c, pt: (0, 0)
    state = lambda b, c, pt: (b, 0, 0)
    step_of = lambda b, c: b * nc + c
    means_spec, means_shape = _means_out(pages_step, batch * nc, step_of)
    return pl.pallas_call(
        functools.partial(_mlstm_kernel, n_pages=pages_step),
        grid_spec=pltpu.PrefetchScalarGridSpec(
            num_scalar_prefetch=1,
            grid=(batch, nc),
            in_specs=[
                pl.BlockSpec((chunk, d), sec(SEC_MQ)),
                pl.BlockSpec((chunk, d), sec(SEC_MK)),
                pl.BlockSpec((chunk, d), sec(SEC_MV)),
                pl.BlockSpec((chunk, d), sec(SEC_MO)),
                pl.BlockSpec((chunk, d), sec(SEC_MZ)),
                pl.BlockSpec((chunk, GATE_LANES), lambda b, c, pt: (b * nc + c, 0)),
                pl.BlockSpec((GATE_LANES, chunk), lambda b, c, pt: (0, b * nc + c)),
                pl.BlockSpec((1, GATE_LANES), fixed),
                pl.BlockSpec((GATE_LANES, 1), fixed),
                pl.BlockSpec((1, ML_HD), fixed),
            ] + _page_specs(pages_step, first_page, step_of),
            out_specs=[
                pl.BlockSpec((chunk, d), lambda b, c, pt: (b * nc + c, 0)),
                pl.BlockSpec((1, ML_HEADS, ML_HD, ML_HD), lambda b, c, pt: (b, 0, 0, 0)),
                pl.BlockSpec((1, ML_HEADS, ML_HD), state),
                pl.BlockSpec((1, ML_HEADS, GATE_LANES), state),
                means_spec,
            ],
            scratch_shapes=[
                pltpu.VMEM((ML_HEADS, ML_HD, ML_HD), F32),
                pltpu.VMEM((ML_HEADS, ML_HD), F32),
                pltpu.VMEM((ML_HEADS, GATE_LANES), F32),
            ],
        ),
        out_shape=[
            jax.ShapeDtypeStruct((t_rows, d), BF16),
            jax.ShapeDtypeStruct((batch, ML_HEADS, ML_HD, ML_HD), F32),
            jax.ShapeDtypeStruct((batch, ML_HEADS, ML_HD), F32),
            jax.ShapeDtypeStruct((batch, ML_HEADS, GATE_LANES), F32),
            means_shape,
        ],
        compiler_params=_params("arbitrary", "arbitrary"),
        name="mlstm_prompt",
    )(page_table_flat, r, r, r, r, r, g, gt, bg_row, bg_col, mh_norm_w, *([cache_k] * pages_step))


def _sstep_kernel(q_ref, k_ref, v_ref, mo_ref, mz_ref, g_ref, bgr_ref, nw_ref, c_ref, n_ref, m_ref,
                  hm_ref, c_out, n_out, m_out):
    for b in range(q_ref.shape[0]):
        gates = g_ref[b] + bgr_ref[...]
        for h in range(ML_HEADS):
            cols = slice(h * ML_HD, (h + 1) * ML_HD)
            q = q_ref[b, :, cols].astype(F32)
            k = k_ref[b, :, cols].astype(F32)
            v = v_ref[b, :, cols].astype(F32)
            ig = gates[:, h:h + 1]
            lf = _log_sigmoid(gates[:, ML_HEADS + h:ML_HEADS + h + 1])
            m_prev = m_ref[b, :, h:h + 1]
            c_prev = c_ref[b, h]
            n_prev = n_ref[b, h:h + 1, :]

            log_inter = lf + m_prev
            m_t = jnp.maximum(log_inter, ig)
            w_intra = jnp.exp(ig - m_t)
            w_inter = jnp.exp(log_inter - m_t)
            s = jnp.sum(q * k, axis=-1, keepdims=True) * w_intra
            qc = _dot_rows_hi_lo(c_prev.astype(BF16), q, transposed=False)[0:1, :]
            num = w_inter * qc + s * v
            den = w_inter * jnp.sum(q * n_prev, axis=-1, keepdims=True) + s
            hh = num / jnp.maximum(jnp.abs(den), jnp.exp(-m_t))
            hm_ref[b, :, cols] = _mlstm_post(
                hh, mo_ref[b, :, cols].astype(F32), mz_ref[b, :, cols].astype(F32), nw_ref[...]).astype(BF16)

            sub = lax.broadcasted_iota(jnp.int32, (8, ML_HD), 0)
            k8 = jnp.where(sub == 0, jnp.broadcast_to(k, (8, ML_HD)), 0.0)
            v8 = jnp.broadcast_to(v, (8, ML_HD))
            kv = lax.dot_general(k8, v8, TN_DIMS, preferred_element_type=F32, precision=HIGHEST)
            c_out[b, h] = w_inter * c_prev + w_intra * kv
            n_out[b, h:h + 1, :] = w_inter * n_prev + w_intra * k
            m_out[b, :, h:h + 1] = m_t


def _sstep(r3, g3, bg_row, mh_norm_w, c_state, n_state, m_state3, rows_step):
    nb = r3.shape[0]
    d = ML_HEADS * ML_HD
    sec = lambda j: (lambda b: (b, 0, j))
    fixed = lambda b: (0, 0)
    row3 = lambda b: (b, 0, 0)
    state4 = lambda b: (b, 0, 0, 0)
    return pl.pallas_call(
        _sstep_kernel,
        grid=(nb // rows_step,),
        in_specs=[
            pl.BlockSpec((rows_step, 1, d), sec(SEC_MQ)),
            pl.BlockSpec((rows_step, 1, d), sec(SEC_MK)),
            pl.BlockSpec((rows_step, 1, d), sec(SEC_MV)),
            pl.BlockSpec((rows_step, 1, d), sec(SEC_MO)),
            pl.BlockSpec((rows_step, 1, d), sec(SEC_MZ)),
            pl.BlockSpec((rows_step, 1, GATE_LANES), row3),
            pl.BlockSpec((1, GATE_LANES), fixed),
            pl.BlockSpec((1, ML_HD), fixed),
            pl.BlockSpec((rows_step, ML_HEADS, ML_HD, ML_HD), state4),
            pl.BlockSpec((rows_step, ML_HEADS, ML_HD), row3),
            pl.BlockSpec((rows_step, 1, ML_HEADS), row3),
        ],
        out_specs=[
            pl.BlockSpec((rows_step, 1, d), row3),
            pl.BlockSpec((rows_step, ML_HEADS, ML_HD, ML_HD), state4),
            pl.BlockSpec((rows_step, ML_HEADS, ML_HD), row3),
            pl.BlockSpec((rows_step, 1, ML_HEADS), row3),
        ],
        out_shape=[
            jax.ShapeDtypeStruct((nb, 1, d), BF16),
            jax.ShapeDtypeStruct(c_state.shape, F32),
            jax.ShapeDtypeStruct(n_state.shape, F32),
            jax.ShapeDtypeStruct(m_state3.shape, F32),
        ],
        compiler_params=_params("arbitrary"),
        name="mlstm_sample",
    )(r3, r3, r3, r3, r3, g3, bg_row, mh_norm_w, c_state, n_state, m_state3)


def _sgate_kernel(q_ref, kmean_ref, sel_ref):
    n_blocks = kmean_ref.shape[1]
    gate = jnp.sum(kmean_ref[0] * q_ref[...], axis=-1, keepdims=True)
    row = lax.broadcasted_iota(jnp.int32, gate.shape, 0).astype(F32)
    lane = lax.broadcasted_iota(jnp.int32, (ATT_HEADS, GATE_LANES), 1)
    picked = jnp.zeros((ATT_HEADS, GATE_LANES), F32)
    for j in range(MOBA_TOPK):
        best = jnp.max(gate, axis=0, keepdims=True)
        idx = jnp.min(jnp.where(gate == best, row, float(n_blocks)), axis=0, keepdims=True)
        picked = jnp.where(lane == j, idx[0], picked)
        gate = jnp.where(row == idx, NEG_INF, gate)
    sel_ref[0] = picked.astype(jnp.int32)


def _sgate(q_heads, kmean):
    nb, n_blocks = kmean.shape[:2]
    return pl.pallas_call(
        _sgate_kernel,
        grid=(nb,),
        in_specs=[pl.BlockSpec((1, ATT_HEADS, ATT_HD), lambda b: (b, 0, 0)),
                  pl.BlockSpec((1, n_blocks, ATT_HEADS, ATT_HD), lambda b: (b, 0, 0, 0))],
        out_specs=pl.BlockSpec((1, ATT_HEADS, GATE_LANES), lambda b: (b, 0, 0)),
        out_shape=jax.ShapeDtypeStruct((nb, ATT_HEADS, GATE_LANES), jnp.int32),
        compiler_params=_params("arbitrary"),
        name="moba_sample_gate",
    )(q_heads, kmean)


def _sattn_copies(pt_ref, sel_ref, ck_ref, cv_ref, kbuf, vbuf, sems, b, slot):
    ppb = MOBA_BLOCK // PAGE_SIZE
    copies = []
    for h in range(ATT_HEADS):
        for j in range(MOBA_TOPK):
            block = sel_ref[b, h * MOBA_TOPK + j]
            for p in range(ppb):
                page = pt_ref[b, block * ppb + p]
                rows = pl.ds((j * ppb + p) * PAGE_SIZE, PAGE_SIZE)
                copies.append(pltpu.make_async_copy(ck_ref.at[page, :, h, :], kbuf.at[slot, h, rows, :], sems.at[0, slot]))
                copies.append(pltpu.make_async_copy(cv_ref.at[page, :, h, :], vbuf.at[slot, h, rows, :], sems.at[1, slot]))
    return copies


def _sattn_row(q, kn, vn, az, k_sel, v_sel):
    qs = q * (ATT_HD ** -0.5)
    s_new = jnp.sum(qs * kn, axis=-1, keepdims=True)
    sub = lax.broadcasted_iota(jnp.int32, (ATT_HEADS, ATT_HD), 0)
    out = jnp.zeros((ATT_HEADS, ATT_HD), F32)
    for h in range(ATT_HEADS):
        s = jnp.sum(k_sel[h] * qs[h:h + 1, :], axis=-1, keepdims=True)
        sn = s_new[h:h + 1, :]
        m = jnp.maximum(sn, jnp.max(s, axis=0, keepdims=True))
        p = jnp.exp(s - m)
        pn = jnp.exp(sn - m)
        l = pn + jnp.sum(p, axis=0, keepdims=True)
        acc = pn * vn[h:h + 1, :] + jnp.sum(p * v_sel[h], axis=0, keepdims=True)
        out = jnp.where(sub == h, acc / l, out)
    return out * _silu(az.astype(F32))


def _merge_kernel(*refs, n_side_rows, rows_step):
    if n_side_rows:
        pt_ref, sel_ref, refs = refs[0], refs[1], refs[2:]
    x_ref, am_ref, hm_ref, ga_ref, gm_ref, wa_ref, wm_ref, wo_ref = refs[:8]
    if n_side_rows:
        q_ref, kn_ref, vn_ref, az_ref, ck_ref, cv_ref, y_ref, o_ref, kbuf, vbuf, sems, mix_sc = refs[8:]
        side_tiles = []
        fetch = functools.partial(_sattn_copies, pt_ref, sel_ref, ck_ref, cv_ref, kbuf, vbuf, sems)
        step = pl.program_id(0)
        last_block = n_side_rows // rows_step - 1

        def row_of(s, j):
            return jnp.minimum(s, last_block) * rows_step + j

        def after(s, j):
            return (s, j + 1) if j + 1 < rows_step else (s + 1, 0)

        @pl.when(step == 0)
        def _():
            for cp in fetch(row_of(0, 0), 0):
                cp.start()

        for j in range(rows_step):
            cnt = step * rows_step + j
            for cp in fetch(row_of(*after(step, j)), (cnt + 1) % 2):
                cp.start()
            for cp in fetch(row_of(step, j), cnt % 2):
                cp.wait()
            side_out = _sattn_row(q_ref[j], kn_ref[j], vn_ref[j], az_ref[j], kbuf.at[cnt % 2], vbuf.at[cnt % 2])
            o_ref[j] = side_out.astype(BF16)
            side_tiles.append(side_out)
    else:
        y_ref = refs[8]

    am = jnp.concatenate([am_ref[h] for h in range(ATT_HEADS)], axis=1)
    y_att = jnp.dot(am, wa_ref[...], preferred_element_type=F32)
    y_ml = jnp.dot(hm_ref[...], wm_ref[...], preferred_element_type=F32)
    mix = _sigmoid(ga_ref[...].astype(F32)) * y_att + _sigmoid(gm_ref[...].astype(F32)) * y_ml
    if n_side_rows:
        mix_sc[...] = mix.astype(BF16)
        for tile in side_tiles:
            _pin_before_readers(mix_sc, tile)
        y_ref[...] = x_ref[...] + jnp.dot(mix_sc[...], wo_ref[...], preferred_element_type=F32)

        @pl.when(step == pl.num_programs(0) - 1)
        def _():
            for cp in fetch(row_of(step + 1, 0), ((step + 1) * rows_step) % 2):
                cp.wait()
    else:
        y_ref[...] = x_ref[...] + jnp.dot(mix.astype(BF16), wo_ref[...], preferred_element_type=F32)


def _merge(x, am, hm, r, wa, wm, wo, tm, side=None):
    t_rows, d = x.shape
    n_steps = t_rows // tm
    row = lambda i, *_: (i, 0)
    fixed = lambda i, *_: (0, 0)
    in_specs = [
        pl.BlockSpec((tm, d), row),
        pl.BlockSpec((ATT_HEADS, tm, ATT_HD), lambda i, *_: (0, i, 0)),
        pl.BlockSpec((tm, d), row),
        pl.BlockSpec((tm, d), lambda i, *_: (i, SEC_GA)),
        pl.BlockSpec((tm, d), lambda i, *_: (i, SEC_GM)),
        pl.BlockSpec((d, d), fixed),
        pl.BlockSpec((d, d), fixed),
        pl.BlockSpec((d, d), fixed),
    ]
    y_spec = pl.BlockSpec((tm, d), row)
    y_shape = jax.ShapeDtypeStruct((t_rows, d), F32)
    if side is None:
        return pl.pallas_call(
            functools.partial(_merge_kernel, n_side_rows=0, rows_step=0),
            grid=(n_steps,), in_specs=in_specs, out_specs=y_spec, out_shape=y_shape,
            compiler_params=_params("arbitrary"), name="merge",
        )(x, am, hm, r, r, wa, wm, wo)

    page_table, sel, q_heads, k_heads, v_heads, az_heads, cache_k, cache_v = side
    n_rows = q_heads.shape[0]
    rows_step = next(c for c in range(1, n_rows + 1) if n_rows % c == 0 and n_rows // c <= n_steps)
    n_blocks = n_rows // rows_step
    n_keys = MOBA_TOPK * MOBA_BLOCK
    head_spec = pl.BlockSpec((rows_step, ATT_HEADS, ATT_HD), lambda i, *_: (jnp.minimum(i, n_blocks - 1), 0, 0))
    return pl.pallas_call(
        functools.partial(_merge_kernel, n_side_rows=n_rows, rows_step=rows_step),
        grid_spec=pltpu.PrefetchScalarGridSpec(
            num_scalar_prefetch=2,
            grid=(n_steps,),
            in_specs=in_specs + [head_spec, head_spec, head_spec, head_spec,
                                 pl.BlockSpec(memory_space=pl.ANY), pl.BlockSpec(memory_space=pl.ANY)],
            out_specs=[y_spec, head_spec],
            scratch_shapes=[
                pltpu.VMEM((2, ATT_HEADS, n_keys, ATT_HD), F32),
                pltpu.VMEM((2, ATT_HEADS, n_keys, ATT_HD), F32),
                pltpu.SemaphoreType.DMA((2, 2)),
                pltpu.VMEM((tm, d), BF16),
            ],
        ),
        out_shape=[y_shape, jax.ShapeDtypeStruct((n_rows, ATT_HEADS, ATT_HD), BF16)],
        compiler_params=_params("arbitrary"),
        name="merge_with_sample_attend",
    )(page_table, sel, x, am, hm, r, r, wa, wm, wo, q_heads, k_heads, v_heads, az_heads, cache_k, cache_v)


def _row_tile(rows, target):
    tm = min(rows, target)
    while rows % tm:
        tm //= 2
    return tm


def kernel(x_prompt, x_sample, cache_k, cache_v, page_table, state_mlstm_C, state_mlstm_n, state_mlstm_m,
           norm_w, w_in, b_gates, q_norm_w, k_norm_w, mh_norm_w, w_proj_attn, w_proj_mlstm, w_out):
    batch, seq, d = x_prompt.shape
    dec_batch, dec_seq, _ = x_sample.shape
    assert dec_seq == 1 and d == ATT_HEADS * ATT_HD == ML_HEADS * ML_HD
    depth = w_in.shape[0]
    n_pages = page_table.shape[1]
    past = n_pages * PAGE_SIZE
    assert past % MOBA_BLOCK == 0 and seq % MOBA_BLOCK == 0
    t_p, t_s = batch * seq, dec_batch * dec_seq

    tm_proj = MOBA_BLOCK
    assert seq % tm_proj == 0
    chunk = _row_tile(seq, 256)
    rope_p = _rope_tables(jnp.arange(seq, dtype=jnp.int32))
    rope_s = _rope_tables(jnp.tile(past + jnp.arange(dec_seq, dtype=jnp.int32), dec_batch))

    ppb = MOBA_BLOCK // PAGE_SIZE
    steps = (t_p // tm_proj, batch * ATT_HEADS, batch * (seq // chunk))
    weights = (3, 1, 4)
    total_pages = dec_batch * n_pages
    unit = -(-total_pages // (sum(s * w for s, w in zip(steps, weights)) * ppb)) * ppb
    share = tuple(unit * w for w in weights)
    first = (0, steps[0] * share[0], steps[0] * share[0] + steps[1] * share[1])
    capacity = first[2] + steps[2] * share[2]
    pt_flat = jnp.pad(page_table.reshape(-1), (0, capacity - total_pages))

    y_p = x_prompt.reshape(t_p, d)
    y_s = x_sample.reshape(t_s, d)
    outs = [[] for _ in range(10)]
    for l in range(depth):
        w_wide = _cast_wide_weight(w_in, l, d)
        w_gate2 = _split_gate_weight(w_in[l][:, N_WIDE * d:])
        bg_row = jnp.pad(b_gates[l].astype(F32), (0, GATE_LANES - 2 * ML_HEADS)).reshape(1, GATE_LANES)
        bg_col = bg_row.reshape(GATE_LANES, 1)
        nw = norm_w[l].reshape(1, d)
        qnw, knw = q_norm_w[l].reshape(1, ATT_HD), k_norm_w[l].reshape(1, ATT_HD)
        mhw = mh_norm_w[l].reshape(1, ML_HD)
        wa, wm, wo = w_proj_attn[l].astype(BF16), w_proj_mlstm[l].astype(BF16), w_out[l].astype(BF16)

        q_h, k, kb_h, km, v, vt_h, az_h, r, g, gt, means0 = _inproj(
            pt_flat, y_p, nw, w_wide, w_gate2, rope_p, qnw, knw, cache_k[l], tm_proj, seq // tm_proj,
            share[0], first[0])
        am_h, means1 = _moba(pt_flat, q_h, kb_h, vt_h, az_h, km, cache_k[l], batch, seq, share[1], first[1])
        hm, c_new, n_new, m_new, means2 = _mlstm(pt_flat, r, g, gt, bg_row, bg_col, mhw, cache_k[l], batch, seq, chunk,
                                                 share[2], first[2])
        for dst, val in zip(outs[:5], (k.reshape(batch, seq, ATT_HEADS, ATT_HD), v.reshape(batch, seq, ATT_HEADS, ATT_HD),
                                       c_new, n_new, m_new[:, :, 0])):
            dst.append(val)
        kmean = jnp.concatenate([means0, means1, means2], axis=0)[:total_pages // ppb]
        kmean = kmean.reshape(dec_batch, n_pages // ppb, ATT_HEADS, ATT_HD)
        r_p = r

        q_h, k, _, _, v, _, az_h, r, g, _ = _inproj(
            pt_flat, y_s, nw, w_wide, w_gate2, rope_s, qnw, knw, cache_k[l], t_s, 1, 0, 0)
        r3 = r.reshape(t_s, 1, r.shape[-1])
        by_head = lambda t: t.reshape(t_s, ATT_HEADS, ATT_HD)
        rows_major = lambda t: jnp.swapaxes(t, 0, 1)
        q_rows = rows_major(q_h)
        sel = _sgate(q_rows, kmean)
        sel = sel[:, :, :MOBA_TOPK].reshape(t_s, ATT_HEADS * MOBA_TOPK)
        y_p, am_s = _merge(y_p, am_h, hm, r_p, wa, wm, wo, _row_tile(t_p, 512),
                           side=(page_table, sel, q_rows, by_head(k), by_head(v), rows_major(az_h),
                                 cache_k[l], cache_v[l]))
        hm, c_new, n_new, m_new = _sstep(
            r3, g.reshape(t_s, 1, GATE_LANES), bg_row, mhw,
            state_mlstm_C[l], state_mlstm_n[l], state_mlstm_m[l].reshape(dec_batch, 1, ML_HEADS), _row_tile(t_s, 4))
        y_s = _merge(y_s, rows_major(am_s), hm.reshape(t_s, d), r, wa, wm, wo, t_s)
        for dst, val in zip(outs[5:], (k.reshape(dec_batch, dec_seq, ATT_HEADS, ATT_HD),
                                       v.reshape(dec_batch, dec_seq, ATT_HEADS, ATT_HD),
                                       c_new, n_new, m_new.reshape(dec_batch, ML_HEADS))):
            dst.append(val)

    st = state_mlstm_C.dtype
    k_p, v_p, c_p, n_p, m_p, k_s, v_s, c_s, n_s, m_s = (jnp.stack(o) for o in outs)
    return (y_p.reshape(batch, seq, d), y_s.reshape(dec_batch, dec_seq, d),
            k_p, v_p, c_p.astype(st), n_p.astype(st), m_p.astype(st),
            k_s, v_s, c_s.astype(st), n_s.astype(st), m_s.astype(st))
```

```python
import functools

import jax
import jax.numpy as jnp
from jax import lax
from jax.experimental import pallas as pl
from jax.experimental.pallas import tpu as pltpu

F32 = jnp.float32
BF16 = jnp.bfloat16
HIGHEST = lax.Precision.HIGHEST
NEG_INF = float("-inf")

ATT_HEADS = 8
ATT_HD = 128
ROT_DIM = ATT_HD // 4
ROPE_THETA = 500000.0
MOBA_BLOCK = 256
MOBA_TOPK = 3
ML_HEADS = 4
ML_HD = 256
PAGE_SIZE = 128
EPS = 1e-6
N_WIDE = 11
N_ROW_SECTIONS = 7
SEC_MQ, SEC_MK, SEC_MV, SEC_MO, SEC_MZ, SEC_GA, SEC_GM = range(N_ROW_SECTIONS)
GATE_LANES = 128
N_INPROJ_OUTS = 10
LOG2_E = 1.4426950408889634
KEY_TILE = 256
MOBA_HEADS_STEP = 4

VMEM_LIMIT_BYTES = 60000 * 1024

NT_DIMS = (((1,), (1,)), ((), ()))
TN_DIMS = (((0,), (0,)), ((), ()))


def _sigmoid(x):
    return 1.0 / (1.0 + jnp.exp(-x))


def _silu(x):
    return x * _sigmoid(x)


def _log_sigmoid(x):
    return jnp.minimum(x, 0.0) - jnp.log1p(jnp.exp(-jnp.abs(x)))


def _params(*semantics):
    return pltpu.CompilerParams(dimension_semantics=semantics, vmem_limit_bytes=VMEM_LIMIT_BYTES)


def _head_norm_rope(t, w, rc, rs1, rs2):
    half = ROT_DIM // 2
    outs = []
    for h in range(ATT_HEADS):
        th = t[:, h * ATT_HD:(h + 1) * ATT_HD]
        y = th * lax.rsqrt(jnp.mean(th * th, axis=-1, keepdims=True) + EPS) * w
        up = pltpu.roll(y, ATT_HD - half, axis=1)
        down = pltpu.roll(y, half, axis=1)
        outs.append(y * rc + up * rs1 + down * rs2)
    return outs


def _cast_kernel(x_ref, o_ref):
    o_ref[...] = x_ref[...].T.astype(o_ref.dtype)


def _cast_wide_weight(w_in, layer, d):
    return pl.pallas_call(
        _cast_kernel,
        grid=(N_WIDE,),
        in_specs=[pl.BlockSpec((None, d, d), lambda j: (layer, j, 0))],
        out_specs=pl.BlockSpec((None, d, d), lambda j: (j, 0, 0)),
        out_shape=jax.ShapeDtypeStruct((N_WIDE, d, d), BF16),
        compiler_params=_params("arbitrary"),
        name="cast_weight",
    )(jnp.swapaxes(w_in, 1, 2))


def _page_specs(n_step, first_page, step_of):
    def spec(p):
        return pl.BlockSpec((1, PAGE_SIZE, ATT_HEADS, ATT_HD),
                            lambda *ids: (ids[-1][first_page + step_of(*ids[:-1]) * n_step + p], 0, 0, 0))
    return [spec(p) for p in range(n_step)]


def _block_means_steps(page_refs, out_ref):
    ppb = MOBA_BLOCK // PAGE_SIZE
    for j in range(len(page_refs) // ppb):
        tot = jnp.sum(page_refs[ppb * j][0], axis=0)
        for p in range(1, ppb):
            tot = tot + jnp.sum(page_refs[ppb * j + p][0], axis=0)
        out_ref[j] = tot * (1.0 / MOBA_BLOCK)
        yield


def _block_means(page_refs, out_ref):
    for _ in _block_means_steps(page_refs, out_ref):
        pass


def _emit_round_robin(streams):
    streams = list(streams)
    while streams:
        for g in list(streams):
            if next(g, "done") == "done":
                streams.remove(g)


def _means_out(n_step, n_steps, step_of):
    ppb = MOBA_BLOCK // PAGE_SIZE
    spec = pl.BlockSpec((n_step // ppb, ATT_HEADS, ATT_HD), lambda *ids: (step_of(*ids[:-1]), 0, 0))
    return spec, jax.ShapeDtypeStruct((n_steps * n_step // ppb, ATT_HEADS, ATT_HD), F32)


def _pin_before_readers(anchor_ref, value):
    zero = pltpu.bitcast(lax.shift_right_logical(pltpu.bitcast(value, jnp.uint32), jnp.uint32(32)), F32)
    zero = jnp.concatenate([zero, zero], axis=0).astype(anchor_ref.dtype)
    anchor_ref[0:16, 0:GATE_LANES] = anchor_ref[0:16, 0:GATE_LANES] + zero


def _inproj_kernel(pt_ref, x_ref, nw_ref, wg_ref, rc_ref, rs1_ref, rs2_ref, qnw_ref, knw_ref, *refs, n_pages):
    del pt_ref
    w_refs, refs = refs[:N_WIDE], refs[N_WIDE:]
    q_ref, k_ref, kb_ref, km_ref, v_ref, vt_ref, az_ref, r_ref, g_ref, gt_ref = refs[n_pages:n_pages + N_INPROJ_OUTS]
    xb_sc = refs[-1]
    d = x_ref.shape[1]
    x = x_ref[...]
    xn = x * lax.rsqrt(jnp.mean(x * x, axis=-1, keepdims=True) + EPS) * nw_ref[...]
    xb = xn.astype(BF16)
    xb_sc[...] = xb

    ppb = MOBA_BLOCK // PAGE_SIZE
    page_blocks = [(refs[ppb * j:ppb * (j + 1)], refs[n_pages + N_INPROJ_OUTS].at[pl.ds(j, 1)])
                   for j in range(n_pages // ppb)]

    def section(i):
        share = -(-len(page_blocks) // N_WIDE)
        for pages, out in page_blocks[i * share:(i + 1) * share]:
            _block_means(pages, out)
            _pin_before_readers(xb_sc, out[0])
        return jnp.dot(xb_sc[...], w_refs[i][...], preferred_element_type=F32)

    rc, rs1, rs2 = rc_ref[...], rs1_ref[...], rs2_ref[...]
    head = lambda h: slice(h * ATT_HD, (h + 1) * ATT_HD)
    for h, t in enumerate(_head_norm_rope(section(0), qnw_ref[...], rc, rs1, rs2)):
        q_ref[h] = t
    for h, t in enumerate(_head_norm_rope(section(1), knw_ref[...], rc, rs1, rs2)):
        k_ref[:, head(h)] = t
        kb_ref[h] = t.astype(BF16)
        km_ref[h, 0] = jnp.broadcast_to(jnp.sum(t, axis=0, keepdims=True) * (1.0 / t.shape[0]), (8, ATT_HD))
    v = section(2)
    v_ref[...] = v
    az = section(3)
    for h in range(ATT_HEADS):
        vt_ref[h] = v[:, head(h)].T.astype(BF16)
        az_ref[h] = az[:, head(h)].astype(BF16)
    for j in range(N_ROW_SECTIONS):
        t = section(4 + j)
        if j == SEC_MK:
            t = t * (ML_HD ** -0.5)
        r_ref[:, j * d:(j + 1) * d] = t.astype(BF16)
    n_gate = 2 * ML_HEADS
    x_lo = (xn - xb.astype(F32)).astype(BF16)
    both = (jnp.dot(xb_sc[...], wg_ref[...], preferred_element_type=F32)
            + jnp.dot(x_lo, wg_ref[...], preferred_element_type=F32))
    lane = lax.broadcasted_iota(jnp.int32, both.shape, 1)
    g = jnp.where(lane < n_gate, both + pltpu.roll(both, GATE_LANES - n_gate, axis=1), 0.0)
    g_ref[...] = g
    gt_ref[...] = g.T


def _split_gate_weight(w_gate):
    hi = w_gate.astype(BF16)
    lo = (w_gate - hi.astype(F32)).astype(BF16)
    return jnp.pad(jnp.concatenate([hi, lo], axis=1), ((0, 0), (0, GATE_LANES - 2 * w_gate.shape[1])))


def _inproj(page_table_flat, x, norm_w, w_wide, w_gate2, rope, q_norm_w, k_norm_w, cache_k, tm, n_pos_blocks,
            pages_step, first_page):
    t_rows, d = x.shape
    rc, rs1, rs2 = rope
    n_steps = t_rows // tm
    row = lambda i, pt: (i, 0)
    fixed = lambda i, pt: (0, 0)
    pos = lambda i, pt: (i % n_pos_blocks, 0)
    once = pl.Buffered(1)
    step_of = lambda i: i
    wide_section = lambda j: pl.BlockSpec((None, d, d), lambda i, pt: (j, 0, 0), pipeline_mode=once)
    by_head = pl.BlockSpec((ATT_HEADS, tm, ATT_HD), lambda i, pt: (0, i, 0))
    out_specs = [
        by_head,
        pl.BlockSpec((tm, d), row),
        by_head,
        pl.BlockSpec((ATT_HEADS, 1, 8, ATT_HD), lambda i, pt: (0, i, 0, 0)),
        pl.BlockSpec((tm, d), row),
        pl.BlockSpec((ATT_HEADS, ATT_HD, tm), lambda i, pt: (0, 0, i)),
        by_head,
        pl.BlockSpec((tm, N_ROW_SECTIONS * d), row),
        pl.BlockSpec((tm, GATE_LANES), row),
        pl.BlockSpec((GATE_LANES, tm), lambda i, pt: (0, i)),
    ]
    out_shape = [
        jax.ShapeDtypeStruct((ATT_HEADS, t_rows, ATT_HD), F32),
        jax.ShapeDtypeStruct((t_rows, d), F32),
        jax.ShapeDtypeStruct((ATT_HEADS, t_rows, ATT_HD), BF16),
        jax.ShapeDtypeStruct((ATT_HEADS, n_steps, 8, ATT_HD), F32),
        jax.ShapeDtypeStruct((t_rows, d), F32),
        jax.ShapeDtypeStruct((ATT_HEADS, ATT_HD, t_rows), BF16),
        jax.ShapeDtypeStruct((ATT_HEADS, t_rows, ATT_HD), BF16),
        jax.ShapeDtypeStruct((t_rows, N_ROW_SECTIONS * d), BF16),
        jax.ShapeDtypeStruct((t_rows, GATE_LANES), F32),
        jax.ShapeDtypeStruct((GATE_LANES, t_rows), F32),
    ]
    assert len(out_specs) == N_INPROJ_OUTS
    if pages_step:
        spec, shape = _means_out(pages_step, n_steps, step_of)
        out_specs.append(spec)
        out_shape.append(shape)
    return pl.pallas_call(
        functools.partial(_inproj_kernel, n_pages=pages_step),
        grid_spec=pltpu.PrefetchScalarGridSpec(
            num_scalar_prefetch=1,
            grid=(n_steps,),
            in_specs=[
                pl.BlockSpec((tm, d), row),
                pl.BlockSpec((1, d), fixed),
                pl.BlockSpec((d, GATE_LANES), fixed, pipeline_mode=once),
                pl.BlockSpec((tm, ATT_HD), pos),
                pl.BlockSpec((tm, ATT_HD), pos),
                pl.BlockSpec((tm, ATT_HD), pos),
                pl.BlockSpec((1, ATT_HD), fixed),
                pl.BlockSpec((1, ATT_HD), fixed),
            ] + [wide_section(j) for j in range(N_WIDE)] + _page_specs(pages_step, first_page, step_of),
            out_specs=out_specs,
            scratch_shapes=[pltpu.VMEM((tm, d), BF16)],
        ),
        out_shape=out_shape,
        compiler_params=_params("arbitrary"),
        name="inproj",
    )(page_table_flat, x, norm_w, w_gate2, rc, rs1, rs2, q_norm_w, k_norm_w, *([w_wide] * N_WIDE),
      *([cache_k] * pages_step))


def _rope_tables(pos):
    half = ROT_DIM // 2
    inv = ROPE_THETA ** (-(jnp.arange(half, dtype=F32) * 2.0) / ROT_DIM)
    ang = pos.astype(F32)[:, None] * inv[None, :]
    cos, sin = jnp.cos(ang), jnp.sin(ang)
    n = pos.shape[0]
    zeros = jnp.zeros((n, half), F32)
    tail0 = jnp.zeros((n, ATT_HD - ROT_DIM), F32)
    rc = jnp.concatenate([cos, cos, jnp.ones((n, ATT_HD - ROT_DIM), F32)], axis=-1)
    rs1 = jnp.concatenate([-sin, zeros, tail0], axis=-1)
    rs2 = jnp.concatenate([zeros, sin, tail0], axis=-1)
    return rc, rs1, rs2


def _moba_kernel(pt_ref, q_ref, kb_ref, vt_ref, az_ref, km_ref, *refs, n_pages):
    del pt_ref
    o_all, means_ref, s_all, p_all, qs_all = refs[n_pages:]
    page_refs = refs[:n_pages]
    heads_step = q_ref.shape[0]
    pages_head = n_pages // heads_step
    for hh in range(heads_step):
        _moba_head(q_ref.at[hh], kb_ref.at[hh], vt_ref.at[hh], az_ref.at[hh], km_ref.at[hh], o_all.at[hh],
                   s_all.at[hh], p_all.at[hh], qs_all.at[hh],
                   page_refs[hh * pages_head:(hh + 1) * pages_head],
                   means_ref.at[pl.ds(hh * pages_head // (MOBA_BLOCK // PAGE_SIZE), pages_head // (MOBA_BLOCK // PAGE_SIZE))])


def _moba_head(q_ref, kb_ref, vt_ref, az_ref, km_ref, o_ref, s_sc, p_sc, qs_sc, page_refs, means_ref):
    n_pages = len(page_refs)
    blk = MOBA_BLOCK
    nb = q_ref.shape[0] // blk
    nb_pad = -(-nb // 8) * 8
    rows = lambda j: slice(j * blk, (j + 1) * blk)

    sub8 = lax.broadcasted_iota(jnp.int32, (8, ATT_HD), 0)
    groups = []
    for g0 in range(0, nb, 8):
        tile = jnp.zeros((8, ATT_HD), F32)
        for j in range(g0, min(g0 + 8, nb)):
            tile = jnp.where(sub8 == j - g0, km_ref[j], tile)
        groups.append(tile)
    kmean = groups[0] if len(groups) == 1 else jnp.concatenate(groups, axis=0)

    blk_id = lax.broadcasted_iota(jnp.int32, (nb_pad, blk), 0)
    key_ix = lax.broadcasted_iota(jnp.int32, (blk, blk), 0)
    qry_ix = lax.broadcasted_iota(jnp.int32, (blk, blk), 1)
    causal = key_ix <= qry_ix

    tiles_per_blk = blk // KEY_TILE
    fold = lambda t: t.reshape(KEY_TILE // 8, 8, blk)
    state = {}

    def pass1(c):
        q = q_ref[rows(c), :]
        gate = lax.dot_general(kmean, q, NT_DIMS, preferred_element_type=F32, precision=HIGHEST)
        past = blk_id < c
        gate = jnp.where(past, gate, NEG_INF)
        beaten = jnp.zeros(gate.shape, F32)
        for m in range(c):
            gm = gate[m:m + 1, :]
            wins = jnp.where(gm > gate, 1.0, jnp.where(gm == gate, jnp.where(blk_id > m, 1.0, 0.0), 0.0))
            beaten = beaten + wins
        bias = jnp.where(past, jnp.where(beaten < MOBA_TOPK, 0.0, NEG_INF), NEG_INF)
        qs_sc[c % 2] = (q * (ATT_HD ** -0.5 * LOG2_E)).astype(BF16)
        yield
        for j in range(c + 1):
            for t in range(tiles_per_blk):
                keys = slice(j * blk + t * KEY_TILE, j * blk + (t + 1) * KEY_TILE)
                s = lax.dot_general(kb_ref[keys, :], qs_sc[c % 2], NT_DIMS, preferred_element_type=F32)
                if j == c:
                    s = jnp.where(causal[t * KEY_TILE:(t + 1) * KEY_TILE, :], s, NEG_INF)
                else:
                    s = s + bias[j:j + 1, :]
                s_sc[c % 2, keys, :] = s
                s_max = jnp.max(fold(s), axis=0)
                state[c] = s_max if c not in state else jnp.maximum(state[c], s_max)
                yield

    def pass2(c):
        m_col = jnp.max(state[c], axis=0, keepdims=True)
        n_keys = (c + 1) * blk
        l_acc = jnp.zeros((8, blk), F32)
        for t in range(n_keys // KEY_TILE):
            keys = slice(t * KEY_TILE, (t + 1) * KEY_TILE)
            p = jnp.exp2(s_sc[c % 2, keys, :] - m_col)
            l_acc = l_acc + jnp.sum(fold(p), axis=0)
            p_sc[c % 2, keys, :] = p.astype(BF16)
            yield
        l_col = jnp.sum(l_acc, axis=0, keepdims=True)
        acc = jnp.dot(vt_ref[:, :n_keys], p_sc[c % 2, :n_keys, :], preferred_element_type=F32)
        out = (acc / l_col).T
        o_ref[rows(c), :] = (out * _silu(az_ref[rows(c), :].astype(F32))).astype(BF16)
        yield

    ppb = MOBA_BLOCK // PAGE_SIZE
    rounds = max(nb - 1, 1)
    per_round = -(-n_pages // (rounds * ppb)) * ppb

    def pinned_means(lo, hi, anchor):
        for j in range(lo // ppb, hi // ppb):
            out = means_ref.at[pl.ds(j, 1)]
            _block_means(page_refs[ppb * j:ppb * (j + 1)], out)
            _pin_before_readers(anchor, out[0])
            yield

    _emit_round_robin([pass1(0)])
    for c in range(nb):
        streams = [pass2(c)]
        if c + 1 < nb:
            streams.append(pass1(c + 1))
        lo, hi = min(c * per_round, n_pages), min((c + 1) * per_round, n_pages)
        if hi > lo:
            streams.append(pinned_means(lo, hi, qs_sc.at[(c + 1) % 2 if c + 1 < nb else c % 2]))
        _emit_round_robin(streams)


def _moba(page_table_flat, q_h, kb_h, vt_h, az_h, km, cache_k, batch, seq, pages_step, first_page):
    n_heads, t_rows, _ = q_h.shape
    blk = MOBA_BLOCK
    nb = seq // blk
    hs = MOBA_HEADS_STEP
    rows_of = lambda b, h, pt: (h, b, 0)
    step_of = lambda b, h: b * (ATT_HEADS // hs) + h
    means_spec, means_shape = _means_out(pages_step, batch * ATT_HEADS // hs, step_of)
    return pl.pallas_call(
        functools.partial(_moba_kernel, n_pages=pages_step),
        grid_spec=pltpu.PrefetchScalarGridSpec(
            num_scalar_prefetch=1,
            grid=(batch, ATT_HEADS // hs),
            in_specs=[
                pl.BlockSpec((hs, seq, ATT_HD), rows_of),
                pl.BlockSpec((hs, seq, ATT_HD), rows_of),
                pl.BlockSpec((hs, ATT_HD, seq), lambda b, h, pt: (h, 0, b)),
                pl.BlockSpec((hs, seq, ATT_HD), rows_of),
                pl.BlockSpec((hs, nb, 8, ATT_HD), lambda b, h, pt: (h, b, 0, 0)),
            ] + _page_specs(pages_step, first_page, step_of),
            out_specs=[pl.BlockSpec((hs, seq, ATT_HD), rows_of), means_spec],
            scratch_shapes=[
                pltpu.VMEM((hs, 2, seq, blk), F32),
                pltpu.VMEM((hs, 2, seq, blk), BF16),
                pltpu.VMEM((hs, 2, blk, ATT_HD), BF16),
            ],
        ),
        out_shape=[jax.ShapeDtypeStruct((n_heads, t_rows, ATT_HD), BF16), means_shape],
        compiler_params=_params("arbitrary", "arbitrary"),
        name="moba_prompt",
    )(page_table_flat, q_h, kb_h, vt_h, az_h, km, *([cache_k] * pages_step))


def _mlstm_post(h, mo, mz, nw):
    h = h * _sigmoid(mo)
    h = h * lax.rsqrt(jnp.mean(h * h, axis=-1, keepdims=True) + EPS) * nw
    return h * _silu(mz)


def _dot_rows_hi_lo(mat, row, transposed):
    rows = jnp.broadcast_to(row, (8, row.shape[1]))
    hi = rows.astype(BF16)
    lo = (rows - hi.astype(F32)).astype(BF16)
    if transposed:
        return (lax.dot_general(mat, hi, NT_DIMS, preferred_element_type=F32)
                + lax.dot_general(mat, lo, NT_DIMS, preferred_element_type=F32))
    return jnp.dot(hi, mat, preferred_element_type=F32) + jnp.dot(lo, mat, preferred_element_type=F32)


def _mlstm_kernel(pt_ref, q_ref, k_ref, v_ref, mo_ref, mz_ref, g_ref, gt_ref, bgr_ref, bgc_ref, nw_ref, *refs,
                  n_pages):
    del pt_ref
    hm_ref, c_out, n_out, m_out, means_ref, c_sc, n_sc, m_sc = refs[n_pages:]
    c = pl.program_id(1)
    chunk = q_ref.shape[0]

    @pl.when(c == 0)
    def _():
        c_sc[...] = jnp.zeros(c_sc.shape, F32)
        n_sc[...] = jnp.zeros(n_sc.shape, F32)
        m_sc[...] = jnp.zeros(m_sc.shape, F32)

    _block_means(refs[:n_pages], means_ref)

    gc = g_ref[...] + bgr_ref[...]
    lane = lax.broadcasted_iota(jnp.int32, gc.shape, 1)
    gc = jnp.where(lane >= ML_HEADS, _log_sigmoid(gc), gc)
    gr = gt_ref[0:2 * ML_HEADS, :] + bgc_ref[0:2 * ML_HEADS, :]
    sub = lax.broadcasted_iota(jnp.int32, gr.shape, 0)
    gr = jnp.where(sub >= ML_HEADS, _log_sigmoid(gr), gr)

    t_ix = lax.broadcasted_iota(jnp.int32, (chunk, chunk), 0)
    s_ix = lax.broadcasted_iota(jnp.int32, (chunk, chunk), 1)
    causal = s_ix <= t_ix
    tril = jnp.where(causal, 1.0, 0.0)
    bc = jnp.dot(tril, gc, preferred_element_type=F32, precision=HIGHEST)
    br = lax.dot_general(gr, tril, NT_DIMS, preferred_element_type=F32, precision=HIGHEST)

    for h in range(ML_HEADS):
        cols = slice(h * ML_HD, (h + 1) * ML_HD)
        q, k, v = q_ref[:, cols], k_ref[:, cols], v_ref[:, cols]
        ig_r, b_r = gr[h:h + 1, :], br[ML_HEADS + h:ML_HEADS + h + 1, :]
        ig_c, b_c = gc[:, h:h + 1], bc[:, ML_HEADS + h:ML_HEADS + h + 1]
        m_prev = m_sc[h:h + 1, 0:1]
        c_prev = c_sc[h]
        n_prev = n_sc[h:h + 1, :]

        log_w = jnp.where(causal, b_c - b_r + ig_r, NEG_INF)
        log_inter = b_c + m_prev
        m_t = jnp.maximum(log_inter, jnp.max(log_w, axis=-1, keepdims=True))
        w_intra = jnp.exp(log_w - m_t)
        w_inter = jnp.exp(log_inter - m_t)
        s = lax.dot_general(q, k, NT_DIMS, preferred_element_type=F32) * w_intra
        num = (w_inter * jnp.dot(q, c_prev.astype(BF16), preferred_element_type=F32)
               + jnp.dot(s.astype(BF16), v, preferred_element_type=F32))
        qn = jnp.sum(q.astype(F32) * n_prev, axis=-1, keepdims=True)
        den = w_inter * qn + jnp.sum(s, axis=-1, keepdims=True)
        hh = num / jnp.maximum(jnp.abs(den), jnp.exp(-m_t))

        m_new = m_t[chunk - 1:chunk, :]
        b_last = b_c[chunk - 1:chunk, :]
        a_prev = jnp.exp(b_last + m_prev - m_new)
        a_c = jnp.exp(b_last - b_c + ig_c - m_new)
        a_r = jnp.exp(b_last - b_r + ig_r - m_new)
        kf = k.astype(F32)
        ka = (kf * a_c).astype(BF16)
        c_sc[h] = a_prev * c_prev + lax.dot_general(ka, v, TN_DIMS, preferred_element_type=F32)
        a_rows = jnp.broadcast_to(a_r, (8, chunk))
        n_sc[h:h + 1, :] = a_prev * n_prev + jnp.dot(
            a_rows, kf, preferred_element_type=F32, precision=HIGHEST)[0:1, :]
        m_sc[h:h + 1, :] = jnp.broadcast_to(m_new, (1, m_sc.shape[1]))

        hm_ref[:, cols] = _mlstm_post(
            hh, mo_ref[:, cols].astype(F32), mz_ref[:, cols].astype(F32), nw_ref[...]).astype(BF16)

    @pl.when(c == pl.num_programs(1) - 1)
    def _():
        c_out[0] = c_sc[...]
        n_out[0] = n_sc[...]
        m_out[0] = m_sc[...]


def _mlstm(page_table_flat, r, g, gt, bg_row, bg_col, mh_norm_w, cache_k, batch, seq, chunk, pages_step, first_page):
    t_rows = r.shape[0]
    d = ML_HEADS * ML_HD
    nc = seq // chunk
    sec = lambda j: (lambda b, c, pt: (b * nc + c, j))
    fixed = lambda b, c, pt: (0, 0)
    state = lambda b, c, pt: (b, 0, 0)
    step_of = lambda b, c: b * nc + c
    means_spec, means_shape = _means_out(pages_step, batch * nc, step_of)
    return pl.pallas_call(
        functools.partial(_mlstm_kernel, n_pages=pages_step),
        grid_spec=pltpu.PrefetchScalarGridSpec(
            num_scalar_prefetch=1,
            grid=(batch, nc),
            in_specs=[
                pl.BlockSpec((chunk, d), sec(SEC_MQ)),
                pl.BlockSpec((chunk, d), sec(SEC_MK)),
                pl.BlockSpec((chunk, d), sec(SEC_MV)),
                pl.BlockSpec((chunk, d), sec(SEC_MO)),
                pl.BlockSpec((chunk, d), sec(SEC_MZ)),
                pl.BlockSpec((chunk, GATE_LANES), lambda b, c, pt: (b * nc + c, 0)),
                pl.BlockSpec((GATE_LANES, chunk), lambda b, c, pt: (0, b * nc + c)),
                pl.BlockSpec((1, GATE_LANES), fixed),
                pl.BlockSpec((GATE_LANES, 1), fixed),
                pl.BlockSpec((1, ML_HD), fixed),
            ] + _page_specs(pages_step, first_page, step_of),
            out_specs=[
                pl.BlockSpec((chunk, d), lambda b, c, pt: (b * nc + c, 0)),
                pl.BlockSpec((1, ML_HEADS, ML_HD, ML_HD), lambda b, c, pt: (b, 0, 0, 0)),
                pl.BlockSpec((1, ML_HEADS, ML_HD), state),
                pl.BlockSpec((1, ML_HEADS, GATE_LANES), state),
                means_spec,
            ],
            scratch_shapes=[
                pltpu.VMEM((ML_HEADS, ML_HD, ML_HD), F32),
                pltpu.VMEM((ML_HEADS, ML_HD), F32),
                pltpu.VMEM((ML_HEADS, GATE_LANES), F32),
            ],
        ),
        out_shape=[
            jax.ShapeDtypeStruct((t_rows, d), BF16),
            jax.ShapeDtypeStruct((batch, ML_HEADS, ML_HD, ML_HD), F32),
            jax.ShapeDtypeStruct((batch, ML_HEADS, ML_HD), F32),
            jax.ShapeDtypeStruct((batch, ML_HEADS, GATE_LANES), F32),
            means_shape,
        ],
        compiler_params=_params("arbitrary", "arbitrary"),
        name="mlstm_prompt",
    )(page_table_flat, r, r, r, r, r, g, gt, bg_row, bg_col, mh_norm_w, *([cache_k] * pages_step))


def _sstep_kernel(q_ref, k_ref, v_ref, mo_ref, mz_ref, g_ref, bgr_ref, nw_ref, c_ref, n_ref, m_ref,
                  hm_ref, c_out, n_out, m_out):
    for b in range(q_ref.shape[0]):
        gates = g_ref[b] + bgr_ref[...]
        for h in range(ML_HEADS):
            cols = slice(h * ML_HD, (h + 1) * ML_HD)
            q = q_ref[b, :, cols].astype(F32)
            k = k_ref[b, :, cols].astype(F32)
            v = v_ref[b, :, cols].astype(F32)
            ig = gates[:, h:h + 1]
            lf = _log_sigmoid(gates[:, ML_HEADS + h:ML_HEADS + h + 1])
            m_prev = m_ref[b, :, h:h + 1]
            c_prev = c_ref[b, h]
            n_prev = n_ref[b, h:h + 1, :]

            log_inter = lf + m_prev
            m_t = jnp.maximum(log_inter, ig)
            w_intra = jnp.exp(ig - m_t)
            w_inter = jnp.exp(log_inter - m_t)
            s = jnp.sum(q * k, axis=-1, keepdims=True) * w_intra
            qc = _dot_rows_hi_lo(c_prev.astype(BF16), q, transposed=False)[0:1, :]
            num = w_inter * qc + s * v
            den = w_inter * jnp.sum(q * n_prev, axis=-1, keepdims=True) + s
            hh = num / jnp.maximum(jnp.abs(den), jnp.exp(-m_t))
            hm_ref[b, :, cols] = _mlstm_post(
                hh, mo_ref[b, :, cols].astype(F32), mz_ref[b, :, cols].astype(F32), nw_ref[...]).astype(BF16)

            sub = lax.broadcasted_iota(jnp.int32, (8, ML_HD), 0)
            k8 = jnp.where(sub == 0, jnp.broadcast_to(k, (8, ML_HD)), 0.0)
            v8 = jnp.broadcast_to(v, (8, ML_HD))
            kv = lax.dot_general(k8, v8, TN_DIMS, preferred_element_type=F32, precision=HIGHEST)
            c_out[b, h] = w_inter * c_prev + w_intra * kv
            n_out[b, h:h + 1, :] = w_inter * n_prev + w_intra * k
            m_out[b, :, h:h + 1] = m_t


def _sstep(r3, g3, bg_row, mh_norm_w, c_state, n_state, m_state3, rows_step):
    nb = r3.shape[0]
    d = ML_HEADS * ML_HD
    sec = lambda j: (lambda b: (b, 0, j))
    fixed = lambda b: (0, 0)
    row3 = lambda b: (b, 0, 0)
    state4 = lambda b: (b, 0, 0, 0)
    return pl.pallas_call(
        _sstep_kernel,
        grid=(nb // rows_step,),
        in_specs=[
            pl.BlockSpec((rows_step, 1, d), sec(SEC_MQ)),
            pl.BlockSpec((rows_step, 1, d), sec(SEC_MK)),
            pl.BlockSpec((rows_step, 1, d), sec(SEC_MV)),
            pl.BlockSpec((rows_step, 1, d), sec(SEC_MO)),
            pl.BlockSpec((rows_step, 1, d), sec(SEC_MZ)),
            pl.BlockSpec((rows_step, 1, GATE_LANES), row3),
            pl.BlockSpec((1, GATE_LANES), fixed),
            pl.BlockSpec((1, ML_HD), fixed),
            pl.BlockSpec((rows_step, ML_HEADS, ML_HD, ML_HD), state4),
            pl.BlockSpec((rows_step, ML_HEADS, ML_HD), row3),
            pl.BlockSpec((rows_step, 1, ML_HEADS), row3),
        ],
        out_specs=[
            pl.BlockSpec((rows_step, 1, d), row3),
            pl.BlockSpec((rows_step, ML_HEADS, ML_HD, ML_HD), state4),
            pl.BlockSpec((rows_step, ML_HEADS, ML_HD), row3),
            pl.BlockSpec((rows_step, 1, ML_HEADS), row3),
        ],
        out_shape=[
            jax.ShapeDtypeStruct((nb, 1, d), BF16),
            jax.ShapeDtypeStruct(c_state.shape, F32),
            jax.ShapeDtypeStruct(n_state.shape, F32),
            jax.ShapeDtypeStruct(m_state3.shape, F32),
        ],
        compiler_params=_params("arbitrary"),
        name="mlstm_sample",
    )(r3, r3, r3, r3, r3, g3, bg_row, mh_norm_w, c_state, n_state, m_state3)


def _sgate_kernel(q_ref, kmean_ref, sel_ref):
    n_blocks = kmean_ref.shape[1]
    gate = jnp.sum(kmean_ref[0] * q_ref[...], axis=-1, keepdims=True)
    row = lax.broadcasted_iota(jnp.int32, gate.shape, 0).astype(F32)
    lane = lax.broadcasted_iota(jnp.int32, (ATT_HEADS, GATE_LANES), 1)
    picked = jnp.zeros((ATT_HEADS, GATE_LANES), F32)
    for j in range(MOBA_TOPK):
        best = jnp.max(gate, axis=0, keepdims=True)
        idx = jnp.min(jnp.where(gate == best, row, float(n_blocks)), axis=0, keepdims=True)
        picked = jnp.where(lane == j, idx[0], picked)
        gate = jnp.where(row == idx, NEG_INF, gate)
    sel_ref[0] = picked.astype(jnp.int32)


def _sgate(q_heads, kmean):
    nb, n_blocks = kmean.shape[:2]
    return pl.pallas_call(
        _sgate_kernel,
        grid=(nb,),
        in_specs=[pl.BlockSpec((1, ATT_HEADS, ATT_HD), lambda b: (b, 0, 0)),
                  pl.BlockSpec((1, n_blocks, ATT_HEADS, ATT_HD), lambda b: (b, 0, 0, 0))],
        out_specs=pl.BlockSpec((1, ATT_HEADS, GATE_LANES), lambda b: (b, 0, 0)),
        out_shape=jax.ShapeDtypeStruct((nb, ATT_HEADS, GATE_LANES), jnp.int32),
        compiler_params=_params("arbitrary"),
        name="moba_sample_gate",
    )(q_heads, kmean)


def _sattn_copies(pt_ref, sel_ref, ck_ref, cv_ref, kbuf, vbuf, sems, b, slot):
    ppb = MOBA_BLOCK // PAGE_SIZE
    copies = []
    for h in range(ATT_HEADS):
        for j in range(MOBA_TOPK):
            block = sel_ref[b, h * MOBA_TOPK + j]
            for p in range(ppb):
                page = pt_ref[b, block * ppb + p]
                rows = pl.ds((j * ppb + p) * PAGE_SIZE, PAGE_SIZE)
                copies.append(pltpu.make_async_copy(ck_ref.at[page, :, h, :], kbuf.at[slot, h, rows, :], sems.at[0, slot]))
                copies.append(pltpu.make_async_copy(cv_ref.at[page, :, h, :], vbuf.at[slot, h, rows, :], sems.at[1, slot]))
    return copies


def _sattn_row(q, kn, vn, az, k_sel, v_sel):
    qs = q * (ATT_HD ** -0.5)
    s_new = jnp.sum(qs * kn, axis=-1, keepdims=True)
    sub = lax.broadcasted_iota(jnp.int32, (ATT_HEADS, ATT_HD), 0)
    out = jnp.zeros((ATT_HEADS, ATT_HD), F32)
    for h in range(ATT_HEADS):
        s = jnp.sum(k_sel[h] * qs[h:h + 1, :], axis=-1, keepdims=True)
        sn = s_new[h:h + 1, :]
        m = jnp.maximum(sn, jnp.max(s, axis=0, keepdims=True))
        p = jnp.exp(s - m)
        pn = jnp.exp(sn - m)
        l = pn + jnp.sum(p, axis=0, keepdims=True)
        acc = pn * vn[h:h + 1, :] + jnp.sum(p * v_sel[h], axis=0, keepdims=True)
        out = jnp.where(sub == h, acc / l, out)
    return out * _silu(az.astype(F32))


def _merge_kernel(*refs, n_side_rows, rows_step):
    if n_side_rows:
        pt_ref, sel_ref, refs = refs[0], refs[1], refs[2:]
    x_ref, am_ref, hm_ref, ga_ref, gm_ref, wa_ref, wm_ref, wo_ref = refs[:8]
    if n_side_rows:
        q_ref, kn_ref, vn_ref, az_ref, ck_ref, cv_ref, y_ref, o_ref, kbuf, vbuf, sems, mix_sc = refs[8:]
        side_tiles = []
        fetch = functools.partial(_sattn_copies, pt_ref, sel_ref, ck_ref, cv_ref, kbuf, vbuf, sems)
        step = pl.program_id(0)
        last_block = n_side_rows // rows_step - 1

        def row_of(s, j):
            return jnp.minimum(s, last_block) * rows_step + j

        def after(s, j):
            return (s, j + 1) if j + 1 < rows_step else (s + 1, 0)

        @pl.when(step == 0)
        def _():
            for cp in fetch(row_of(0, 0), 0):
                cp.start()

        for j in range(rows_step):
            cnt = step * rows_step + j
            for cp in fetch(row_of(*after(step, j)), (cnt + 1) % 2):
                cp.start()
            for cp in fetch(row_of(step, j), cnt % 2):
                cp.wait()
            side_out = _sattn_row(q_ref[j], kn_ref[j], vn_ref[j], az_ref[j], kbuf.at[cnt % 2], vbuf.at[cnt % 2])
            o_ref[j] = side_out.astype(BF16)
            side_tiles.append(side_out)
    else:
        y_ref = refs[8]

    am = jnp.concatenate([am_ref[h] for h in range(ATT_HEADS)], axis=1)
    y_att = jnp.dot(am, wa_ref[...], preferred_element_type=F32)
    y_ml = jnp.dot(hm_ref[...], wm_ref[...], preferred_element_type=F32)
    mix = _sigmoid(ga_ref[...].astype(F32)) * y_att + _sigmoid(gm_ref[...].astype(F32)) * y_ml
    if n_side_rows:
        mix_sc[...] = mix.astype(BF16)
        for tile in side_tiles:
            _pin_before_readers(mix_sc, tile)
        y_ref[...] = x_ref[...] + jnp.dot(mix_sc[...], wo_ref[...], preferred_element_type=F32)

        @pl.when(step == pl.num_programs(0) - 1)
        def _():
            for cp in fetch(row_of(step + 1, 0), ((step + 1) * rows_step) % 2):
                cp.wait()
    else:
        y_ref[...] = x_ref[...] + jnp.dot(mix.astype(BF16), wo_ref[...], preferred_element_type=F32)


def _merge(x, am, hm, r, wa, wm, wo, tm, side=None):
    t_rows, d = x.shape
    n_steps = t_rows // tm
    row = lambda i, *_: (i, 0)
    fixed = lambda i, *_: (0, 0)
    in_specs = [
        pl.BlockSpec((tm, d), row),
        pl.BlockSpec((ATT_HEADS, tm, ATT_HD), lambda i, *_: (0, i, 0)),
        pl.BlockSpec((tm, d), row),
        pl.BlockSpec((tm, d), lambda i, *_: (i, SEC_GA)),
        pl.BlockSpec((tm, d), lambda i, *_: (i, SEC_GM)),
        pl.BlockSpec((d, d), fixed),
        pl.BlockSpec((d, d), fixed),
        pl.BlockSpec((d, d), fixed),
    ]
    y_spec = pl.BlockSpec((tm, d), row)
    y_shape = jax.ShapeDtypeStruct((t_rows, d), F32)
    if side is None:
        return pl.pallas_call(
            functools.partial(_merge_kernel, n_side_rows=0, rows_step=0),
            grid=(n_steps,), in_specs=in_specs, out_specs=y_spec, out_shape=y_shape,
            compiler_params=_params("arbitrary"), name="merge",
        )(x, am, hm, r, r, wa, wm, wo)

    page_table, sel, q_heads, k_heads, v_heads, az_heads, cache_k, cache_v = side
    n_rows = q_heads.shape[0]
    rows_step = next(c for c in range(1, n_rows + 1) if n_rows % c == 0 and n_rows // c <= n_steps)
    n_blocks = n_rows // rows_step
    n_keys = MOBA_TOPK * MOBA_BLOCK
    head_spec = pl.BlockSpec((rows_step, ATT_HEADS, ATT_HD), lambda i, *_: (jnp.minimum(i, n_blocks - 1), 0, 0))
    return pl.pallas_call(
        functools.partial(_merge_kernel, n_side_rows=n_rows, rows_step=rows_step),
        grid_spec=pltpu.PrefetchScalarGridSpec(
            num_scalar_prefetch=2,
            grid=(n_steps,),
            in_specs=in_specs + [head_spec, head_spec, head_spec, head_spec,
                                 pl.BlockSpec(memory_space=pl.ANY), pl.BlockSpec(memory_space=pl.ANY)],
            out_specs=[y_spec, head_spec],
            scratch_shapes=[
                pltpu.VMEM((2, ATT_HEADS, n_keys, ATT_HD), F32),
                pltpu.VMEM((2, ATT_HEADS, n_keys, ATT_HD), F32),
                pltpu.SemaphoreType.DMA((2, 2)),
                pltpu.VMEM((tm, d), BF16),
            ],
        ),
        out_shape=[y_shape, jax.ShapeDtypeStruct((n_rows, ATT_HEADS, ATT_HD), BF16)],
        compiler_params=_params("arbitrary"),
        name="merge_with_sample_attend",
    )(page_table, sel, x, am, hm, r, r, wa, wm, wo, q_heads, k_heads, v_heads, az_heads, cache_k, cache_v)


def _row_tile(rows, target):
    tm = min(rows, target)
    while rows % tm:
        tm //= 2
    return tm


def kernel(x_prompt, x_sample, cache_k, cache_v, page_table, state_mlstm_C, state_mlstm_n, state_mlstm_m,
           norm_w, w_in, b_gates, q_norm_w, k_norm_w, mh_norm_w, w_proj_attn, w_proj_mlstm, w_out):
    batch, seq, d = x_prompt.shape
    dec_batch, dec_seq, _ = x_sample.shape
    assert dec_seq == 1 and d == ATT_HEADS * ATT_HD == ML_HEADS * ML_HD
    depth = w_in.shape[0]
    n_pages = page_table.shape[1]
    past = n_pages * PAGE_SIZE
    assert past % MOBA_BLOCK == 0 and seq % MOBA_BLOCK == 0
    t_p, t_s = batch * seq, dec_batch * dec_seq

    tm_proj = MOBA_BLOCK
    assert seq % tm_proj == 0
    chunk = _row_tile(seq, 256)
    rope_p = _rope_tables(jnp.arange(seq, dtype=jnp.int32))
    rope_s = _rope_tables(jnp.tile(past + jnp.arange(dec_seq, dtype=jnp.int32), dec_batch))

    ppb = MOBA_BLOCK // PAGE_SIZE
    steps = (t_p // tm_proj, batch * ATT_HEADS // MOBA_HEADS_STEP, batch * (seq // chunk))
    weights = (3, 4, 4)
    total_pages = dec_batch * n_pages
    unit = -(-total_pages // (sum(s * w for s, w in zip(steps, weights)) * ppb)) * ppb
    share = tuple(unit * w for w in weights)
    first = (0, steps[0] * share[0], steps[0] * share[0] + steps[1] * share[1])
    capacity = first[2] + steps[2] * share[2]
    pt_flat = jnp.pad(page_table.reshape(-1), (0, capacity - total_pages))

    y_p = x_prompt.reshape(t_p, d)
    y_s = x_sample.reshape(t_s, d)
    outs = [[] for _ in range(10)]
    for l in range(depth):
        w_wide = _cast_wide_weight(w_in, l, d)
        w_gate2 = _split_gate_weight(w_in[l][:, N_WIDE * d:])
        bg_row = jnp.pad(b_gates[l].astype(F32), (0, GATE_LANES - 2 * ML_HEADS)).reshape(1, GATE_LANES)
        bg_col = bg_row.reshape(GATE_LANES, 1)
        nw = norm_w[l].reshape(1, d)
        qnw, knw = q_norm_w[l].reshape(1, ATT_HD), k_norm_w[l].reshape(1, ATT_HD)
        mhw = mh_norm_w[l].reshape(1, ML_HD)
        wa, wm, wo = w_proj_attn[l].astype(BF16), w_proj_mlstm[l].astype(BF16), w_out[l].astype(BF16)

        q_h, k, kb_h, km, v, vt_h, az_h, r, g, gt, means0 = _inproj(
            pt_flat, y_p, nw, w_wide, w_gate2, rope_p, qnw, knw, cache_k[l], tm_proj, seq // tm_proj,
            share[0], first[0])
        am_h, means1 = _moba(pt_flat, q_h, kb_h, vt_h, az_h, km, cache_k[l], batch, seq, share[1], first[1])
        hm, c_new, n_new, m_new, means2 = _mlstm(pt_flat, r, g, gt, bg_row, bg_col, mhw, cache_k[l], batch, seq, chunk,
                                                 share[2], first[2])
        for dst, val in zip(outs[:5], (k.reshape(batch, seq, ATT_HEADS, ATT_HD), v.reshape(batch, seq, ATT_HEADS, ATT_HD),
                                       c_new, n_new, m_new[:, :, 0])):
            dst.append(val)
        kmean = jnp.concatenate([means0, means1, means2], axis=0)[:total_pages // ppb]
        kmean = kmean.reshape(dec_batch, n_pages // ppb, ATT_HEADS, ATT_HD)
        r_p = r

        q_h, k, _, _, v, _, az_h, r, g, _ = _inproj(
            pt_flat, y_s, nw, w_wide, w_gate2, rope_s, qnw, knw, cache_k[l], t_s, 1, 0, 0)
        r3 = r.reshape(t_s, 1, r.shape[-1])
        by_head = lambda t: t.reshape(t_s, ATT_HEADS, ATT_HD)
        rows_major = lambda t: jnp.swapaxes(t, 0, 1)
        q_rows = rows_major(q_h)
        sel = _sgate(q_rows, kmean)
        sel = sel[:, :, :MOBA_TOPK].reshape(t_s, ATT_HEADS * MOBA_TOPK)
        y_p, am_s = _merge(y_p, am_h, hm, r_p, wa, wm, wo, _row_tile(t_p, 512),
                           side=(page_table, sel, q_rows, by_head(k), by_head(v), rows_major(az_h),
                                 cache_k[l], cache_v[l]))
        hm, c_new, n_new, m_new = _sstep(
            r3, g.reshape(t_s, 1, GATE_LANES), bg_row, mhw,
            state_mlstm_C[l], state_mlstm_n[l], state_mlstm_m[l].reshape(dec_batch, 1, ML_HEADS), _row_tile(t_s, 4))
        y_s = _merge(y_s, rows_major(am_s), hm.reshape(t_s, d), r, wa, wm, wo, t_s)
        for dst, val in zip(outs[5:], (k.reshape(dec_batch, dec_seq, ATT_HEADS, ATT_HD),
                                       v.reshape(dec_batch, dec_seq, ATT_HEADS, ATT_HD),
                                       c_new, n_new, m_new.reshape(dec_batch, ML_HEADS))):
            dst.append(val)

    st = state_mlstm_C.dtype
    k_p, v_p, c_p, n_p, m_p, k_s, v_s, c_s, n_s, m_s = (jnp.stack(o) for o in outs)
    return (y_p.reshape(batch, seq, d), y_s.reshape(dec_batch, dec_seq, d),
            k_p, v_p, c_p.astype(st), n_p.astype(st), m_p.astype(st),
            k_s, v_s, c_s.astype(st), n_s.astype(st), m_s.astype(st))
```

```python
import functools

import jax
import jax.numpy as jnp
from jax import lax
from jax.experimental import pallas as pl
from jax.experimental.pallas import tpu as pltpu

F32 = jnp.float32
BF16 = jnp.bfloat16
HIGHEST = lax.Precision.HIGHEST
NEG_INF = float("-inf")

ATT_HEADS = 8
ATT_HD = 128
ROT_DIM = ATT_HD // 4
ROPE_THETA = 500000.0
MOBA_BLOCK = 256
MOBA_TOPK = 3
ML_HEADS = 4
ML_HD = 256
PAGE_SIZE = 128
EPS = 1e-6
N_WIDE = 11
N_ROW_SECTIONS = 7
SEC_MQ, SEC_MK, SEC_MV, SEC_MO, SEC_MZ, SEC_GA, SEC_GM = range(N_ROW_SECTIONS)
GATE_LANES = 128
N_INPROJ_OUTS = 10
LOG2_E = 1.4426950408889634
KEY_TILE = 256
MOBA_HEADS_STEP = 4

VMEM_LIMIT_BYTES = 60000 * 1024

NT_DIMS = (((1,), (1,)), ((), ()))
TN_DIMS = (((0,), (0,)), ((), ()))


def _sigmoid(x):
    return 1.0 / (1.0 + jnp.exp(-x))


def _silu(x):
    return x * _sigmoid(x)


def _log_sigmoid(x):
    return jnp.minimum(x, 0.0) - jnp.log1p(jnp.exp(-jnp.abs(x)))


def _params(*semantics):
    return pltpu.CompilerParams(dimension_semantics=semantics, vmem_limit_bytes=VMEM_LIMIT_BYTES)


def _head_norm_rope(t, w, rc, rs1, rs2):
    half = ROT_DIM // 2
    outs = []
    for h in range(ATT_HEADS):
        th = t[:, h * ATT_HD:(h + 1) * ATT_HD]
        y = th * lax.rsqrt(jnp.mean(th * th, axis=-1, keepdims=True) + EPS) * w
        up = pltpu.roll(y, ATT_HD - half, axis=1)
        down = pltpu.roll(y, half, axis=1)
        outs.append(y * rc + up * rs1 + down * rs2)
    return outs


def _cast_kernel(x_ref, o_ref):
    o_ref[...] = x_ref[...].T.astype(o_ref.dtype)


def _cast_wide_weight(w_in, layer, d):
    return pl.pallas_call(
        _cast_kernel,
        grid=(N_WIDE,),
        in_specs=[pl.BlockSpec((None, d, d), lambda j: (layer, j, 0))],
        out_specs=pl.BlockSpec((None, d, d), lambda j: (j, 0, 0)),
        out_shape=jax.ShapeDtypeStruct((N_WIDE, d, d), BF16),
        compiler_params=_params("arbitrary"),
        name="cast_weight",
    )(jnp.swapaxes(w_in, 1, 2))


def _page_specs(n_step, first_page, step_of):
    def spec(p):
        return pl.BlockSpec((1, PAGE_SIZE, ATT_HEADS, ATT_HD),
                            lambda *ids: (ids[-1][first_page + step_of(*ids[:-1]) * n_step + p], 0, 0, 0))
    return [spec(p) for p in range(n_step)]


def _block_means_steps(page_refs, out_ref):
    ppb = MOBA_BLOCK // PAGE_SIZE
    for j in range(len(page_refs) // ppb):
        tot = jnp.sum(page_refs[ppb * j][0], axis=0)
        for p in range(1, ppb):
            tot = tot + jnp.sum(page_refs[ppb * j + p][0], axis=0)
        out_ref[j] = tot * (1.0 / MOBA_BLOCK)
        yield


def _block_means(page_refs, out_ref):
    for _ in _block_means_steps(page_refs, out_ref):
        pass


def _emit_round_robin(streams):
    streams = list(streams)
    while streams:
        for g in list(streams):
            if next(g, "done") == "done":
                streams.remove(g)


def _means_out(n_step, n_steps, step_of):
    ppb = MOBA_BLOCK // PAGE_SIZE
    spec = pl.BlockSpec((n_step // ppb, ATT_HEADS, ATT_HD), lambda *ids: (step_of(*ids[:-1]), 0, 0))
    return spec, jax.ShapeDtypeStruct((n_steps * n_step // ppb, ATT_HEADS, ATT_HD), F32)


def _pin_before_readers(anchor_ref, value):
    zero = pltpu.bitcast(lax.shift_right_logical(pltpu.bitcast(value, jnp.uint32), jnp.uint32(32)), F32)
    zero = jnp.concatenate([zero, zero], axis=0).astype(anchor_ref.dtype)
    anchor_ref[0:16, 0:GATE_LANES] = anchor_ref[0:16, 0:GATE_LANES] + zero


def _inproj_kernel(pt_ref, x_ref, nw_ref, wg_ref, rc_ref, rs1_ref, rs2_ref, qnw_ref, knw_ref, *refs, n_pages):
    del pt_ref
    w_refs, refs = refs[:N_WIDE], refs[N_WIDE:]
    q_ref, k_ref, kb_ref, km_ref, v_ref, vt_ref, az_ref, r_ref, g_ref, gt_ref = refs[n_pages:n_pages + N_INPROJ_OUTS]
    xb_sc = refs[-1]
    d = x_ref.shape[1]
    x = x_ref[...]
    inv_rms = lax.rsqrt(jnp.mean(x * x, axis=-1, keepdims=True) + EPS)
    xw = x * nw_ref[...]
    xb = xw.astype(BF16)
    xb_sc[...] = xb

    n_gate = 2 * ML_HEADS
    x_lo = (xw - xb.astype(F32)).astype(BF16)
    both = (jnp.dot(xb_sc[...], wg_ref[...], preferred_element_type=F32)
            + jnp.dot(x_lo, wg_ref[...], preferred_element_type=F32)) * inv_rms
    lane = lax.broadcasted_iota(jnp.int32, both.shape, 1)
    g = jnp.where(lane < n_gate, both + pltpu.roll(both, GATE_LANES - n_gate, axis=1), 0.0)
    g_ref[...] = g
    gt_ref[...] = g.T

    ppb = MOBA_BLOCK // PAGE_SIZE
    page_blocks = [(refs[ppb * j:ppb * (j + 1)], refs[n_pages + N_INPROJ_OUTS].at[pl.ds(j, 1)])
                   for j in range(n_pages // ppb)]

    def section(i):
        share = -(-len(page_blocks) // N_WIDE)
        for pages, out in page_blocks[i * share:(i + 1) * share]:
            _block_means(pages, out)
            _pin_before_readers(xb_sc, out[0])
        return jnp.dot(xb_sc[...], w_refs[i][...], preferred_element_type=F32) * inv_rms

    rc, rs1, rs2 = rc_ref[...], rs1_ref[...], rs2_ref[...]
    head = lambda h: slice(h * ATT_HD, (h + 1) * ATT_HD)
    for h, t in enumerate(_head_norm_rope(section(0), qnw_ref[...], rc, rs1, rs2)):
        q_ref[h] = t
    for h, t in enumerate(_head_norm_rope(section(1), knw_ref[...], rc, rs1, rs2)):
        k_ref[:, head(h)] = t
        kb_ref[h] = t.astype(BF16)
        km_ref[h, 0] = jnp.broadcast_to(jnp.sum(t, axis=0, keepdims=True) * (1.0 / t.shape[0]), (8, ATT_HD))
    v = section(2)
    v_ref[...] = v
    az = section(3)
    for h in range(ATT_HEADS):
        vt_ref[h] = v[:, head(h)].T.astype(BF16)
        az_ref[h] = az[:, head(h)].astype(BF16)
    for j in range(N_ROW_SECTIONS):
        t = section(4 + j)
        if j == SEC_MK:
            t = t * (ML_HD ** -0.5)
        r_ref[:, j * d:(j + 1) * d] = t.astype(BF16)


def _split_gate_weight(w_gate):
    hi = w_gate.astype(BF16)
    lo = (w_gate - hi.astype(F32)).astype(BF16)
    return jnp.pad(jnp.concatenate([hi, lo], axis=1), ((0, 0), (0, GATE_LANES - 2 * w_gate.shape[1])))


def _inproj(page_table_flat, x, norm_w, w_wide, w_gate2, rope, q_norm_w, k_norm_w, cache_k, tm, n_pos_blocks,
            pages_step, first_page):
    t_rows, d = x.shape
    rc, rs1, rs2 = rope
    n_steps = t_rows // tm
    row = lambda i, pt: (i, 0)
    fixed = lambda i, pt: (0, 0)
    pos = lambda i, pt: (i % n_pos_blocks, 0)
    once = pl.Buffered(1)
    step_of = lambda i: i
    wide_section = lambda j: pl.BlockSpec((None, d, d), lambda i, pt: (j, 0, 0), pipeline_mode=once)
    by_head = pl.BlockSpec((ATT_HEADS, tm, ATT_HD), lambda i, pt: (0, i, 0))
    out_specs = [
        by_head,
        pl.BlockSpec((tm, d), row),
        by_head,
        pl.BlockSpec((ATT_HEADS, 1, 8, ATT_HD), lambda i, pt: (0, i, 0, 0)),
        pl.BlockSpec((tm, d), row),
        pl.BlockSpec((ATT_HEADS, ATT_HD, tm), lambda i, pt: (0, 0, i)),
        by_head,
        pl.BlockSpec((tm, N_ROW_SECTIONS * d), row),
        pl.BlockSpec((tm, GATE_LANES), row),
        pl.BlockSpec((GATE_LANES, tm), lambda i, pt: (0, i)),
    ]
    out_shape = [
        jax.ShapeDtypeStruct((ATT_HEADS, t_rows, ATT_HD), F32),
        jax.ShapeDtypeStruct((t_rows, d), F32),
        jax.ShapeDtypeStruct((ATT_HEADS, t_rows, ATT_HD), BF16),
        jax.ShapeDtypeStruct((ATT_HEADS, n_steps, 8, ATT_HD), F32),
        jax.ShapeDtypeStruct((t_rows, d), F32),
        jax.ShapeDtypeStruct((ATT_HEADS, ATT_HD, t_rows), BF16),
        jax.ShapeDtypeStruct((ATT_HEADS, t_rows, ATT_HD), BF16),
        jax.ShapeDtypeStruct((t_rows, N_ROW_SECTIONS * d), BF16),
        jax.ShapeDtypeStruct((t_rows, GATE_LANES), F32),
        jax.ShapeDtypeStruct((GATE_LANES, t_rows), F32),
    ]
    assert len(out_specs) == N_INPROJ_OUTS
    if pages_step:
        spec, shape = _means_out(pages_step, n_steps, step_of)
        out_specs.append(spec)
        out_shape.append(shape)
    return pl.pallas_call(
        functools.partial(_inproj_kernel, n_pages=pages_step),
        grid_spec=pltpu.PrefetchScalarGridSpec(
            num_scalar_prefetch=1,
            grid=(n_steps,),
            in_specs=[
                pl.BlockSpec((tm, d), row),
                pl.BlockSpec((1, d), fixed),
                pl.BlockSpec((d, GATE_LANES), fixed, pipeline_mode=once),
                pl.BlockSpec((tm, ATT_HD), pos),
                pl.BlockSpec((tm, ATT_HD), pos),
                pl.BlockSpec((tm, ATT_HD), pos),
                pl.BlockSpec((1, ATT_HD), fixed),
                pl.BlockSpec((1, ATT_HD), fixed),
            ] + [wide_section(j) for j in range(N_WIDE)] + _page_specs(pages_step, first_page, step_of),
            out_specs=out_specs,
            scratch_shapes=[pltpu.VMEM((tm, d), BF16)],
        ),
        out_shape=out_shape,
        compiler_params=_params("arbitrary"),
        name="inproj",
    )(page_table_flat, x, norm_w, w_gate2, rc, rs1, rs2, q_norm_w, k_norm_w, *([w_wide] * N_WIDE),
      *([cache_k] * pages_step))


def _rope_tables(pos):
    half = ROT_DIM // 2
    inv = ROPE_THETA ** (-(jnp.arange(half, dtype=F32) * 2.0) / ROT_DIM)
    ang = pos.astype(F32)[:, None] * inv[None, :]
    cos, sin = jnp.cos(ang), jnp.sin(ang)
    n = pos.shape[0]
    zeros = jnp.zeros((n, half), F32)
    tail0 = jnp.zeros((n, ATT_HD - ROT_DIM), F32)
    rc = jnp.concatenate([cos, cos, jnp.ones((n, ATT_HD - ROT_DIM), F32)], axis=-1)
    rs1 = jnp.concatenate([-sin, zeros, tail0], axis=-1)
    rs2 = jnp.concatenate([zeros, sin, tail0], axis=-1)
    return rc, rs1, rs2


def _moba_kernel(pt_ref, q_ref, kb_ref, vt_ref, az_ref, km_ref, *refs, n_pages):
    del pt_ref
    o_all, means_ref, s_all, p_all, qs_all = refs[n_pages:]
    page_refs = refs[:n_pages]
    heads_step = q_ref.shape[0]
    pages_head = n_pages // heads_step
    for hh in range(heads_step):
        _moba_head(q_ref.at[hh], kb_ref.at[hh], vt_ref.at[hh], az_ref.at[hh], km_ref.at[hh], o_all.at[hh],
                   s_all.at[hh], p_all.at[hh], qs_all.at[hh],
                   page_refs[hh * pages_head:(hh + 1) * pages_head],
                   means_ref.at[pl.ds(hh * pages_head // (MOBA_BLOCK // PAGE_SIZE), pages_head // (MOBA_BLOCK // PAGE_SIZE))])


def _moba_head(q_ref, kb_ref, vt_ref, az_ref, km_ref, o_ref, s_sc, p_sc, qs_sc, page_refs, means_ref):
    n_pages = len(page_refs)
    blk = MOBA_BLOCK
    nb = q_ref.shape[0] // blk
    nb_pad = -(-nb // 8) * 8
    rows = lambda j: slice(j * blk, (j + 1) * blk)

    sub8 = lax.broadcasted_iota(jnp.int32, (8, ATT_HD), 0)
    groups = []
    for g0 in range(0, nb, 8):
        tile = jnp.zeros((8, ATT_HD), F32)
        for j in range(g0, min(g0 + 8, nb)):
            tile = jnp.where(sub8 == j - g0, km_ref[j], tile)
        groups.append(tile)
    kmean = groups[0] if len(groups) == 1 else jnp.concatenate(groups, axis=0)

    blk_id = lax.broadcasted_iota(jnp.int32, (nb_pad, blk), 0)
    key_ix = lax.broadcasted_iota(jnp.int32, (blk, blk), 0)
    qry_ix = lax.broadcasted_iota(jnp.int32, (blk, blk), 1)
    causal = key_ix <= qry_ix

    tiles_per_blk = blk // KEY_TILE
    fold = lambda t: t.reshape(KEY_TILE // 8, 8, blk)
    state = {}

    def pass1(c):
        q = q_ref[rows(c), :]
        gate = lax.dot_general(kmean, q, NT_DIMS, preferred_element_type=F32, precision=HIGHEST)
        past = blk_id < c
        gate = jnp.where(past, gate, NEG_INF)
        beaten = jnp.zeros(gate.shape, F32)
        for m in range(c):
            gm = gate[m:m + 1, :]
            wins = jnp.where(gm > gate, 1.0, jnp.where(gm == gate, jnp.where(blk_id > m, 1.0, 0.0), 0.0))
            beaten = beaten + wins
        bias = jnp.where(past, jnp.where(beaten < MOBA_TOPK, 0.0, NEG_INF), NEG_INF)
        qs_sc[c % 2] = (q * (ATT_HD ** -0.5 * LOG2_E)).astype(BF16)
        yield
        for j in range(c + 1):
            for t in range(tiles_per_blk):
                keys = slice(j * blk + t * KEY_TILE, j * blk + (t + 1) * KEY_TILE)
                s = lax.dot_general(kb_ref[keys, :], qs_sc[c % 2], NT_DIMS, preferred_element_type=F32)
                if j == c:
                    s = jnp.where(causal[t * KEY_TILE:(t + 1) * KEY_TILE, :], s, NEG_INF)
                else:
                    s = s + bias[j:j + 1, :]
                s_sc[c % 2, keys, :] = s
                s_max = jnp.max(fold(s), axis=0)
                state[c] = s_max if c not in state else jnp.maximum(state[c], s_max)
                yield

    def pass2(c):
        m_col = jnp.max(state[c], axis=0, keepdims=True)
        n_keys = (c + 1) * blk
        l_acc = jnp.zeros((8, blk), F32)
        for t in range(n_keys // KEY_TILE):
            keys = slice(t * KEY_TILE, (t + 1) * KEY_TILE)
            p = jnp.exp2(s_sc[c % 2, keys, :] - m_col)
            l_acc = l_acc + jnp.sum(fold(p), axis=0)
            p_sc[c % 2, keys, :] = p.astype(BF16)
            yield
        l_col = jnp.sum(l_acc, axis=0, keepdims=True)
        acc = jnp.dot(vt_ref[:, :n_keys], p_sc[c % 2, :n_keys, :], preferred_element_type=F32)
        out = (acc / l_col).T
        o_ref[rows(c), :] = (out * _silu(az_ref[rows(c), :].astype(F32))).astype(BF16)
        yield

    ppb = MOBA_BLOCK // PAGE_SIZE
    rounds = max(nb - 1, 1)
    per_round = -(-n_pages // (rounds * ppb)) * ppb

    def pinned_means(lo, hi, anchor):
        for j in range(lo // ppb, hi // ppb):
            out = means_ref.at[pl.ds(j, 1)]
            _block_means(page_refs[ppb * j:ppb * (j + 1)], out)
            _pin_before_readers(anchor, out[0])
            yield

    _emit_round_robin([pass1(0)])
    for c in range(nb):
        streams = [pass2(c)]
        if c + 1 < nb:
            streams.append(pass1(c + 1))
        lo, hi = min(c * per_round, n_pages), min((c + 1) * per_round, n_pages)
        if hi > lo:
            streams.append(pinned_means(lo, hi, qs_sc.at[(c + 1) % 2 if c + 1 < nb else c % 2]))
        _emit_round_robin(streams)


def _moba(page_table_flat, q_h, kb_h, vt_h, az_h, km, cache_k, batch, seq, pages_step, first_page):
    n_heads, t_rows, _ = q_h.shape
    blk = MOBA_BLOCK
    nb = seq // blk
    hs = MOBA_HEADS_STEP
    rows_of = lambda b, h, pt: (h, b, 0)
    step_of = lambda b, h: b * (ATT_HEADS // hs) + h
    means_spec, means_shape = _means_out(pages_step, batch * ATT_HEADS // hs, step_of)
    return pl.pallas_call(
        functools.partial(_moba_kernel, n_pages=pages_step),
        grid_spec=pltpu.PrefetchScalarGridSpec(
            num_scalar_prefetch=1,
            grid=(batch, ATT_HEADS // hs),
            in_specs=[
                pl.BlockSpec((hs, seq, ATT_HD), rows_of),
                pl.BlockSpec((hs, seq, ATT_HD), rows_of),
                pl.BlockSpec((hs, ATT_HD, seq), lambda b, h, pt: (h, 0, b)),
                pl.BlockSpec((hs, seq, ATT_HD), rows_of),
                pl.BlockSpec((hs, nb, 8, ATT_HD), lambda b, h, pt: (h, b, 0, 0)),
            ] + _page_specs(pages_step, first_page, step_of),
            out_specs=[pl.BlockSpec((hs, seq, ATT_HD), rows_of), means_spec],
            scratch_shapes=[
                pltpu.VMEM((hs, 2, seq, blk), F32),
                pltpu.VMEM((hs, 2, seq, blk), BF16),
                pltpu.VMEM((hs, 2, blk, ATT_HD), BF16),
            ],
        ),
        out_shape=[jax.ShapeDtypeStruct((n_heads, t_rows, ATT_HD), BF16), means_shape],
        compiler_params=_params("arbitrary", "arbitrary"),
        name="moba_prompt",
    )(page_table_flat, q_h, kb_h, vt_h, az_h, km, *([cache_k] * pages_step))


def _mlstm_post(h, mo, mz, nw):
    h = h * _sigmoid(mo)
    h = h * lax.rsqrt(jnp.mean(h * h, axis=-1, keepdims=True) + EPS) * nw
    return h * _silu(mz)


def _dot_rows_hi_lo(mat, row, transposed):
    rows = jnp.broadcast_to(row, (8, row.shape[1]))
    hi = rows.astype(BF16)
    lo = (rows - hi.astype(F32)).astype(BF16)
    if transposed:
        return (lax.dot_general(mat, hi, NT_DIMS, preferred_element_type=F32)
                + lax.dot_general(mat, lo, NT_DIMS, preferred_element_type=F32))
    return jnp.dot(hi, mat, preferred_element_type=F32) + jnp.dot(lo, mat, preferred_element_type=F32)


def _mlstm_kernel(pt_ref, q_ref, k_ref, v_ref, mo_ref, mz_ref, g_ref, gt_ref, bgr_ref, bgc_ref, nw_ref, *refs,
                  n_pages):
    del pt_ref
    hm_ref, c_out, n_out, m_out, means_ref, c_sc, n_sc, m_sc = refs[n_pages:]
    c = pl.program_id(1)
    chunk = q_ref.shape[0]

    @pl.when(c == 0)
    def _():
        c_sc[...] = jnp.zeros(c_sc.shape, F32)
        n_sc[...] = jnp.zeros(n_sc.shape, F32)
        m_sc[...] = jnp.zeros(m_sc.shape, F32)

    _block_means(refs[:n_pages], means_ref)

    gc = g_ref[...] + bgr_ref[...]
    lane = lax.broadcasted_iota(jnp.int32, gc.shape, 1)
    gc = jnp.where(lane >= ML_HEADS, _log_sigmoid(gc), gc)
    gr = gt_ref[0:2 * ML_HEADS, :] + bgc_ref[0:2 * ML_HEADS, :]
    sub = lax.broadcasted_iota(jnp.int32, gr.shape, 0)
    gr = jnp.where(sub >= ML_HEADS, _log_sigmoid(gr), gr)

    t_ix = lax.broadcasted_iota(jnp.int32, (chunk, chunk), 0)
    s_ix = lax.broadcasted_iota(jnp.int32, (chunk, chunk), 1)
    causal = s_ix <= t_ix
    tril = jnp.where(causal, 1.0, 0.0)
    bc = jnp.dot(tril, gc, preferred_element_type=F32, precision=HIGHEST)
    br = lax.dot_general(gr, tril, NT_DIMS, preferred_element_type=F32, precision=HIGHEST)

    for h in range(ML_HEADS):
        cols = slice(h * ML_HD, (h + 1) * ML_HD)
        q, k, v = q_ref[:, cols], k_ref[:, cols], v_ref[:, cols]
        ig_r, b_r = gr[h:h + 1, :], br[ML_HEADS + h:ML_HEADS + h + 1, :]
        ig_c, b_c = gc[:, h:h + 1], bc[:, ML_HEADS + h:ML_HEADS + h + 1]
        m_prev = m_sc[h:h + 1, 0:1]
        c_prev = c_sc[h]
        n_prev = n_sc[h:h + 1, :]

        log_w = jnp.where(causal, b_c - b_r + ig_r, NEG_INF)
        log_inter = b_c + m_prev
        m_t = jnp.maximum(log_inter, jnp.max(log_w, axis=-1, keepdims=True))
        w_intra = jnp.exp(log_w - m_t)
        w_inter = jnp.exp(log_inter - m_t)
        s = lax.dot_general(q, k, NT_DIMS, preferred_element_type=F32) * w_intra
        num = (w_inter * jnp.dot(q, c_prev.astype(BF16), preferred_element_type=F32)
               + jnp.dot(s.astype(BF16), v, preferred_element_type=F32))
        qn = jnp.sum(q.astype(F32) * n_prev, axis=-1, keepdims=True)
        den = w_inter * qn + jnp.sum(s, axis=-1, keepdims=True)
        hh = num / jnp.maximum(jnp.abs(den), jnp.exp(-m_t))

        m_new = m_t[chunk - 1:chunk, :]
        b_last = b_c[chunk - 1:chunk, :]
        a_prev = jnp.exp(b_last + m_prev - m_new)
        a_c = jnp.exp(b_last - b_c + ig_c - m_new)
        a_r = jnp.exp(b_last - b_r + ig_r - m_new)
        kf = k.astype(F32)
        ka = (kf * a_c).astype(BF16)
        c_sc[h] = a_prev * c_prev + lax.dot_general(ka, v, TN_DIMS, preferred_element_type=F32)
        a_rows = jnp.broadcast_to(a_r, (8, chunk))
        n_sc[h:h + 1, :] = a_prev * n_prev + jnp.dot(
            a_rows, kf, preferred_element_type=F32, precision=HIGHEST)[0:1, :]
        m_sc[h:h + 1, :] = jnp.broadcast_to(m_new, (1, m_sc.shape[1]))

        hm_ref[:, cols] = _mlstm_post(
            hh, mo_ref[:, cols].astype(F32), mz_ref[:, cols].astype(F32), nw_ref[...]).astype(BF16)

    @pl.when(c == pl.num_programs(1) - 1)
    def _():
        c_out[0] = c_sc[...]
        n_out[0] = n_sc[...]
        m_out[0] = m_sc[...]


def _mlstm(page_table_flat, r, g, gt, bg_row, bg_col, mh_norm_w, cache_k, batch, seq, chunk, pages_step, first_page):
    t_rows = r.shape[0]
    d = ML_HEADS * ML_HD
    nc = seq // chunk
    sec = lambda j: (lambda b, c, pt: (b * nc + c, j))
    fixed = lambda b, c, pt: (0, 0)
    state = lambda b, c, pt: (b, 0, 0)
    step_of = lambda b, c: b * nc + c
    means_spec, means_shape = _means_out(pages_step, batch * nc, step_of)
    return pl.pallas_call(
        functools.partial(_mlstm_kernel, n_pages=pages_step),
        grid_spec=pltpu.PrefetchScalarGridSpec(
            num_scalar_prefetch=1,
            grid=(batch, nc),
            in_specs=[
                pl.BlockSpec((chunk, d), sec(SEC_MQ)),
                pl.BlockSpec((chunk, d), sec(SEC_MK)),
                pl.BlockSpec((chunk, d), sec(SEC_MV)),
                pl.BlockSpec((chunk, d), sec(SEC_MO)),
                pl.BlockSpec((chunk, d), sec(SEC_MZ)),
                pl.BlockSpec((chunk, GATE_LANES), lambda b, c, pt: (b * nc + c, 0)),
                pl.BlockSpec((GATE_LANES, chunk), lambda b, c, pt: (0, b * nc + c)),
                pl.BlockSpec((1, GATE_LANES), fixed),
                pl.BlockSpec((GATE_LANES, 1), fixed),
                pl.BlockSpec((1, ML_HD), fixed),
            ] + _page_specs(pages_step, first_page, step_of),
            out_specs=[
                pl.BlockSpec((chunk, d), lambda b, c, pt: (b * nc + c, 0)),
                pl.BlockSpec((1, ML_HEADS, ML_HD, ML_HD), lambda b, c, pt: (b, 0, 0, 0)),
                pl.BlockSpec((1, ML_HEADS, ML_HD), state),
                pl.BlockSpec((1, ML_HEADS, GATE_LANES), state),
                means_spec,
            ],
            scratch_shapes=[
                pltpu.VMEM((ML_HEADS, ML_HD, ML_HD), F32),
                pltpu.VMEM((ML_HEADS, ML_HD), F32),
                pltpu.VMEM((ML_HEADS, GATE_LANES), F32),
            ],
        ),
        out_shape=[
            jax.ShapeDtypeStruct((t_rows, d), BF16),
            jax.ShapeDtypeStruct((batch, ML_HEADS, ML_HD, ML_HD), F32),
            jax.ShapeDtypeStruct((batch, ML_HEADS, ML_HD), F32),
            jax.ShapeDtypeStruct((batch, ML_HEADS, GATE_LANES), F32),
            means_shape,
        ],
        compiler_params=_params("arbitrary", "arbitrary"),
        name="mlstm_prompt",
    )(page_table_flat, r, r, r, r, r, g, gt, bg_row, bg_col, mh_norm_w, *([cache_k] * pages_step))


def _sstep_kernel(q_ref, k_ref, v_ref, mo_ref, mz_ref, g_ref, bgr_ref, nw_ref, c_ref, n_ref, m_ref,
                  hm_ref, c_out, n_out, m_out):
    for b in range(q_ref.shape[0]):
        gates = g_ref[b] + bgr_ref[...]
        for h in range(ML_HEADS):
            cols = slice(h * ML_HD, (h + 1) * ML_HD)
            q = q_ref[b, :, cols].astype(F32)
            k = k_ref[b, :, cols].astype(F32)
            v = v_ref[b, :, cols].astype(F32)
            ig = gates[:, h:h + 1]
            lf = _log_sigmoid(gates[:, ML_HEADS + h:ML_HEADS + h + 1])
            m_prev = m_ref[b, :, h:h + 1]
            c_prev = c_ref[b, h]
            n_prev = n_ref[b, h:h + 1, :]

            log_inter = lf + m_prev
            m_t = jnp.maximum(log_inter, ig)
            w_intra = jnp.exp(ig - m_t)
            w_inter = jnp.exp(log_inter - m_t)
            s = jnp.sum(q * k, axis=-1, keepdims=True) * w_intra
            qc = _dot_rows_hi_lo(c_prev.astype(BF16), q, transposed=False)[0:1, :]
            num = w_inter * qc + s * v
            den = w_inter * jnp.sum(q * n_prev, axis=-1, keepdims=True) + s
            hh = num / jnp.maximum(jnp.abs(den), jnp.exp(-m_t))
            hm_ref[b, :, cols] = _mlstm_post(
                hh, mo_ref[b, :, cols].astype(F32), mz_ref[b, :, cols].astype(F32), nw_ref[...]).astype(BF16)

            sub = lax.broadcasted_iota(jnp.int32, (8, ML_HD), 0)
            k8 = jnp.where(sub == 0, jnp.broadcast_to(k, (8, ML_HD)), 0.0)
            v8 = jnp.broadcast_to(v, (8, ML_HD))
            kv = lax.dot_general(k8, v8, TN_DIMS, preferred_element_type=F32, precision=HIGHEST)
            c_out[b, h] = w_inter * c_prev + w_intra * kv
            n_out[b, h:h + 1, :] = w_inter * n_prev + w_intra * k
            m_out[b, :, h:h + 1] = m_t


def _sstep(r3, g3, bg_row, mh_norm_w, c_state, n_state, m_state3, rows_step):
    nb = r3.shape[0]
    d = ML_HEADS * ML_HD
    sec = lambda j: (lambda b: (b, 0, j))
    fixed = lambda b: (0, 0)
    row3 = lambda b: (b, 0, 0)
    state4 = lambda b: (b, 0, 0, 0)
    return pl.pallas_call(
        _sstep_kernel,
        grid=(nb // rows_step,),
        in_specs=[
            pl.BlockSpec((rows_step, 1, d), sec(SEC_MQ)),
            pl.BlockSpec((rows_step, 1, d), sec(SEC_MK)),
            pl.BlockSpec((rows_step, 1, d), sec(SEC_MV)),
            pl.BlockSpec((rows_step, 1, d), sec(SEC_MO)),
            pl.BlockSpec((rows_step, 1, d), sec(SEC_MZ)),
            pl.BlockSpec((rows_step, 1, GATE_LANES), row3),
            pl.BlockSpec((1, GATE_LANES), fixed),
            pl.BlockSpec((1, ML_HD), fixed),
            pl.BlockSpec((rows_step, ML_HEADS, ML_HD, ML_HD), state4),
            pl.BlockSpec((rows_step, ML_HEADS, ML_HD), row3),
            pl.BlockSpec((rows_step, 1, ML_HEADS), row3),
        ],
        out_specs=[
            pl.BlockSpec((rows_step, 1, d), row3),
            pl.BlockSpec((rows_step, ML_HEADS, ML_HD, ML_HD), state4),
            pl.BlockSpec((rows_step, ML_HEADS, ML_HD), row3),
            pl.BlockSpec((rows_step, 1, ML_HEADS), row3),
        ],
        out_shape=[
            jax.ShapeDtypeStruct((nb, 1, d), BF16),
            jax.ShapeDtypeStruct(c_state.shape, F32),
            jax.ShapeDtypeStruct(n_state.shape, F32),
            jax.ShapeDtypeStruct(m_state3.shape, F32),
        ],
        compiler_params=_params("arbitrary"),
        name="mlstm_sample",
    )(r3, r3, r3, r3, r3, g3, bg_row, mh_norm_w, c_state, n_state, m_state3)


def _sgate_kernel(q_ref, kmean_ref, sel_ref):
    n_blocks = kmean_ref.shape[1]
    for b in range(q_ref.shape[0]):
        gate = jnp.sum(kmean_ref[b] * q_ref[b], axis=-1, keepdims=True)
        row = lax.broadcasted_iota(jnp.int32, gate.shape, 0).astype(F32)
        lane = lax.broadcasted_iota(jnp.int32, (ATT_HEADS, GATE_LANES), 1)
        picked = jnp.zeros((ATT_HEADS, GATE_LANES), F32)
        for j in range(MOBA_TOPK):
            best = jnp.max(gate, axis=0, keepdims=True)
            idx = jnp.min(jnp.where(gate == best, row, float(n_blocks)), axis=0, keepdims=True)
            picked = jnp.where(lane == j, idx[0], picked)
            gate = jnp.where(row == idx, NEG_INF, gate)
        sel_ref[b] = picked.astype(jnp.int32)


def _sgate(q_heads, kmean):
    nb, n_blocks = kmean.shape[:2]
    rows = _row_tile(nb, 8)
    return pl.pallas_call(
        _sgate_kernel,
        grid=(nb // rows,),
        in_specs=[pl.BlockSpec((rows, ATT_HEADS, ATT_HD), lambda b: (b, 0, 0)),
                  pl.BlockSpec((rows, n_blocks, ATT_HEADS, ATT_HD), lambda b: (b, 0, 0, 0))],
        out_specs=pl.BlockSpec((rows, ATT_HEADS, GATE_LANES), lambda b: (b, 0, 0)),
        out_shape=jax.ShapeDtypeStruct((nb, ATT_HEADS, GATE_LANES), jnp.int32),
        compiler_params=_params("arbitrary"),
        name="moba_sample_gate",
    )(q_heads, kmean)


def _sattn_copies(pt_ref, sel_ref, ck_ref, cv_ref, kbuf, vbuf, sems, b, slot):
    ppb = MOBA_BLOCK // PAGE_SIZE
    copies = []
    for h in range(ATT_HEADS):
        for j in range(MOBA_TOPK):
            block = sel_ref[b, h * MOBA_TOPK + j]
            for p in range(ppb):
                page = pt_ref[b, block * ppb + p]
                rows = pl.ds((j * ppb + p) * PAGE_SIZE, PAGE_SIZE)
                copies.append(pltpu.make_async_copy(ck_ref.at[page, :, h, :], kbuf.at[slot, h, rows, :], sems.at[0, slot]))
                copies.append(pltpu.make_async_copy(cv_ref.at[page, :, h, :], vbuf.at[slot, h, rows, :], sems.at[1, slot]))
    return copies


def _sattn_row(q, kn, vn, az, k_sel, v_sel):
    qs = q * (ATT_HD ** -0.5)
    s_new = jnp.sum(qs * kn, axis=-1, keepdims=True)
    sub = lax.broadcasted_iota(jnp.int32, (ATT_HEADS, ATT_HD), 0)
    out = jnp.zeros((ATT_HEADS, ATT_HD), F32)
    for h in range(ATT_HEADS):
        s = jnp.sum(k_sel[h] * qs[h:h + 1, :], axis=-1, keepdims=True)
        sn = s_new[h:h + 1, :]
        m = jnp.maximum(sn, jnp.max(s, axis=0, keepdims=True))
        p = jnp.exp(s - m)
        pn = jnp.exp(sn - m)
        l = pn + jnp.sum(p, axis=0, keepdims=True)
        acc = pn * vn[h:h + 1, :] + jnp.sum(p * v_sel[h], axis=0, keepdims=True)
        out = jnp.where(sub == h, acc / l, out)
    return out * _silu(az.astype(F32))


def _merge_kernel(*refs, n_side_rows, rows_step):
    if n_side_rows:
        pt_ref, sel_ref, refs = refs[0], refs[1], refs[2:]
    x_ref, am_ref, hm_ref, ga_ref, gm_ref, wa_ref, wm_ref, wo_ref = refs[:8]
    if n_side_rows:
        q_ref, kn_ref, vn_ref, az_ref, ck_ref, cv_ref, y_ref, o_ref, kbuf, vbuf, sems, mix_sc = refs[8:]
        side_tiles = []
        fetch = functools.partial(_sattn_copies, pt_ref, sel_ref, ck_ref, cv_ref, kbuf, vbuf, sems)
        step = pl.program_id(0)
        last_block = n_side_rows // rows_step - 1

        def row_of(s, j):
            return jnp.minimum(s, last_block) * rows_step + j

        def after(s, j):
            return (s, j + 1) if j + 1 < rows_step else (s + 1, 0)

        @pl.when(step == 0)
        def _():
            for cp in fetch(row_of(0, 0), 0):
                cp.start()

        for j in range(rows_step):
            cnt = step * rows_step + j
            for cp in fetch(row_of(*after(step, j)), (cnt + 1) % 2):
                cp.start()
            for cp in fetch(row_of(step, j), cnt % 2):
                cp.wait()
            side_out = _sattn_row(q_ref[j], kn_ref[j], vn_ref[j], az_ref[j], kbuf.at[cnt % 2], vbuf.at[cnt % 2])
            o_ref[j] = side_out.astype(BF16)
            side_tiles.append(side_out)
    else:
        y_ref = refs[8]

    am = jnp.concatenate([am_ref[h] for h in range(ATT_HEADS)], axis=1)
    y_att = jnp.dot(am, wa_ref[...], preferred_element_type=F32)
    y_ml = jnp.dot(hm_ref[...], wm_ref[...], preferred_element_type=F32)
    mix = _sigmoid(ga_ref[...].astype(F32)) * y_att + _sigmoid(gm_ref[...].astype(F32)) * y_ml
    if n_side_rows:
        mix_sc[...] = mix.astype(BF16)
        for tile in side_tiles:
            _pin_before_readers(mix_sc, tile)
        y_ref[...] = x_ref[...] + jnp.dot(mix_sc[...], wo_ref[...], preferred_element_type=F32)

        @pl.when(step == pl.num_programs(0) - 1)
        def _():
            for cp in fetch(row_of(step + 1, 0), ((step + 1) * rows_step) % 2):
                cp.wait()
    else:
        y_ref[...] = x_ref[...] + jnp.dot(mix.astype(BF16), wo_ref[...], preferred_element_type=F32)


def _merge(x, am, hm, r, wa, wm, wo, tm, side=None):
    t_rows, d = x.shape
    n_steps = t_rows // tm
    row = lambda i, *_: (i, 0)
    fixed = lambda i, *_: (0, 0)
    in_specs = [
        pl.BlockSpec((tm, d), row),
        pl.BlockSpec((ATT_HEADS, tm, ATT_HD), lambda i, *_: (0, i, 0)),
        pl.BlockSpec((tm, d), row),
        pl.BlockSpec((tm, d), lambda i, *_: (i, SEC_GA)),
        pl.BlockSpec((tm, d), lambda i, *_: (i, SEC_GM)),
        pl.BlockSpec((d, d), fixed),
        pl.BlockSpec((d, d), fixed),
        pl.BlockSpec((d, d), fixed),
    ]
    y_spec = pl.BlockSpec((tm, d), row)
    y_shape = jax.ShapeDtypeStruct((t_rows, d), F32)
    if side is None:
        return pl.pallas_call(
            functools.partial(_merge_kernel, n_side_rows=0, rows_step=0),
            grid=(n_steps,), in_specs=in_specs, out_specs=y_spec, out_shape=y_shape,
            compiler_params=_params("arbitrary"), name="merge",
        )(x, am, hm, r, r, wa, wm, wo)

    page_table, sel, q_heads, k_heads, v_heads, az_heads, cache_k, cache_v = side
    n_rows = q_heads.shape[0]
    rows_step = next(c for c in range(1, n_rows + 1) if n_rows % c == 0 and n_rows // c <= n_steps)
    n_blocks = n_rows // rows_step
    n_keys = MOBA_TOPK * MOBA_BLOCK
    head_spec = pl.BlockSpec((rows_step, ATT_HEADS, ATT_HD), lambda i, *_: (jnp.minimum(i, n_blocks - 1), 0, 0))
    return pl.pallas_call(
        functools.partial(_merge_kernel, n_side_rows=n_rows, rows_step=rows_step),
        grid_spec=pltpu.PrefetchScalarGridSpec(
            num_scalar_prefetch=2,
            grid=(n_steps,),
            in_specs=in_specs + [head_spec, head_spec, head_spec, head_spec,
                                 pl.BlockSpec(memory_space=pl.ANY), pl.BlockSpec(memory_space=pl.ANY)],
            out_specs=[y_spec, head_spec],
            scratch_shapes=[
                pltpu.VMEM((2, ATT_HEADS, n_keys, ATT_HD), F32),
                pltpu.VMEM((2, ATT_HEADS, n_keys, ATT_HD), F32),
                pltpu.SemaphoreType.DMA((2, 2)),
                pltpu.VMEM((tm, d), BF16),
            ],
        ),
        out_shape=[y_shape, jax.ShapeDtypeStruct((n_rows, ATT_HEADS, ATT_HD), BF16)],
        compiler_params=_params("arbitrary"),
        name="merge_with_sample_attend",
    )(page_table, sel, x, am, hm, r, r, wa, wm, wo, q_heads, k_heads, v_heads, az_heads, cache_k, cache_v)


def _row_tile(rows, target):
    tm = min(rows, target)
    while rows % tm:
        tm //= 2
    return tm


def kernel(x_prompt, x_sample, cache_k, cache_v, page_table, state_mlstm_C, state_mlstm_n, state_mlstm_m,
           norm_w, w_in, b_gates, q_norm_w, k_norm_w, mh_norm_w, w_proj_attn, w_proj_mlstm, w_out):
    batch, seq, d = x_prompt.shape
    dec_batch, dec_seq, _ = x_sample.shape
    assert dec_seq == 1 and d == ATT_HEADS * ATT_HD == ML_HEADS * ML_HD
    depth = w_in.shape[0]
    n_pages = page_table.shape[1]
    past = n_pages * PAGE_SIZE
    assert past % MOBA_BLOCK == 0 and seq % MOBA_BLOCK == 0
    t_p, t_s = batch * seq, dec_batch * dec_seq

    tm_proj = MOBA_BLOCK
    assert seq % tm_proj == 0
    chunk = _row_tile(seq, 256)
    rope_p = _rope_tables(jnp.arange(seq, dtype=jnp.int32))
    rope_s = _rope_tables(jnp.tile(past + jnp.arange(dec_seq, dtype=jnp.int32), dec_batch))

    ppb = MOBA_BLOCK // PAGE_SIZE
    steps = (t_p // tm_proj, batch * ATT_HEADS // MOBA_HEADS_STEP, batch * (seq // chunk))
    weights = (3, 4, 4)
    total_pages = dec_batch * n_pages
    unit = -(-total_pages // (sum(s * w for s, w in zip(steps, weights)) * ppb)) * ppb
    share = tuple(unit * w for w in weights)
    first = (0, steps[0] * share[0], steps[0] * share[0] + steps[1] * share[1])
    capacity = first[2] + steps[2] * share[2]
    pt_flat = jnp.pad(page_table.reshape(-1), (0, capacity - total_pages))

    y_p = x_prompt.reshape(t_p, d)
    y_s = x_sample.reshape(t_s, d)
    outs = [[] for _ in range(10)]
    for l in range(depth):
        w_wide = _cast_wide_weight(w_in, l, d)
        w_gate2 = _split_gate_weight(w_in[l][:, N_WIDE * d:])
        bg_row = jnp.pad(b_gates[l].astype(F32), (0, GATE_LANES - 2 * ML_HEADS)).reshape(1, GATE_LANES)
        bg_col = bg_row.reshape(GATE_LANES, 1)
        nw = norm_w[l].reshape(1, d)
        qnw, knw = q_norm_w[l].reshape(1, ATT_HD), k_norm_w[l].reshape(1, ATT_HD)
        mhw = mh_norm_w[l].reshape(1, ML_HD)
        wa, wm, wo = w_proj_attn[l].astype(BF16), w_proj_mlstm[l].astype(BF16), w_out[l].astype(BF16)

        q_h, k, kb_h, km, v, vt_h, az_h, r, g, gt, means0 = _inproj(
            pt_flat, y_p, nw, w_wide, w_gate2, rope_p, qnw, knw, cache_k[l], tm_proj, seq // tm_proj,
            share[0], first[0])
        am_h, means1 = _moba(pt_flat, q_h, kb_h, vt_h, az_h, km, cache_k[l], batch, seq, share[1], first[1])
        hm, c_new, n_new, m_new, means2 = _mlstm(pt_flat, r, g, gt, bg_row, bg_col, mhw, cache_k[l], batch, seq, chunk,
                                                 share[2], first[2])
        for dst, val in zip(outs[:5], (k.reshape(batch, seq, ATT_HEADS, ATT_HD), v.reshape(batch, seq, ATT_HEADS, ATT_HD),
                                       c_new, n_new, m_new[:, :, 0])):
            dst.append(val)
        kmean = jnp.concatenate([means0, means1, means2], axis=0)[:total_pages // ppb]
        kmean = kmean.reshape(dec_batch, n_pages // ppb, ATT_HEADS, ATT_HD)
        r_p = r

        q_h, k, _, _, v, _, az_h, r, g, _ = _inproj(
            pt_flat, y_s, nw, w_wide, w_gate2, rope_s, qnw, knw, cache_k[l], t_s, 1, 0, 0)
        r3 = r.reshape(t_s, 1, r.shape[-1])
        by_head = lambda t: t.reshape(t_s, ATT_HEADS, ATT_HD)
        rows_major = lambda t: jnp.swapaxes(t, 0, 1)
        q_rows = rows_major(q_h)
        sel = _sgate(q_rows, kmean)
        sel = sel[:, :, :MOBA_TOPK].reshape(t_s, ATT_HEADS * MOBA_TOPK)
        y_p, am_s = _merge(y_p, am_h, hm, r_p, wa, wm, wo, _row_tile(t_p, 512),
                           side=(page_table, sel, q_rows, by_head(k), by_head(v), rows_major(az_h),
                                 cache_k[l], cache_v[l]))
        hm, c_new, n_new, m_new = _sstep(
            r3, g.reshape(t_s, 1, GATE_LANES), bg_row, mhw,
            state_mlstm_C[l], state_mlstm_n[l], state_mlstm_m[l].reshape(dec_batch, 1, ML_HEADS), _row_tile(t_s, 4))
        y_s = _merge(y_s, rows_major(am_s), hm.reshape(t_s, d), r, wa, wm, wo, t_s)
        for dst, val in zip(outs[5:], (k.reshape(dec_batch, dec_seq, ATT_HEADS, ATT_HD),
                                       v.reshape(dec_batch, dec_seq, ATT_HEADS, ATT_HD),
                                       c_new, n_new, m_new.reshape(dec_batch, ML_HEADS))):
            dst.append(val)

    st = state_mlstm_C.dtype
    k_p, v_p, c_p, n_p, m_p, k_s, v_s, c_s, n_s, m_s = (jnp.stack(o) for o in outs)
    return (y_p.reshape(batch, seq, d), y_s.reshape(dec_batch, dec_seq, d),
            k_p, v_p, c_p.astype(st), n_p.astype(st), m_p.astype(st),
            k_s, v_s, c_s.astype(st), n_s.astype(st), m_s.astype(st))
```

```python
import functools

import jax
import jax.numpy as jnp
from jax import lax
from jax.experimental import pallas as pl
from jax.experimental.pallas import tpu as pltpu

F32 = jnp.float32
BF16 = jnp.bfloat16
HIGHEST = lax.Precision.HIGHEST
NEG_INF = float("-inf")

ATT_HEADS = 8
ATT_HD = 128
ROT_DIM = ATT_HD // 4
ROPE_THETA = 500000.0
MOBA_BLOCK = 256
MOBA_TOPK = 3
ML_HEADS = 4
ML_HD = 256
PAGE_SIZE = 128
EPS = 1e-6
N_WIDE = 11
N_ROW_SECTIONS = 7
SEC_MQ, SEC_MK, SEC_MV, SEC_MO, SEC_MZ, SEC_GA, SEC_GM = range(N_ROW_SECTIONS)
GATE_LANES = 128
N_INPROJ_OUTS = 10
LOG2_E = 1.4426950408889634
KEY_TILE = 256
MOBA_HEADS_STEP = 4
SAMPLE_K_SPLIT = 8

VMEM_LIMIT_BYTES = 60000 * 1024

NT_DIMS = (((1,), (1,)), ((), ()))
TN_DIMS = (((0,), (0,)), ((), ()))


def _sigmoid(x):
    return 1.0 / (1.0 + jnp.exp(-x))


def _silu(x):
    return x * _sigmoid(x)


def _log_sigmoid(x):
    return jnp.minimum(x, 0.0) - jnp.log1p(jnp.exp(-jnp.abs(x)))


def _params(*semantics):
    return pltpu.CompilerParams(dimension_semantics=semantics, vmem_limit_bytes=VMEM_LIMIT_BYTES)


def _head_norm_rope(t, w, rc, rs1, rs2):
    half = ROT_DIM // 2
    outs = []
    for h in range(ATT_HEADS):
        th = t[:, h * ATT_HD:(h + 1) * ATT_HD]
        y = th * lax.rsqrt(jnp.mean(th * th, axis=-1, keepdims=True) + EPS) * w
        up = pltpu.roll(y, ATT_HD - half, axis=1)
        down = pltpu.roll(y, half, axis=1)
        outs.append(y * rc + up * rs1 + down * rs2)
    return outs


def _cast_kernel(x_ref, o_ref):
    o_ref[...] = x_ref[...].T.astype(o_ref.dtype)


def _cast_wide_weight(w_in, layer, d):
    return pl.pallas_call(
        _cast_kernel,
        grid=(N_WIDE,),
        in_specs=[pl.BlockSpec((None, d, d), lambda j: (layer, j, 0))],
        out_specs=pl.BlockSpec((None, d, d), lambda j: (j, 0, 0)),
        out_shape=jax.ShapeDtypeStruct((N_WIDE, d, d), BF16),
        compiler_params=_params("arbitrary"),
        name="cast_weight",
    )(jnp.swapaxes(w_in, 1, 2))


def _page_specs(n_step, first_page, step_of):
    def spec(p):
        return pl.BlockSpec((1, PAGE_SIZE, ATT_HEADS, ATT_HD),
                            lambda *ids: (ids[-1][first_page + step_of(*ids[:-1]) * n_step + p], 0, 0, 0))
    return [spec(p) for p in range(n_step)]


def _block_means_steps(page_refs, out_ref):
    ppb = MOBA_BLOCK // PAGE_SIZE
    for j in range(len(page_refs) // ppb):
        tot = jnp.sum(page_refs[ppb * j][0], axis=0)
        for p in range(1, ppb):
            tot = tot + jnp.sum(page_refs[ppb * j + p][0], axis=0)
        out_ref[j] = tot * (1.0 / MOBA_BLOCK)
        yield


def _block_means(page_refs, out_ref):
    for _ in _block_means_steps(page_refs, out_ref):
        pass


def _emit_round_robin(streams):
    streams = list(streams)
    while streams:
        for g in list(streams):
            if next(g, "done") == "done":
                streams.remove(g)


def _means_out(n_step, n_steps, step_of):
    ppb = MOBA_BLOCK // PAGE_SIZE
    spec = pl.BlockSpec((n_step // ppb, ATT_HEADS, ATT_HD), lambda *ids: (step_of(*ids[:-1]), 0, 0))
    return spec, jax.ShapeDtypeStruct((n_steps * n_step // ppb, ATT_HEADS, ATT_HD), F32)


def _pin_before_readers(anchor_ref, value):
    zero = pltpu.bitcast(lax.shift_right_logical(pltpu.bitcast(value, jnp.uint32), jnp.uint32(32)), F32)
    zero = jnp.concatenate([zero, zero], axis=0).astype(anchor_ref.dtype)
    anchor_ref[0:16, 0:GATE_LANES] = anchor_ref[0:16, 0:GATE_LANES] + zero


def _inproj_kernel(pt_ref, x_ref, nw_ref, wg_ref, rc_ref, rs1_ref, rs2_ref, qnw_ref, knw_ref, *refs, n_pages,
                   k_split):
    del pt_ref
    w_refs, refs = refs[:N_WIDE * k_split], refs[N_WIDE * k_split:]
    q_ref, k_ref, kb_ref, km_ref, v_ref, vt_ref, az_ref, r_ref, g_ref, gt_ref = refs[n_pages:n_pages + N_INPROJ_OUTS]
    xb_sc = refs[-1]
    d = x_ref.shape[1]
    x = x_ref[...]
    inv_rms = lax.rsqrt(jnp.mean(x * x, axis=-1, keepdims=True) + EPS)
    xw = x * nw_ref[...]
    xb = xw.astype(BF16)
    xb_sc[...] = xb

    n_gate = 2 * ML_HEADS
    x_lo = (xw - xb.astype(F32)).astype(BF16)
    both = (jnp.dot(xb_sc[...], wg_ref[...], preferred_element_type=F32)
            + jnp.dot(x_lo, wg_ref[...], preferred_element_type=F32)) * inv_rms
    lane = lax.broadcasted_iota(jnp.int32, both.shape, 1)
    g = jnp.where(lane < n_gate, both + pltpu.roll(both, GATE_LANES - n_gate, axis=1), 0.0)
    g_ref[...] = g
    gt_ref[...] = g.T

    ppb = MOBA_BLOCK // PAGE_SIZE
    page_blocks = [(refs[ppb * j:ppb * (j + 1)], refs[n_pages + N_INPROJ_OUTS].at[pl.ds(j, 1)])
                   for j in range(n_pages // ppb)]

    def section(i):
        share = -(-len(page_blocks) // N_WIDE)
        for pages, out in page_blocks[i * share:(i + 1) * share]:
            _block_means(pages, out)
            _pin_before_readers(xb_sc, out[0])
        kc = d // k_split
        acc = jnp.dot(xb_sc[:, 0:kc], w_refs[i * k_split][...], preferred_element_type=F32)
        for kk in range(1, k_split):
            acc = acc + jnp.dot(xb_sc[:, kk * kc:(kk + 1) * kc], w_refs[i * k_split + kk][...],
                                preferred_element_type=F32)
        return acc * inv_rms

    rc, rs1, rs2 = rc_ref[...], rs1_ref[...], rs2_ref[...]
    head = lambda h: slice(h * ATT_HD, (h + 1) * ATT_HD)
    for h, t in enumerate(_head_norm_rope(section(0), qnw_ref[...], rc, rs1, rs2)):
        q_ref[h] = t
    for h, t in enumerate(_head_norm_rope(section(1), knw_ref[...], rc, rs1, rs2)):
        k_ref[:, head(h)] = t
        kb_ref[h] = t.astype(BF16)
        km_ref[h, 0] = jnp.broadcast_to(jnp.sum(t, axis=0, keepdims=True) * (1.0 / t.shape[0]), (8, ATT_HD))
    v = section(2)
    v_ref[...] = v
    az = section(3)
    for h in range(ATT_HEADS):
        vt_ref[h] = v[:, head(h)].T.astype(BF16)
        az_ref[h] = az[:, head(h)].astype(BF16)
    for j in range(N_ROW_SECTIONS):
        t = section(4 + j)
        if j == SEC_MK:
            t = t * (ML_HD ** -0.5)
        r_ref[:, j * d:(j + 1) * d] = t.astype(BF16)


def _split_gate_weight(w_gate):
    hi = w_gate.astype(BF16)
    lo = (w_gate - hi.astype(F32)).astype(BF16)
    return jnp.pad(jnp.concatenate([hi, lo], axis=1), ((0, 0), (0, GATE_LANES - 2 * w_gate.shape[1])))


def _inproj(page_table_flat, x, norm_w, w_wide, w_gate2, rope, q_norm_w, k_norm_w, cache_k, tm, n_pos_blocks,
            pages_step, first_page, k_split=1):
    t_rows, d = x.shape
    rc, rs1, rs2 = rope
    n_steps = t_rows // tm
    row = lambda i, pt: (i, 0)
    fixed = lambda i, pt: (0, 0)
    pos = lambda i, pt: (i % n_pos_blocks, 0)
    once = pl.Buffered(1)
    step_of = lambda i: i
    w_blocks = w_wide.reshape(N_WIDE * k_split, d // k_split, d)
    wide_section = lambda j: pl.BlockSpec((None, d // k_split, d), lambda i, pt: (j, 0, 0), pipeline_mode=once)
    by_head = pl.BlockSpec((ATT_HEADS, tm, ATT_HD), lambda i, pt: (0, i, 0))
    out_specs = [
        by_head,
        pl.BlockSpec((tm, d), row),
        by_head,
        pl.BlockSpec((ATT_HEADS, 1, 8, ATT_HD), lambda i, pt: (0, i, 0, 0)),
        pl.BlockSpec((tm, d), row),
        pl.BlockSpec((ATT_HEADS, ATT_HD, tm), lambda i, pt: (0, 0, i)),
        by_head,
        pl.BlockSpec((tm, N_ROW_SECTIONS * d), row),
        pl.BlockSpec((tm, GATE_LANES), row),
        pl.BlockSpec((GATE_LANES, tm), lambda i, pt: (0, i)),
    ]
    out_shape = [
        jax.ShapeDtypeStruct((ATT_HEADS, t_rows, ATT_HD), F32),
        jax.ShapeDtypeStruct((t_rows, d), F32),
        jax.ShapeDtypeStruct((ATT_HEADS, t_rows, ATT_HD), BF16),
        jax.ShapeDtypeStruct((ATT_HEADS, n_steps, 8, ATT_HD), F32),
        jax.ShapeDtypeStruct((t_rows, d), F32),
        jax.ShapeDtypeStruct((ATT_HEADS, ATT_HD, t_rows), BF16),
        jax.ShapeDtypeStruct((ATT_HEADS, t_rows, ATT_HD), BF16),
        jax.ShapeDtypeStruct((t_rows, N_ROW_SECTIONS * d), BF16),
        jax.ShapeDtypeStruct((t_rows, GATE_LANES), F32),
        jax.ShapeDtypeStruct((GATE_LANES, t_rows), F32),
    ]
    assert len(out_specs) == N_INPROJ_OUTS
    if pages_step:
        spec, shape = _means_out(pages_step, n_steps, step_of)
        out_specs.append(spec)
        out_shape.append(shape)
    return pl.pallas_call(
        functools.partial(_inproj_kernel, n_pages=pages_step, k_split=k_split),
        grid_spec=pltpu.PrefetchScalarGridSpec(
            num_scalar_prefetch=1,
            grid=(n_steps,),
            in_specs=[
                pl.BlockSpec((tm, d), row),
                pl.BlockSpec((1, d), fixed),
                pl.BlockSpec((d, GATE_LANES), fixed, pipeline_mode=once),
                pl.BlockSpec((tm, ATT_HD), pos),
                pl.BlockSpec((tm, ATT_HD), pos),
                pl.BlockSpec((tm, ATT_HD), pos),
                pl.BlockSpec((1, ATT_HD), fixed),
                pl.BlockSpec((1, ATT_HD), fixed),
            ] + [wide_section(j) for j in range(N_WIDE * k_split)] + _page_specs(pages_step, first_page, step_of),
            out_specs=out_specs,
            scratch_shapes=[pltpu.VMEM((tm, d), BF16)],
        ),
        out_shape=out_shape,
        compiler_params=_params("arbitrary"),
        name="inproj",
    )(page_table_flat, x, norm_w, w_gate2, rc, rs1, rs2, q_norm_w, k_norm_w, *([w_blocks] * (N_WIDE * k_split)),
      *([cache_k] * pages_step))


def _rope_tables(pos):
    half = ROT_DIM // 2
    inv = ROPE_THETA ** (-(jnp.arange(half, dtype=F32) * 2.0) / ROT_DIM)
    ang = pos.astype(F32)[:, None] * inv[None, :]
    cos, sin = jnp.cos(ang), jnp.sin(ang)
    n = pos.shape[0]
    zeros = jnp.zeros((n, half), F32)
    tail0 = jnp.zeros((n, ATT_HD - ROT_DIM), F32)
    rc = jnp.concatenate([cos, cos, jnp.ones((n, ATT_HD - ROT_DIM), F32)], axis=-1)
    rs1 = jnp.concatenate([-sin, zeros, tail0], axis=-1)
    rs2 = jnp.concatenate([zeros, sin, tail0], axis=-1)
    return rc, rs1, rs2


def _moba_kernel(pt_ref, q_ref, kb_ref, vt_ref, az_ref, km_ref, *refs, n_pages):
    del pt_ref
    o_all, means_ref, s_all, p_all, qs_all = refs[n_pages:]
    page_refs = refs[:n_pages]
    heads_step = q_ref.shape[0]
    pages_head = n_pages // heads_step
    for hh in range(heads_step):
        _moba_head(q_ref.at[hh], kb_ref.at[hh], vt_ref.at[hh], az_ref.at[hh], km_ref.at[hh], o_all.at[hh],
                   s_all.at[hh], p_all.at[hh], qs_all.at[hh],
                   page_refs[hh * pages_head:(hh + 1) * pages_head],
                   means_ref.at[pl.ds(hh * pages_head // (MOBA_BLOCK // PAGE_SIZE), pages_head // (MOBA_BLOCK // PAGE_SIZE))])


def _moba_head(q_ref, kb_ref, vt_ref, az_ref, km_ref, o_ref, s_sc, p_sc, qs_sc, page_refs, means_ref):
    n_pages = len(page_refs)
    blk = MOBA_BLOCK
    nb = q_ref.shape[0] // blk
    nb_pad = -(-nb // 8) * 8
    rows = lambda j: slice(j * blk, (j + 1) * blk)

    sub8 = lax.broadcasted_iota(jnp.int32, (8, ATT_HD), 0)
    groups = []
    for g0 in range(0, nb, 8):
        tile = jnp.zeros((8, ATT_HD), F32)
        for j in range(g0, min(g0 + 8, nb)):
            tile = jnp.where(sub8 == j - g0, km_ref[j], tile)
        groups.append(tile)
    kmean = groups[0] if len(groups) == 1 else jnp.concatenate(groups, axis=0)

    blk_id = lax.broadcasted_iota(jnp.int32, (nb_pad, blk), 0)
    key_ix = lax.broadcasted_iota(jnp.int32, (blk, blk), 0)
    qry_ix = lax.broadcasted_iota(jnp.int32, (blk, blk), 1)
    causal = key_ix <= qry_ix

    tiles_per_blk = blk // KEY_TILE
    fold = lambda t: t.reshape(KEY_TILE // 8, 8, blk)
    state = {}

    def pass1(c):
        q = q_ref[rows(c), :]
        gate = lax.dot_general(kmean, q, NT_DIMS, preferred_element_type=F32, precision=HIGHEST)
        past = blk_id < c
        gate = jnp.where(past, gate, NEG_INF)
        beaten = jnp.zeros(gate.shape, F32)
        for m in range(c):
            gm = gate[m:m + 1, :]
            wins = jnp.where(gm > gate, 1.0, jnp.where(gm == gate, jnp.where(blk_id > m, 1.0, 0.0), 0.0))
            beaten = beaten + wins
        bias = jnp.where(past, jnp.where(beaten < MOBA_TOPK, 0.0, NEG_INF), NEG_INF)
        qs_sc[c % 2] = (q * (ATT_HD ** -0.5 * LOG2_E)).astype(BF16)
        yield
        for j in range(c + 1):
            for t in range(tiles_per_blk):
                keys = slice(j * blk + t * KEY_TILE, j * blk + (t + 1) * KEY_TILE)
                s = lax.dot_general(kb_ref[keys, :], qs_sc[c % 2], NT_DIMS, preferred_element_type=F32)
                if j == c:
                    s = jnp.where(causal[t * KEY_TILE:(t + 1) * KEY_TILE, :], s, NEG_INF)
                else:
                    s = s + bias[j:j + 1, :]
                s_sc[c % 2, keys, :] = s
                s_max = jnp.max(fold(s), axis=0)
                state[c] = s_max if c not in state else jnp.maximum(state[c], s_max)
                yield

    def pass2(c):
        m_col = jnp.max(state[c], axis=0, keepdims=True)
        n_keys = (c + 1) * blk
        l_acc = jnp.zeros((8, blk), F32)
        for t in range(n_keys // KEY_TILE):
            keys = slice(t * KEY_TILE, (t + 1) * KEY_TILE)
            p = jnp.exp2(s_sc[c % 2, keys, :] - m_col)
            l_acc = l_acc + jnp.sum(fold(p), axis=0)
            p_sc[c % 2, keys, :] = p.astype(BF16)
            yield
        l_col = jnp.sum(l_acc, axis=0, keepdims=True)
        acc = jnp.dot(vt_ref[:, :n_keys], p_sc[c % 2, :n_keys, :], preferred_element_type=F32)
        out = (acc / l_col).T
        o_ref[rows(c), :] = (out * _silu(az_ref[rows(c), :].astype(F32))).astype(BF16)
        yield

    ppb = MOBA_BLOCK // PAGE_SIZE
    rounds = max(nb - 1, 1)
    per_round = -(-n_pages // (rounds * ppb)) * ppb

    def pinned_means(lo, hi, anchor):
        for j in range(lo // ppb, hi // ppb):
            out = means_ref.at[pl.ds(j, 1)]
            _block_means(page_refs[ppb * j:ppb * (j + 1)], out)
            _pin_before_readers(anchor, out[0])
            yield

    _emit_round_robin([pass1(0)])
    for c in range(nb):
        streams = [pass2(c)]
        if c + 1 < nb:
            streams.append(pass1(c + 1))
        lo, hi = min(c * per_round, n_pages), min((c + 1) * per_round, n_pages)
        if hi > lo:
            streams.append(pinned_means(lo, hi, qs_sc.at[(c + 1) % 2 if c + 1 < nb else c % 2]))
        _emit_round_robin(streams)


def _moba(page_table_flat, q_h, kb_h, vt_h, az_h, km, cache_k, batch, seq, pages_step, first_page):
    n_heads, t_rows, _ = q_h.shape
    blk = MOBA_BLOCK
    nb = seq // blk
    hs = MOBA_HEADS_STEP
    rows_of = lambda b, h, pt: (h, b, 0)
    step_of = lambda b, h: b * (ATT_HEADS // hs) + h
    means_spec, means_shape = _means_out(pages_step, batch * ATT_HEADS // hs, step_of)
    return pl.pallas_call(
        functools.partial(_moba_kernel, n_pages=pages_step),
        grid_spec=pltpu.PrefetchScalarGridSpec(
            num_scalar_prefetch=1,
            grid=(batch, ATT_HEADS // hs),
            in_specs=[
                pl.BlockSpec((hs, seq, ATT_HD), rows_of),
                pl.BlockSpec((hs, seq, ATT_HD), rows_of),
                pl.BlockSpec((hs, ATT_HD, seq), lambda b, h, pt: (h, 0, b)),
                pl.BlockSpec((hs, seq, ATT_HD), rows_of),
                pl.BlockSpec((hs, nb, 8, ATT_HD), lambda b, h, pt: (h, b, 0, 0)),
            ] + _page_specs(pages_step, first_page, step_of),
            out_specs=[pl.BlockSpec((hs, seq, ATT_HD), rows_of), means_spec],
            scratch_shapes=[
                pltpu.VMEM((hs, 2, seq, blk), F32),
                pltpu.VMEM((hs, 2, seq, blk), BF16),
                pltpu.VMEM((hs, 2, blk, ATT_HD), BF16),
            ],
        ),
        out_shape=[jax.ShapeDtypeStruct((n_heads, t_rows, ATT_HD), BF16), means_shape],
        compiler_params=_params("arbitrary", "arbitrary"),
        name="moba_prompt",
    )(page_table_flat, q_h, kb_h, vt_h, az_h, km, *([cache_k] * pages_step))


def _mlstm_post(h, mo, mz, nw):
    h = h * _sigmoid(mo)
    h = h * lax.rsqrt(jnp.mean(h * h, axis=-1, keepdims=True) + EPS) * nw
    return h * _silu(mz)


def _dot_rows_hi_lo(mat, row, transposed):
    rows = jnp.broadcast_to(row, (8, row.shape[1]))
    hi = rows.astype(BF16)
    lo = (rows - hi.astype(F32)).astype(BF16)
    if transposed:
        return (lax.dot_general(mat, hi, NT_DIMS, preferred_element_type=F32)
                + lax.dot_general(mat, lo, NT_DIMS, preferred_element_type=F32))
    return jnp.dot(hi, mat, preferred_element_type=F32) + jnp.dot(lo, mat, preferred_element_type=F32)


def _mlstm_kernel(pt_ref, q_ref, k_ref, v_ref, mo_ref, mz_ref, g_ref, gt_ref, bgr_ref, bgc_ref, nw_ref, *refs,
                  n_pages):
    del pt_ref
    hm_ref, c_out, n_out, m_out, means_ref, c_sc, n_sc, m_sc = refs[n_pages:]
    c = pl.program_id(1)
    chunk = q_ref.shape[0]

    @pl.when(c == 0)
    def _():
        c_sc[...] = jnp.zeros(c_sc.shape, F32)
        n_sc[...] = jnp.zeros(n_sc.shape, F32)
        m_sc[...] = jnp.zeros(m_sc.shape, F32)

    _block_means(refs[:n_pages], means_ref)

    gc = g_ref[...] + bgr_ref[...]
    lane = lax.broadcasted_iota(jnp.int32, gc.shape, 1)
    gc = jnp.where(lane >= ML_HEADS, _log_sigmoid(gc), gc)
    gr = gt_ref[0:2 * ML_HEADS, :] + bgc_ref[0:2 * ML_HEADS, :]
    sub = lax.broadcasted_iota(jnp.int32, gr.shape, 0)
    gr = jnp.where(sub >= ML_HEADS, _log_sigmoid(gr), gr)

    t_ix = lax.broadcasted_iota(jnp.int32, (chunk, chunk), 0)
    s_ix = lax.broadcasted_iota(jnp.int32, (chunk, chunk), 1)
    causal = s_ix <= t_ix
    tril = jnp.where(causal, 1.0, 0.0)
    bc = jnp.dot(tril, gc, preferred_element_type=F32, precision=HIGHEST)
    br = lax.dot_general(gr, tril, NT_DIMS, preferred_element_type=F32, precision=HIGHEST)

    for h in range(ML_HEADS):
        cols = slice(h * ML_HD, (h + 1) * ML_HD)
        q, k, v = q_ref[:, cols], k_ref[:, cols], v_ref[:, cols]
        ig_r, b_r = gr[h:h + 1, :], br[ML_HEADS + h:ML_HEADS + h + 1, :]
        ig_c, b_c = gc[:, h:h + 1], bc[:, ML_HEADS + h:ML_HEADS + h + 1]
        m_prev = m_sc[h:h + 1, 0:1]
        c_prev = c_sc[h]
        n_prev = n_sc[h:h + 1, :]

        log_w = jnp.where(causal, b_c - b_r + ig_r, NEG_INF)
        log_inter = b_c + m_prev
        m_t = jnp.maximum(log_inter, jnp.max(log_w, axis=-1, keepdims=True))
        w_intra = jnp.exp(log_w - m_t)
        w_inter = jnp.exp(log_inter - m_t)
        s = lax.dot_general(q, k, NT_DIMS, preferred_element_type=F32) * w_intra
        num = (w_inter * jnp.dot(q, c_prev.astype(BF16), preferred_element_type=F32)
               + jnp.dot(s.astype(BF16), v, preferred_element_type=F32))
        qn = jnp.sum(q.astype(F32) * n_prev, axis=-1, keepdims=True)
        den = w_inter * qn + jnp.sum(s, axis=-1, keepdims=True)
        hh = num / jnp.maximum(jnp.abs(den), jnp.exp(-m_t))

        m_new = m_t[chunk - 1:chunk, :]
        b_last = b_c[chunk - 1:chunk, :]
        a_prev = jnp.exp(b_last + m_prev - m_new)
        a_c = jnp.exp(b_last - b_c + ig_c - m_new)
        a_r = jnp.exp(b_last - b_r + ig_r - m_new)
        kf = k.astype(F32)
        ka = (kf * a_c).astype(BF16)
        c_sc[h] = a_prev * c_prev + lax.dot_general(ka, v, TN_DIMS, preferred_element_type=F32)
        a_rows = jnp.broadcast_to(a_r, (8, chunk))
        n_sc[h:h + 1, :] = a_prev * n_prev + jnp.dot(
            a_rows, kf, preferred_element_type=F32, precision=HIGHEST)[0:1, :]
        m_sc[h:h + 1, :] = jnp.broadcast_to(m_new, (1, m_sc.shape[1]))

        hm_ref[:, cols] = _mlstm_post(
            hh, mo_ref[:, cols].astype(F32), mz_ref[:, cols].astype(F32), nw_ref[...]).astype(BF16)

    @pl.when(c == pl.num_programs(1) - 1)
    def _():
        c_out[0] = c_sc[...]
        n_out[0] = n_sc[...]
        m_out[0] = m_sc[...]


def _mlstm(page_table_flat, r, g, gt, bg_row, bg_col, mh_norm_w, cache_k, batch, seq, chunk, pages_step, first_page):
    t_rows = r.shape[0]
    d = ML_HEADS * ML_HD
    nc = seq // chunk
    sec = lambda j: (lambda b, c, pt: (b * nc + c, j))
    fixed = lambda b, c, pt: (0, 0)
    state = lambda b, c, pt: (b, 0, 0)
    step_of = lambda b, c: b * nc + c
    means_spec, means_shape = _means_out(pages_step, batch * nc, step_of)
    return pl.pallas_call(
        functools.partial(_mlstm_kernel, n_pages=pages_step),
        grid_spec=pltpu.PrefetchScalarGridSpec(
            num_scalar_prefetch=1,
            grid=(batch, nc),
            in_specs=[
                pl.BlockSpec((chunk, d), sec(SEC_MQ)),
                pl.BlockSpec((chunk, d), sec(SEC_MK)),
                pl.BlockSpec((chunk, d), sec(SEC_MV)),
                pl.BlockSpec((chunk, d), sec(SEC_MO)),
                pl.BlockSpec((chunk, d), sec(SEC_MZ)),
                pl.BlockSpec((chunk, GATE_LANES), lambda b, c, pt: (b * nc + c, 0)),
                pl.BlockSpec((GATE_LANES, chunk), lambda b, c, pt: (0, b * nc + c)),
                pl.BlockSpec((1, GATE_LANES), fixed),
                pl.BlockSpec((GATE_LANES, 1), fixed),
                pl.BlockSpec((1, ML_HD), fixed),
            ] + _page_specs(pages_step, first_page, step_of),
            out_specs=[
                pl.BlockSpec((chunk, d), lambda b, c, pt: (b * nc + c, 0)),
                pl.BlockSpec((1, ML_HEADS, ML_HD, ML_HD), lambda b, c, pt: (b, 0, 0, 0)),
                pl.BlockSpec((1, ML_HEADS, ML_HD), state),
                pl.BlockSpec((1, ML_HEADS, GATE_LANES), state),
                means_spec,
            ],
            scratch_shapes=[
                pltpu.VMEM((ML_HEADS, ML_HD, ML_HD), F32),
                pltpu.VMEM((ML_HEADS, ML_HD), F32),
                pltpu.VMEM((ML_HEADS, GATE_LANES), F32),
            ],
        ),
        out_shape=[
            jax.ShapeDtypeStruct((t_rows, d), BF16),
            jax.ShapeDtypeStruct((batch, ML_HEADS, ML_HD, ML_HD), F32),
            jax.ShapeDtypeStruct((batch, ML_HEADS, ML_HD), F32),
            jax.ShapeDtypeStruct((batch, ML_HEADS, GATE_LANES), F32),
            means_shape,
        ],
        compiler_params=_params("arbitrary", "arbitrary"),
        name="mlstm_prompt",
    )(page_table_flat, r, r, r, r, r, g, gt, bg_row, bg_col, mh_norm_w, *([cache_k] * pages_step))


def _sstep_kernel(q_ref, k_ref, v_ref, mo_ref, mz_ref, g_ref, bgr_ref, nw_ref, c_ref, n_ref, m_ref,
                  hm_ref, c_out, n_out, m_out):
    for b in range(q_ref.shape[0]):
        gates = g_ref[b] + bgr_ref[...]
        for h in range(ML_HEADS):
            cols = slice(h * ML_HD, (h + 1) * ML_HD)
            q = q_ref[b, :, cols].astype(F32)
            k = k_ref[b, :, cols].astype(F32)
            v = v_ref[b, :, cols].astype(F32)
            ig = gates[:, h:h + 1]
            lf = _log_sigmoid(gates[:, ML_HEADS + h:ML_HEADS + h + 1])
            m_prev = m_ref[b, :, h:h + 1]
            c_prev = c_ref[b, h]
            n_prev = n_ref[b, h:h + 1, :]

            log_inter = lf + m_prev
            m_t = jnp.maximum(log_inter, ig)
            w_intra = jnp.exp(ig - m_t)
            w_inter = jnp.exp(log_inter - m_t)
            s = jnp.sum(q * k, axis=-1, keepdims=True) * w_intra
            qc = _dot_rows_hi_lo(c_prev.astype(BF16), q, transposed=False)[0:1, :]
            num = w_inter * qc + s * v
            den = w_inter * jnp.sum(q * n_prev, axis=-1, keepdims=True) + s
            hh = num / jnp.maximum(jnp.abs(den), jnp.exp(-m_t))
            hm_ref[b, :, cols] = _mlstm_post(
                hh, mo_ref[b, :, cols].astype(F32), mz_ref[b, :, cols].astype(F32), nw_ref[...]).astype(BF16)

            sub = lax.broadcasted_iota(jnp.int32, (16, ML_HD), 0)
            k16 = jnp.where(sub == 0, jnp.broadcast_to(k, (16, ML_HD)), 0.0).astype(BF16)
            v16 = jnp.broadcast_to(v, (16, ML_HD)).astype(BF16)
            kv = lax.dot_general(k16, v16, TN_DIMS, preferred_element_type=F32)
            c_out[b, h] = w_inter * c_prev + w_intra * kv
            n_out[b, h:h + 1, :] = w_inter * n_prev + w_intra * k
            m_out[b, :, h:h + 1] = m_t


def _sstep(r3, g3, bg_row, mh_norm_w, c_state, n_state, m_state3, rows_step):
    nb = r3.shape[0]
    d = ML_HEADS * ML_HD
    sec = lambda j: (lambda b: (b, 0, j))
    fixed = lambda b: (0, 0)
    row3 = lambda b: (b, 0, 0)
    state4 = lambda b: (b, 0, 0, 0)
    return pl.pallas_call(
        _sstep_kernel,
        grid=(nb // rows_step,),
        in_specs=[
            pl.BlockSpec((rows_step, 1, d), sec(SEC_MQ)),
            pl.BlockSpec((rows_step, 1, d), sec(SEC_MK)),
            pl.BlockSpec((rows_step, 1, d), sec(SEC_MV)),
            pl.BlockSpec((rows_step, 1, d), sec(SEC_MO)),
            pl.BlockSpec((rows_step, 1, d), sec(SEC_MZ)),
            pl.BlockSpec((rows_step, 1, GATE_LANES), row3),
            pl.BlockSpec((1, GATE_LANES), fixed),
            pl.BlockSpec((1, ML_HD), fixed),
            pl.BlockSpec((rows_step, ML_HEADS, ML_HD, ML_HD), state4),
            pl.BlockSpec((rows_step, ML_HEADS, ML_HD), row3),
            pl.BlockSpec((rows_step, 1, ML_HEADS), row3),
        ],
        out_specs=[
            pl.BlockSpec((rows_step, 1, d), row3),
            pl.BlockSpec((rows_step, ML_HEADS, ML_HD, ML_HD), state4),
            pl.BlockSpec((rows_step, ML_HEADS, ML_HD), row3),
            pl.BlockSpec((rows_step, 1, ML_HEADS), row3),
        ],
        out_shape=[
            jax.ShapeDtypeStruct((nb, 1, d), BF16),
            jax.ShapeDtypeStruct(c_state.shape, F32),
            jax.ShapeDtypeStruct(n_state.shape, F32),
            jax.ShapeDtypeStruct(m_state3.shape, F32),
        ],
        compiler_params=_params("arbitrary"),
        name="mlstm_sample",
    )(r3, r3, r3, r3, r3, g3, bg_row, mh_norm_w, c_state, n_state, m_state3)


def _sgate_kernel(q_ref, kmean_ref, sel_ref):
    n_blocks = kmean_ref.shape[1]
    for b in range(q_ref.shape[0]):
        gate = jnp.sum(kmean_ref[b] * q_ref[b], axis=-1, keepdims=True)
        row = lax.broadcasted_iota(jnp.int32, gate.shape, 0).astype(F32)
        lane = lax.broadcasted_iota(jnp.int32, (ATT_HEADS, GATE_LANES), 1)
        picked = jnp.zeros((ATT_HEADS, GATE_LANES), F32)
        for j in range(MOBA_TOPK):
            best = jnp.max(gate, axis=0, keepdims=True)
            idx = jnp.min(jnp.where(gate == best, row, float(n_blocks)), axis=0, keepdims=True)
            picked = jnp.where(lane == j, idx[0], picked)
            gate = jnp.where(row == idx, NEG_INF, gate)
        sel_ref[b] = picked.astype(jnp.int32)


def _sgate(q_heads, kmean):
    nb, n_blocks = kmean.shape[:2]
    rows = _row_tile(nb, 8)
    return pl.pallas_call(
        _sgate_kernel,
        grid=(nb // rows,),
        in_specs=[pl.BlockSpec((rows, ATT_HEADS, ATT_HD), lambda b: (b, 0, 0)),
                  pl.BlockSpec((rows, n_blocks, ATT_HEADS, ATT_HD), lambda b: (b, 0, 0, 0))],
        out_specs=pl.BlockSpec((rows, ATT_HEADS, GATE_LANES), lambda b: (b, 0, 0)),
        out_shape=jax.ShapeDtypeStruct((nb, ATT_HEADS, GATE_LANES), jnp.int32),
        compiler_params=_params("arbitrary"),
        name="moba_sample_gate",
    )(q_heads, kmean)


def _sattn_copies(pt_ref, sel_ref, ck_ref, cv_ref, kbuf, vbuf, sems, b, slot):
    ppb = MOBA_BLOCK // PAGE_SIZE
    copies = []
    for h in range(ATT_HEADS):
        for j in range(MOBA_TOPK):
            block = sel_ref[b, h * MOBA_TOPK + j]
            for p in range(ppb):
                page = pt_ref[b, block * ppb + p]
                rows = pl.ds((j * ppb + p) * PAGE_SIZE, PAGE_SIZE)
                copies.append(pltpu.make_async_copy(ck_ref.at[page, :, h, :], kbuf.at[slot, h, rows, :], sems.at[0, slot]))
                copies.append(pltpu.make_async_copy(cv_ref.at[page, :, h, :], vbuf.at[slot, h, rows, :], sems.at[1, slot]))
    return copies


def _sattn_row(q, kn, vn, az, k_sel, v_sel):
    qs = q * (ATT_HD ** -0.5)
    s_new = jnp.sum(qs * kn, axis=-1, keepdims=True)
    sub = lax.broadcasted_iota(jnp.int32, (ATT_HEADS, ATT_HD), 0)
    out = jnp.zeros((ATT_HEADS, ATT_HD), F32)
    for h in range(ATT_HEADS):
        s = jnp.sum(k_sel[h] * qs[h:h + 1, :], axis=-1, keepdims=True)
        sn = s_new[h:h + 1, :]
        m = jnp.maximum(sn, jnp.max(s, axis=0, keepdims=True))
        p = jnp.exp(s - m)
        pn = jnp.exp(sn - m)
        l = pn + jnp.sum(p, axis=0, keepdims=True)
        acc = pn * vn[h:h + 1, :] + jnp.sum(p * v_sel[h], axis=0, keepdims=True)
        out = jnp.where(sub == h, acc / l, out)
    return out * _silu(az.astype(F32))


def _merge_kernel(*refs, n_side_rows, rows_step):
    if n_side_rows:
        pt_ref, sel_ref, refs = refs[0], refs[1], refs[2:]
    x_ref, am_ref, hm_ref, ga_ref, gm_ref, wa_ref, wm_ref, wo_ref = refs[:8]
    if n_side_rows:
        q_ref, kn_ref, vn_ref, az_ref, ck_ref, cv_ref, y_ref, o_ref, kbuf, vbuf, sems, mix_sc = refs[8:]
        side_tiles = []
        fetch = functools.partial(_sattn_copies, pt_ref, sel_ref, ck_ref, cv_ref, kbuf, vbuf, sems)
        step = pl.program_id(0)
        last_block = n_side_rows // rows_step - 1

        def row_of(s, j):
            return jnp.minimum(s, last_block) * rows_step + j

        def after(s, j):
            return (s, j + 1) if j + 1 < rows_step else (s + 1, 0)

        @pl.when(step == 0)
        def _():
            for cp in fetch(row_of(0, 0), 0):
                cp.start()

        for j in range(rows_step):
            cnt = step * rows_step + j
            for cp in fetch(row_of(*after(step, j)), (cnt + 1) % 2):
                cp.start()
            for cp in fetch(row_of(step, j), cnt % 2):
                cp.wait()
            side_out = _sattn_row(q_ref[j], kn_ref[j], vn_ref[j], az_ref[j], kbuf.at[cnt % 2], vbuf.at[cnt % 2])
            o_ref[j] = side_out.astype(BF16)
            side_tiles.append(side_out)
    else:
        y_ref = refs[8]

    am = jnp.concatenate([am_ref[h] for h in range(ATT_HEADS)], axis=1)
    y_att = jnp.dot(am, wa_ref[...], preferred_element_type=F32)
    y_ml = jnp.dot(hm_ref[...], wm_ref[...], preferred_element_type=F32)
    mix = _sigmoid(ga_ref[...].astype(F32)) * y_att + _sigmoid(gm_ref[...].astype(F32)) * y_ml
    if n_side_rows:
        mix_sc[...] = mix.astype(BF16)
        for tile in side_tiles:
            _pin_before_readers(mix_sc, tile)
        y_ref[...] = x_ref[...] + jnp.dot(mix_sc[...], wo_ref[...], preferred_element_type=F32)

        @pl.when(step == pl.num_programs(0) - 1)
        def _():
            for cp in fetch(row_of(step + 1, 0), ((step + 1) * rows_step) % 2):
                cp.wait()
    else:
        y_ref[...] = x_ref[...] + jnp.dot(mix.astype(BF16), wo_ref[...], preferred_element_type=F32)


def _merge(x, am, hm, r, wa, wm, wo, tm, side=None):
    t_rows, d = x.shape
    n_steps = t_rows // tm
    row = lambda i, *_: (i, 0)
    fixed = lambda i, *_: (0, 0)
    in_specs = [
        pl.BlockSpec((tm, d), row),
        pl.BlockSpec((ATT_HEADS, tm, ATT_HD), lambda i, *_: (0, i, 0)),
        pl.BlockSpec((tm, d), row),
        pl.BlockSpec((tm, d), lambda i, *_: (i, SEC_GA)),
        pl.BlockSpec((tm, d), lambda i, *_: (i, SEC_GM)),
        pl.BlockSpec((d, d), fixed),
        pl.BlockSpec((d, d), fixed),
        pl.BlockSpec((d, d), fixed),
    ]
    y_spec = pl.BlockSpec((tm, d), row)
    y_shape = jax.ShapeDtypeStruct((t_rows, d), F32)
    if side is None:
        return pl.pallas_call(
            functools.partial(_merge_kernel, n_side_rows=0, rows_step=0),
            grid=(n_steps,), in_specs=in_specs, out_specs=y_spec, out_shape=y_shape,
            compiler_params=_params("arbitrary"), name="merge",
        )(x, am, hm, r, r, wa, wm, wo)

    page_table, sel, q_heads, k_heads, v_heads, az_heads, cache_k, cache_v = side
    n_rows = q_heads.shape[0]
    rows_step = next(c for c in range(1, n_rows + 1) if n_rows % c == 0 and n_rows // c <= n_steps)
    n_blocks = n_rows // rows_step
    n_keys = MOBA_TOPK * MOBA_BLOCK
    head_spec = pl.BlockSpec((rows_step, ATT_HEADS, ATT_HD), lambda i, *_: (jnp.minimum(i, n_blocks - 1), 0, 0))
    return pl.pallas_call(
        functools.partial(_merge_kernel, n_side_rows=n_rows, rows_step=rows_step),
        grid_spec=pltpu.PrefetchScalarGridSpec(
            num_scalar_prefetch=2,
            grid=(n_steps,),
            in_specs=in_specs + [head_spec, head_spec, head_spec, head_spec,
                                 pl.BlockSpec(memory_space=pl.ANY), pl.BlockSpec(memory_space=pl.ANY)],
            out_specs=[y_spec, head_spec],
            scratch_shapes=[
                pltpu.VMEM((2, ATT_HEADS, n_keys, ATT_HD), F32),
                pltpu.VMEM((2, ATT_HEADS, n_keys, ATT_HD), F32),
                pltpu.SemaphoreType.DMA((2, 2)),
                pltpu.VMEM((tm, d), BF16),
            ],
        ),
        out_shape=[y_shape, jax.ShapeDtypeStruct((n_rows, ATT_HEADS, ATT_HD), BF16)],
        compiler_params=_params("arbitrary"),
        name="merge_with_sample_attend",
    )(page_table, sel, x, am, hm, r, r, wa, wm, wo, q_heads, k_heads, v_heads, az_heads, cache_k, cache_v)


def _row_tile(rows, target):
    tm = min(rows, target)
    while rows % tm:
        tm //= 2
    return tm


def kernel(x_prompt, x_sample, cache_k, cache_v, page_table, state_mlstm_C, state_mlstm_n, state_mlstm_m,
           norm_w, w_in, b_gates, q_norm_w, k_norm_w, mh_norm_w, w_proj_attn, w_proj_mlstm, w_out):
    batch, seq, d = x_prompt.shape
    dec_batch, dec_seq, _ = x_sample.shape
    assert dec_seq == 1 and d == ATT_HEADS * ATT_HD == ML_HEADS * ML_HD
    depth = w_in.shape[0]
    n_pages = page_table.shape[1]
    past = n_pages * PAGE_SIZE
    assert past % MOBA_BLOCK == 0 and seq % MOBA_BLOCK == 0
    t_p, t_s = batch * seq, dec_batch * dec_seq

    tm_proj = MOBA_BLOCK
    assert seq % tm_proj == 0
    chunk = _row_tile(seq, 256)
    rope_p = _rope_tables(jnp.arange(seq, dtype=jnp.int32))
    rope_s = _rope_tables(jnp.tile(past + jnp.arange(dec_seq, dtype=jnp.int32), dec_batch))

    ppb = MOBA_BLOCK // PAGE_SIZE
    steps = (t_p // tm_proj, batch * ATT_HEADS // MOBA_HEADS_STEP, batch * (seq // chunk))
    weights = (3, 4, 4)
    total_pages = dec_batch * n_pages
    unit = -(-total_pages // (sum(s * w for s, w in zip(steps, weights)) * ppb)) * ppb
    share = tuple(unit * w for w in weights)
    first = (0, steps[0] * share[0], steps[0] * share[0] + steps[1] * share[1])
    capacity = first[2] + steps[2] * share[2]
    pt_flat = jnp.pad(page_table.reshape(-1), (0, capacity - total_pages))

    y_p = x_prompt.reshape(t_p, d)
    y_s = x_sample.reshape(t_s, d)
    outs = [[] for _ in range(10)]
    for l in range(depth):
        w_wide = _cast_wide_weight(w_in, l, d)
        w_gate2 = _split_gate_weight(w_in[l][:, N_WIDE * d:])
        bg_row = jnp.pad(b_gates[l].astype(F32), (0, GATE_LANES - 2 * ML_HEADS)).reshape(1, GATE_LANES)
        bg_col = bg_row.reshape(GATE_LANES, 1)
        nw = norm_w[l].reshape(1, d)
        qnw, knw = q_norm_w[l].reshape(1, ATT_HD), k_norm_w[l].reshape(1, ATT_HD)
        mhw = mh_norm_w[l].reshape(1, ML_HD)
        wa, wm, wo = w_proj_attn[l].astype(BF16), w_proj_mlstm[l].astype(BF16), w_out[l].astype(BF16)

        q_h, k, kb_h, km, v, vt_h, az_h, r, g, gt, means0 = _inproj(
            pt_flat, y_p, nw, w_wide, w_gate2, rope_p, qnw, knw, cache_k[l], tm_proj, seq // tm_proj,
            share[0], first[0])
        am_h, means1 = _moba(pt_flat, q_h, kb_h, vt_h, az_h, km, cache_k[l], batch, seq, share[1], first[1])
        hm, c_new, n_new, m_new, means2 = _mlstm(pt_flat, r, g, gt, bg_row, bg_col, mhw, cache_k[l], batch, seq, chunk,
                                                 share[2], first[2])
        for dst, val in zip(outs[:5], (k.reshape(batch, seq, ATT_HEADS, ATT_HD), v.reshape(batch, seq, ATT_HEADS, ATT_HD),
                                       c_new, n_new, m_new[:, :, 0])):
            dst.append(val)
        kmean = jnp.concatenate([means0, means1, means2], axis=0)[:total_pages // ppb]
        kmean = kmean.reshape(dec_batch, n_pages // ppb, ATT_HEADS, ATT_HD)
        r_p = r

        q_h, k, _, _, v, _, az_h, r, g, _ = _inproj(
            pt_flat, y_s, nw, w_wide, w_gate2, rope_s, qnw, knw, cache_k[l], t_s, 1, 0, 0, k_split=SAMPLE_K_SPLIT)
        r3 = r.reshape(t_s, 1, r.shape[-1])
        by_head = lambda t: t.reshape(t_s, ATT_HEADS, ATT_HD)
        rows_major = lambda t: jnp.swapaxes(t, 0, 1)
        q_rows = rows_major(q_h)
        sel = _sgate(q_rows, kmean)
        sel = sel[:, :, :MOBA_TOPK].reshape(t_s, ATT_HEADS * MOBA_TOPK)
        y_p, am_s = _merge(y_p, am_h, hm, r_p, wa, wm, wo, _row_tile(t_p, 512),
                           side=(page_table, sel, q_rows, by_head(k), by_head(v), rows_major(az_h),
                                 cache_k[l], cache_v[l]))
        hm, c_new, n_new, m_new = _sstep(
            r3, g.reshape(t_s, 1, GATE_LANES), bg_row, mhw,
            state_mlstm_C[l], state_mlstm_n[l], state_mlstm_m[l].reshape(dec_batch, 1, ML_HEADS), _row_tile(t_s, 4))
        y_s = _merge(y_s, rows_major(am_s), hm.reshape(t_s, d), r, wa, wm, wo, t_s)
        for dst, val in zip(outs[5:], (k.reshape(dec_batch, dec_seq, ATT_HEADS, ATT_HD),
                                       v.reshape(dec_batch, dec_seq, ATT_HEADS, ATT_HD),
                                       c_new, n_new, m_new.reshape(dec_batch, ML_HEADS))):
            dst.append(val)

    st = state_mlstm_C.dtype
    k_p, v_p, c_p, n_p, m_p, k_s, v_s, c_s, n_s, m_s = (jnp.stack(o) for o in outs)
    return (y_p.reshape(batch, seq, d), y_s.reshape(dec_batch, dec_seq, d),
            k_p, v_p, c_p.astype(st), n_p.astype(st), m_p.astype(st),
            k_s, v_s, c_s.astype(st), n_s.astype(st), m_s.astype(st))
```

```python
import functools

import jax
import jax.numpy as jnp
from jax import lax
from jax.experimental import pallas as pl
from jax.experimental.pallas import tpu as pltpu

F32 = jnp.float32
BF16 = jnp.bfloat16
HIGHEST = lax.Precision.HIGHEST
NEG_INF = float("-inf")

ATT_HEADS = 8
ATT_HD = 128
ROT_DIM = ATT_HD // 4
ROPE_THETA = 500000.0
MOBA_BLOCK = 256
MOBA_TOPK = 3
ML_HEADS = 4
ML_HD = 256
PAGE_SIZE = 128
EPS = 1e-6
N_WIDE = 11
N_ROW_SECTIONS = 7
SEC_MQ, SEC_MK, SEC_MV, SEC_MO, SEC_MZ, SEC_GA, SEC_GM = range(N_ROW_SECTIONS)
GATE_LANES = 128
N_INPROJ_OUTS = 10
LOG2_E = 1.4426950408889634
KEY_TILE = 256
MOBA_HEADS_STEP = 2
WEIGHT_K_SPLIT = 8
WEIGHT_K_SPLIT_PROMPT = 2

VMEM_LIMIT_BYTES = 60000 * 1024

NT_DIMS = (((1,), (1,)), ((), ()))
TN_DIMS = (((0,), (0,)), ((), ()))


def _sigmoid(x):
    return 1.0 / (1.0 + jnp.exp(-x))


def _silu(x):
    return x * _sigmoid(x)


def _log_sigmoid(x):
    return jnp.minimum(x, 0.0) - jnp.log1p(jnp.exp(-jnp.abs(x)))


def _params(*semantics):
    return pltpu.CompilerParams(dimension_semantics=semantics, vmem_limit_bytes=VMEM_LIMIT_BYTES)


def _head_norm_rope(t, w, rc, rs1, rs2):
    half = ROT_DIM // 2
    outs = []
    for h in range(ATT_HEADS):
        th = t[:, h * ATT_HD:(h + 1) * ATT_HD]
        y = th * lax.rsqrt(jnp.mean(th * th, axis=-1, keepdims=True) + EPS) * w
        up = pltpu.roll(y, ATT_HD - half, axis=1)
        down = pltpu.roll(y, half, axis=1)
        outs.append(y * rc + up * rs1 + down * rs2)
    return outs


def _cast_kernel(x_ref, o_ref):
    o_ref[...] = x_ref[...].T.astype(o_ref.dtype)


def _cast_wide_weight(w_in, layer, d):
    return pl.pallas_call(
        _cast_kernel,
        grid=(N_WIDE,),
        in_specs=[pl.BlockSpec((None, d, d), lambda j: (layer, j, 0))],
        out_specs=pl.BlockSpec((None, d, d), lambda j: (j, 0, 0)),
        out_shape=jax.ShapeDtypeStruct((N_WIDE, d, d), BF16),
        compiler_params=_params("arbitrary"),
        name="cast_weight",
    )(jnp.swapaxes(w_in, 1, 2))


def _page_specs(n_step, first_page, step_of):
    def spec(p):
        return pl.BlockSpec((1, PAGE_SIZE, ATT_HEADS, ATT_HD),
                            lambda *ids: (ids[-1][first_page + step_of(*ids[:-1]) * n_step + p], 0, 0, 0))
    return [spec(p) for p in range(n_step)]


def _block_means_steps(page_refs, out_ref):
    ppb = MOBA_BLOCK // PAGE_SIZE
    for j in range(len(page_refs) // ppb):
        tot = jnp.sum(page_refs[ppb * j][0], axis=0)
        for p in range(1, ppb):
            tot = tot + jnp.sum(page_refs[ppb * j + p][0], axis=0)
        out_ref[j] = tot * (1.0 / MOBA_BLOCK)
        yield


def _block_means(page_refs, out_ref):
    for _ in _block_means_steps(page_refs, out_ref):
        pass


def _emit_round_robin(streams):
    streams = list(streams)
    while streams:
        for g in list(streams):
            if next(g, "done") == "done":
                streams.remove(g)


def _means_out(n_step, n_steps, step_of):
    ppb = MOBA_BLOCK // PAGE_SIZE
    spec = pl.BlockSpec((n_step // ppb, ATT_HEADS, ATT_HD), lambda *ids: (step_of(*ids[:-1]), 0, 0))
    return spec, jax.ShapeDtypeStruct((n_steps * n_step // ppb, ATT_HEADS, ATT_HD), F32)


def _pin_before_readers(anchor_ref, value):
    zero = pltpu.bitcast(lax.shift_right_logical(pltpu.bitcast(value, jnp.uint32), jnp.uint32(32)), F32)
    zero = jnp.concatenate([zero, zero], axis=0).astype(anchor_ref.dtype)
    anchor_ref[0:16, 0:GATE_LANES] = anchor_ref[0:16, 0:GATE_LANES] + zero


def _inproj_kernel(pt_ref, x_ref, nw_ref, wg_ref, rc_ref, rs1_ref, rs2_ref, qnw_ref, knw_ref, *refs, n_pages,
                   k_split):
    del pt_ref
    w_refs, refs = refs[:N_WIDE * k_split], refs[N_WIDE * k_split:]
    q_ref, k_ref, kb_ref, km_ref, v_ref, vt_ref, az_ref, r_ref, g_ref, gt_ref = refs[n_pages:n_pages + N_INPROJ_OUTS]
    xb_sc = refs[-1]
    d = x_ref.shape[1]
    x = x_ref[...]
    inv_rms = lax.rsqrt(jnp.mean(x * x, axis=-1, keepdims=True) + EPS)
    xw = x * nw_ref[...]
    xb = xw.astype(BF16)
    xb_sc[...] = xb

    n_gate = 2 * ML_HEADS
    x_lo = (xw - xb.astype(F32)).astype(BF16)
    both = (jnp.dot(xb_sc[...], wg_ref[...], preferred_element_type=F32)
            + jnp.dot(x_lo, wg_ref[...], preferred_element_type=F32)) * inv_rms
    lane = lax.broadcasted_iota(jnp.int32, both.shape, 1)
    g = jnp.where(lane < n_gate, both + pltpu.roll(both, GATE_LANES - n_gate, axis=1), 0.0)
    g_ref[...] = g
    gt_ref[...] = g.T

    ppb = MOBA_BLOCK // PAGE_SIZE
    page_blocks = [(refs[ppb * j:ppb * (j + 1)], refs[n_pages + N_INPROJ_OUTS].at[pl.ds(j, 1)])
                   for j in range(n_pages // ppb)]

    def section(i):
        share = -(-len(page_blocks) // N_WIDE)
        for pages, out in page_blocks[i * share:(i + 1) * share]:
            _block_means(pages, out)
            _pin_before_readers(xb_sc, out[0])
        blocks = [w_refs[i * k_split + kk][...] for kk in range(k_split)]
        w = blocks[0] if k_split == 1 else jnp.concatenate(blocks, axis=0)
        return jnp.dot(xb_sc[...], w, preferred_element_type=F32) * inv_rms

    rc, rs1, rs2 = rc_ref[...], rs1_ref[...], rs2_ref[...]
    head = lambda h: slice(h * ATT_HD, (h + 1) * ATT_HD)
    for h, t in enumerate(_head_norm_rope(section(0), qnw_ref[...], rc, rs1, rs2)):
        q_ref[h] = t
    for h, t in enumerate(_head_norm_rope(section(1), knw_ref[...], rc, rs1, rs2)):
        k_ref[:, head(h)] = t
        kb_ref[h] = t.astype(BF16)
        km_ref[h, 0] = jnp.broadcast_to(jnp.sum(t, axis=0, keepdims=True) * (1.0 / t.shape[0]), (8, ATT_HD))
    v = section(2)
    v_ref[...] = v
    az = section(3)
    for h in range(ATT_HEADS):
        vt_ref[h] = v[:, head(h)].T.astype(BF16)
        az_ref[h] = az[:, head(h)].astype(BF16)
    for j in range(N_ROW_SECTIONS):
        t = section(4 + j)
        if j == SEC_MK:
            t = t * (ML_HD ** -0.5)
        r_ref[:, j * d:(j + 1) * d] = t.astype(BF16)


def _split_gate_weight(w_gate):
    hi = w_gate.astype(BF16)
    lo = (w_gate - hi.astype(F32)).astype(BF16)
    return jnp.pad(jnp.concatenate([hi, lo], axis=1), ((0, 0), (0, GATE_LANES - 2 * w_gate.shape[1])))


def _inproj(page_table_flat, x, norm_w, w_wide, w_gate2, rope, q_norm_w, k_norm_w, cache_k, tm, n_pos_blocks,
            pages_step, first_page, k_split=1):
    t_rows, d = x.shape
    rc, rs1, rs2 = rope
    n_steps = t_rows // tm
    row = lambda i, pt: (i, 0)
    fixed = lambda i, pt: (0, 0)
    pos = lambda i, pt: (i % n_pos_blocks, 0)
    once = pl.Buffered(1)
    step_of = lambda i: i
    w_blocks = w_wide.reshape(N_WIDE * k_split, d // k_split, d)
    wide_section = lambda j: pl.BlockSpec((None, d // k_split, d), lambda i, pt: (j, 0, 0), pipeline_mode=once)
    by_head = pl.BlockSpec((ATT_HEADS, tm, ATT_HD), lambda i, pt: (0, i, 0))
    out_specs = [
        by_head,
        pl.BlockSpec((tm, d), row),
        by_head,
        pl.BlockSpec((ATT_HEADS, 1, 8, ATT_HD), lambda i, pt: (0, i, 0, 0)),
        pl.BlockSpec((tm, d), row),
        pl.BlockSpec((ATT_HEADS, ATT_HD, tm), lambda i, pt: (0, 0, i)),
        by_head,
        pl.BlockSpec((tm, N_ROW_SECTIONS * d), row),
        pl.BlockSpec((tm, GATE_LANES), row),
        pl.BlockSpec((GATE_LANES, tm), lambda i, pt: (0, i)),
    ]
    out_shape = [
        jax.ShapeDtypeStruct((ATT_HEADS, t_rows, ATT_HD), F32),
        jax.ShapeDtypeStruct((t_rows, d), F32),
        jax.ShapeDtypeStruct((ATT_HEADS, t_rows, ATT_HD), BF16),
        jax.ShapeDtypeStruct((ATT_HEADS, n_steps, 8, ATT_HD), F32),
        jax.ShapeDtypeStruct((t_rows, d), F32),
        jax.ShapeDtypeStruct((ATT_HEADS, ATT_HD, t_rows), BF16),
        jax.ShapeDtypeStruct((ATT_HEADS, t_rows, ATT_HD), BF16),
        jax.ShapeDtypeStruct((t_rows, N_ROW_SECTIONS * d), BF16),
        jax.ShapeDtypeStruct((t_rows, GATE_LANES), F32),
        jax.ShapeDtypeStruct((GATE_LANES, t_rows), F32),
    ]
    assert len(out_specs) == N_INPROJ_OUTS
    if pages_step:
        spec, shape = _means_out(pages_step, n_steps, step_of)
        out_specs.append(spec)
        out_shape.append(shape)
    return pl.pallas_call(
        functools.partial(_inproj_kernel, n_pages=pages_step, k_split=k_split),
        grid_spec=pltpu.PrefetchScalarGridSpec(
            num_scalar_prefetch=1,
            grid=(n_steps,),
            in_specs=[
                pl.BlockSpec((tm, d), row),
                pl.BlockSpec((1, d), fixed),
                pl.BlockSpec((d, GATE_LANES), fixed, pipeline_mode=once),
                pl.BlockSpec((tm, ATT_HD), pos),
                pl.BlockSpec((tm, ATT_HD), pos),
                pl.BlockSpec((tm, ATT_HD), pos),
                pl.BlockSpec((1, ATT_HD), fixed),
                pl.BlockSpec((1, ATT_HD), fixed),
            ] + [wide_section(j) for j in range(N_WIDE * k_split)] + _page_specs(pages_step, first_page, step_of),
            out_specs=out_specs,
            scratch_shapes=[pltpu.VMEM((tm, d), BF16)],
        ),
        out_shape=out_shape,
        compiler_params=_params("arbitrary"),
        name="inproj",
    )(page_table_flat, x, norm_w, w_gate2, rc, rs1, rs2, q_norm_w, k_norm_w, *([w_blocks] * (N_WIDE * k_split)),
      *([cache_k] * pages_step))


def _rope_tables(pos):
    half = ROT_DIM // 2
    inv = ROPE_THETA ** (-(jnp.arange(half, dtype=F32) * 2.0) / ROT_DIM)
    ang = pos.astype(F32)[:, None] * inv[None, :]
    cos, sin = jnp.cos(ang), jnp.sin(ang)
    n = pos.shape[0]
    zeros = jnp.zeros((n, half), F32)
    tail0 = jnp.zeros((n, ATT_HD - ROT_DIM), F32)
    rc = jnp.concatenate([cos, cos, jnp.ones((n, ATT_HD - ROT_DIM), F32)], axis=-1)
    rs1 = jnp.concatenate([-sin, zeros, tail0], axis=-1)
    rs2 = jnp.concatenate([zeros, sin, tail0], axis=-1)
    return rc, rs1, rs2


def _moba_kernel(pt_ref, q_ref, kb_ref, vt_ref, az_ref, km_ref, *refs, n_pages):
    del pt_ref
    o_all, means_ref, s_all, p_all, qs_all = refs[n_pages:]
    page_refs = refs[:n_pages]
    heads_step = q_ref.shape[0]
    pages_head = n_pages // heads_step
    for hh in range(heads_step):
        _moba_head(q_ref.at[hh], kb_ref.at[hh], vt_ref.at[hh], az_ref.at[hh], km_ref.at[hh], o_all.at[hh],
                   s_all.at[hh], p_all.at[hh], qs_all.at[hh],
                   page_refs[hh * pages_head:(hh + 1) * pages_head],
                   means_ref.at[pl.ds(hh * pages_head // (MOBA_BLOCK // PAGE_SIZE), pages_head // (MOBA_BLOCK // PAGE_SIZE))])


def _moba_head(q_ref, kb_ref, vt_ref, az_ref, km_ref, o_ref, s_sc, p_sc, qs_sc, page_refs, means_ref):
    n_pages = len(page_refs)
    blk = MOBA_BLOCK
    nb = q_ref.shape[0] // blk
    nb_pad = -(-nb // 8) * 8
    rows = lambda j: slice(j * blk, (j + 1) * blk)

    sub8 = lax.broadcasted_iota(jnp.int32, (8, ATT_HD), 0)
    groups = []
    for g0 in range(0, nb, 8):
        tile = jnp.zeros((8, ATT_HD), F32)
        for j in range(g0, min(g0 + 8, nb)):
            tile = jnp.where(sub8 == j - g0, km_ref[j], tile)
        groups.append(tile)
    kmean = groups[0] if len(groups) == 1 else jnp.concatenate(groups, axis=0)

    blk_id = lax.broadcasted_iota(jnp.int32, (nb_pad, blk), 0)
    key_ix = lax.broadcasted_iota(jnp.int32, (blk, blk), 0)
    qry_ix = lax.broadcasted_iota(jnp.int32, (blk, blk), 1)
    causal = key_ix <= qry_ix

    tiles_per_blk = blk // KEY_TILE
    fold = lambda t: t.reshape(KEY_TILE // 8, 8, blk)
    state = {}

    def pass1(c):
        q = q_ref[rows(c), :]
        gate = lax.dot_general(kmean, q, NT_DIMS, preferred_element_type=F32, precision=HIGHEST)
        past = blk_id < c
        gate = jnp.where(past, gate, NEG_INF)
        beaten = jnp.zeros(gate.shape, F32)
        for m in range(c):
            gm = gate[m:m + 1, :]
            wins = jnp.where(gm > gate, 1.0, jnp.where(gm == gate, jnp.where(blk_id > m, 1.0, 0.0), 0.0))
            beaten = beaten + wins
        bias = jnp.where(past, jnp.where(beaten < MOBA_TOPK, 0.0, NEG_INF), NEG_INF)
        qs_sc[c % 2] = (q * (ATT_HD ** -0.5 * LOG2_E)).astype(BF16)
        yield
        for j in range(c + 1):
            for t in range(tiles_per_blk):
                keys = slice(j * blk + t * KEY_TILE, j * blk + (t + 1) * KEY_TILE)
                s = lax.dot_general(kb_ref[keys, :], qs_sc[c % 2], NT_DIMS, preferred_element_type=F32)
                if j == c:
                    s = jnp.where(causal[t * KEY_TILE:(t + 1) * KEY_TILE, :], s, NEG_INF)
                else:
                    s = s + bias[j:j + 1, :]
                s_sc[c % 2, keys, :] = s
                s_max = jnp.max(fold(s), axis=0)
                state[c] = s_max if c not in state else jnp.maximum(state[c], s_max)
                yield

    def pass2(c):
        m_col = jnp.max(state[c], axis=0, keepdims=True)
        n_keys = (c + 1) * blk
        l_acc = jnp.zeros((8, blk), F32)
        for t in range(n_keys // KEY_TILE):
            keys = slice(t * KEY_TILE, (t + 1) * KEY_TILE)
            p = jnp.exp2(s_sc[c % 2, keys, :] - m_col)
            l_acc = l_acc + jnp.sum(fold(p), axis=0)
            p_sc[c % 2, keys, :] = p.astype(BF16)
            yield
        l_col = jnp.sum(l_acc, axis=0, keepdims=True)
        acc = jnp.dot(vt_ref[:, :n_keys], p_sc[c % 2, :n_keys, :], preferred_element_type=F32)
        out = (acc / l_col).T
        o_ref[rows(c), :] = (out * _silu(az_ref[rows(c), :].astype(F32))).astype(BF16)
        yield

    ppb = MOBA_BLOCK // PAGE_SIZE
    rounds = max(nb - 1, 1)
    per_round = -(-n_pages // (rounds * ppb)) * ppb

    def pinned_means(lo, hi, anchor):
        for j in range(lo // ppb, hi // ppb):
            out = means_ref.at[pl.ds(j, 1)]
            _block_means(page_refs[ppb * j:ppb * (j + 1)], out)
            _pin_before_readers(anchor, out[0])
            yield

    _emit_round_robin([pass1(0)])
    for c in range(nb):
        streams = [pass2(c)]
        if c + 1 < nb:
            streams.append(pass1(c + 1))
        lo, hi = min(c * per_round, n_pages), min((c + 1) * per_round, n_pages)
        if hi > lo:
            streams.append(pinned_means(lo, hi, qs_sc.at[(c + 1) % 2 if c + 1 < nb else c % 2]))
        _emit_round_robin(streams)


def _moba(page_table_flat, q_h, kb_h, vt_h, az_h, km, cache_k, batch, seq, pages_step, first_page):
    n_heads, t_rows, _ = q_h.shape
    blk = MOBA_BLOCK
    nb = seq // blk
    hs = MOBA_HEADS_STEP
    rows_of = lambda b, h, pt: (h, b, 0)
    step_of = lambda b, h: b * (ATT_HEADS // hs) + h
    means_spec, means_shape = _means_out(pages_step, batch * ATT_HEADS // hs, step_of)
    return pl.pallas_call(
        functools.partial(_moba_kernel, n_pages=pages_step),
        grid_spec=pltpu.PrefetchScalarGridSpec(
            num_scalar_prefetch=1,
            grid=(batch, ATT_HEADS // hs),
            in_specs=[
                pl.BlockSpec((hs, seq, ATT_HD), rows_of),
                pl.BlockSpec((hs, seq, ATT_HD), rows_of),
                pl.BlockSpec((hs, ATT_HD, seq), lambda b, h, pt: (h, 0, b)),
                pl.BlockSpec((hs, seq, ATT_HD), rows_of),
                pl.BlockSpec((hs, nb, 8, ATT_HD), lambda b, h, pt: (h, b, 0, 0)),
            ] + _page_specs(pages_step, first_page, step_of),
            out_specs=[pl.BlockSpec((hs, seq, ATT_HD), rows_of), means_spec],
            scratch_shapes=[
                pltpu.VMEM((hs, 2, seq, blk), F32),
                pltpu.VMEM((hs, 2, seq, blk), BF16),
                pltpu.VMEM((hs, 2, blk, ATT_HD), BF16),
            ],
        ),
        out_shape=[jax.ShapeDtypeStruct((n_heads, t_rows, ATT_HD), BF16), means_shape],
        compiler_params=_params("arbitrary", "arbitrary"),
        name="moba_prompt",
    )(page_table_flat, q_h, kb_h, vt_h, az_h, km, *([cache_k] * pages_step))


def _mlstm_post(h, mo, mz, nw):
    h = h * _sigmoid(mo)
    h = h * lax.rsqrt(jnp.mean(h * h, axis=-1, keepdims=True) + EPS) * nw
    return h * _silu(mz)


def _dot_rows_hi_lo(mat, row, transposed):
    rows = jnp.broadcast_to(row, (8, row.shape[1]))
    hi = rows.astype(BF16)
    lo = (rows - hi.astype(F32)).astype(BF16)
    if transposed:
        return (lax.dot_general(mat, hi, NT_DIMS, preferred_element_type=F32)
                + lax.dot_general(mat, lo, NT_DIMS, preferred_element_type=F32))
    return jnp.dot(hi, mat, preferred_element_type=F32) + jnp.dot(lo, mat, preferred_element_type=F32)


def _mlstm_kernel(pt_ref, q_ref, k_ref, v_ref, mo_ref, mz_ref, g_ref, gt_ref, bgr_ref, bgc_ref, nw_ref, *refs,
                  n_pages):
    del pt_ref
    hm_ref, c_out, n_out, m_out, means_ref, c_sc, n_sc, m_sc = refs[n_pages:]
    c = pl.program_id(1)
    chunk = q_ref.shape[0]

    @pl.when(c == 0)
    def _():
        c_sc[...] = jnp.zeros(c_sc.shape, F32)
        n_sc[...] = jnp.zeros(n_sc.shape, F32)
        m_sc[...] = jnp.zeros(m_sc.shape, F32)

    _block_means(refs[:n_pages], means_ref)

    gc = g_ref[...] + bgr_ref[...]
    lane = lax.broadcasted_iota(jnp.int32, gc.shape, 1)
    gc = jnp.where(lane >= ML_HEADS, _log_sigmoid(gc), gc)
    gr = gt_ref[0:2 * ML_HEADS, :] + bgc_ref[0:2 * ML_HEADS, :]
    sub = lax.broadcasted_iota(jnp.int32, gr.shape, 0)
    gr = jnp.where(sub >= ML_HEADS, _log_sigmoid(gr), gr)

    t_ix = lax.broadcasted_iota(jnp.int32, (chunk, chunk), 0)
    s_ix = lax.broadcasted_iota(jnp.int32, (chunk, chunk), 1)
    causal = s_ix <= t_ix
    tril = jnp.where(causal, 1.0, 0.0)
    bc = jnp.dot(tril, gc, preferred_element_type=F32, precision=HIGHEST)
    br = lax.dot_general(gr, tril, NT_DIMS, preferred_element_type=F32, precision=HIGHEST)

    for h in range(ML_HEADS):
        cols = slice(h * ML_HD, (h + 1) * ML_HD)
        q, k, v = q_ref[:, cols], k_ref[:, cols], v_ref[:, cols]
        ig_r, b_r = gr[h:h + 1, :], br[ML_HEADS + h:ML_HEADS + h + 1, :]
        ig_c, b_c = gc[:, h:h + 1], bc[:, ML_HEADS + h:ML_HEADS + h + 1]
        m_prev = m_sc[h:h + 1, 0:1]
        c_prev = c_sc[h]
        n_prev = n_sc[h:h + 1, :]

        log_w = jnp.where(causal, b_c - b_r + ig_r, NEG_INF)
        log_inter = b_c + m_prev
        m_t = jnp.maximum(log_inter, jnp.max(log_w, axis=-1, keepdims=True))
        w_intra = jnp.exp(log_w - m_t)
        w_inter = jnp.exp(log_inter - m_t)
        s = lax.dot_general(q, k, NT_DIMS, preferred_element_type=F32) * w_intra
        num = (w_inter * jnp.dot(q, c_prev.astype(BF16), preferred_element_type=F32)
               + jnp.dot(s.astype(BF16), v, preferred_element_type=F32))
        qn = jnp.sum(q.astype(F32) * n_prev, axis=-1, keepdims=True)
        den = w_inter * qn + jnp.sum(s, axis=-1, keepdims=True)
        hh = num / jnp.maximum(jnp.abs(den), jnp.exp(-m_t))

        m_new = m_t[chunk - 1:chunk, :]
        b_last = b_c[chunk - 1:chunk, :]
        a_prev = jnp.exp(b_last + m_prev - m_new)
        a_c = jnp.exp(b_last - b_c + ig_c - m_new)
        a_r = jnp.exp(b_last - b_r + ig_r - m_new)
        kf = k.astype(F32)
        ka = (kf * a_c).astype(BF16)
        c_sc[h] = a_prev * c_prev + lax.dot_general(ka, v, TN_DIMS, preferred_element_type=F32)
        a_rows = jnp.broadcast_to(a_r, (8, chunk))
        n_sc[h:h + 1, :] = a_prev * n_prev + jnp.dot(
            a_rows, kf, preferred_element_type=F32, precision=HIGHEST)[0:1, :]
        m_sc[h:h + 1, :] = jnp.broadcast_to(m_new, (1, m_sc.shape[1]))

        hm_ref[:, cols] = _mlstm_post(
            hh, mo_ref[:, cols].astype(F32), mz_ref[:, cols].astype(F32), nw_ref[...]).astype(BF16)

    @pl.when(c == pl.num_programs(1) - 1)
    def _():
        c_out[0] = c_sc[...]
        n_out[0] = n_sc[...]
        m_out[0] = m_sc[...]


def _mlstm(page_table_flat, r, g, gt, bg_row, bg_col, mh_norm_w, cache_k, batch, seq, chunk, pages_step, first_page):
    t_rows = r.shape[0]
    d = ML_HEADS * ML_HD
    nc = seq // chunk
    sec = lambda j: (lambda b, c, pt: (b * nc + c, j))
    fixed = lambda b, c, pt: (0, 0)
    state = lambda b, c, pt: (b, 0, 0)
    step_of = lambda b, c: b * nc + c
    means_spec, means_shape = _means_out(pages_step, batch * nc, step_of)
    return pl.pallas_call(
        functools.partial(_mlstm_kernel, n_pages=pages_step),
        grid_spec=pltpu.PrefetchScalarGridSpec(
            num_scalar_prefetch=1,
            grid=(batch, nc),
            in_specs=[
                pl.BlockSpec((chunk, d), sec(SEC_MQ)),
                pl.BlockSpec((chunk, d), sec(SEC_MK)),
                pl.BlockSpec((chunk, d), sec(SEC_MV)),
                pl.BlockSpec((chunk, d), sec(SEC_MO)),
                pl.BlockSpec((chunk, d), sec(SEC_MZ)),
                pl.BlockSpec((chunk, GATE_LANES), lambda b, c, pt: (b * nc + c, 0)),
                pl.BlockSpec((GATE_LANES, chunk), lambda b, c, pt: (0, b * nc + c)),
                pl.BlockSpec((1, GATE_LANES), fixed),
                pl.BlockSpec((GATE_LANES, 1), fixed),
                pl.BlockSpec((1, ML_HD), fixed),
            ] + _page_specs(pages_step, first_page, step_of),
            out_specs=[
                pl.BlockSpec((chunk, d), lambda b, c, pt: (b * nc + c, 0)),
                pl.BlockSpec((1, ML_HEADS, ML_HD, ML_HD), lambda b, c, pt: (b, 0, 0, 0)),
                pl.BlockSpec((1, ML_HEADS, ML_HD), state),
                pl.BlockSpec((1, ML_HEADS, GATE_LANES), state),
                means_spec,
            ],
            scratch_shapes=[
                pltpu.VMEM((ML_HEADS, ML_HD, ML_HD), F32),
                pltpu.VMEM((ML_HEADS, ML_HD), F32),
                pltpu.VMEM((ML_HEADS, GATE_LANES), F32),
            ],
        ),
        out_shape=[
            jax.ShapeDtypeStruct((t_rows, d), BF16),
            jax.ShapeDtypeStruct((batch, ML_HEADS, ML_HD, ML_HD), F32),
            jax.ShapeDtypeStruct((batch, ML_HEADS, ML_HD), F32),
            jax.ShapeDtypeStruct((batch, ML_HEADS, GATE_LANES), F32),
            means_shape,
        ],
        compiler_params=_params("arbitrary", "arbitrary"),
        name="mlstm_prompt",
    )(page_table_flat, r, r, r, r, r, g, gt, bg_row, bg_col, mh_norm_w, *([cache_k] * pages_step))


def _sstep_kernel(q_ref, k_ref, v_ref, mo_ref, mz_ref, g_ref, bgr_ref, nw_ref, n_ref, m_ref, *refs):
    c_refs = refs[:ML_HEADS]
    hm_ref, c_out, n_out, m_out = refs[ML_HEADS:]
    for b in range(q_ref.shape[0]):
        gates = g_ref[b] + bgr_ref[...]
        for h in range(ML_HEADS):
            cols = slice(h * ML_HD, (h + 1) * ML_HD)
            q = q_ref[b, :, cols].astype(F32)
            k = k_ref[b, :, cols].astype(F32)
            v = v_ref[b, :, cols].astype(F32)
            ig = gates[:, h:h + 1]
            lf = _log_sigmoid(gates[:, ML_HEADS + h:ML_HEADS + h + 1])
            m_prev = m_ref[b, :, h:h + 1]
            c_prev = c_refs[h][b, 0]
            n_prev = n_ref[b, h:h + 1, :]

            log_inter = lf + m_prev
            m_t = jnp.maximum(log_inter, ig)
            w_intra = jnp.exp(ig - m_t)
            w_inter = jnp.exp(log_inter - m_t)
            s = jnp.sum(q * k, axis=-1, keepdims=True) * w_intra
            qc = _dot_rows_hi_lo(c_prev.astype(BF16), q, transposed=False)[0:1, :]
            num = w_inter * qc + s * v
            den = w_inter * jnp.sum(q * n_prev, axis=-1, keepdims=True) + s
            hh = num / jnp.maximum(jnp.abs(den), jnp.exp(-m_t))
            hm_ref[b, :, cols] = _mlstm_post(
                hh, mo_ref[b, :, cols].astype(F32), mz_ref[b, :, cols].astype(F32), nw_ref[...]).astype(BF16)

            sub = lax.broadcasted_iota(jnp.int32, (16, ML_HD), 0)
            k16 = jnp.where(sub == 0, jnp.broadcast_to(k, (16, ML_HD)), 0.0).astype(BF16)
            v16 = jnp.broadcast_to(v, (16, ML_HD)).astype(BF16)
            kv = lax.dot_general(k16, v16, TN_DIMS, preferred_element_type=F32)
            c_out[b, h] = w_inter * c_prev + w_intra * kv
            n_out[b, h:h + 1, :] = w_inter * n_prev + w_intra * k
            m_out[b, :, h:h + 1] = m_t


def _sstep(r3, g3, bg_row, mh_norm_w, c_state, n_state, m_state3, rows_step):
    nb = r3.shape[0]
    d = ML_HEADS * ML_HD
    sec = lambda j: (lambda b: (b, 0, j))
    fixed = lambda b: (0, 0)
    row3 = lambda b: (b, 0, 0)
    state4 = lambda b: (b, 0, 0, 0)
    return pl.pallas_call(
        _sstep_kernel,
        grid=(nb // rows_step,),
        in_specs=[
            pl.BlockSpec((rows_step, 1, d), sec(SEC_MQ)),
            pl.BlockSpec((rows_step, 1, d), sec(SEC_MK)),
            pl.BlockSpec((rows_step, 1, d), sec(SEC_MV)),
            pl.BlockSpec((rows_step, 1, d), sec(SEC_MO)),
            pl.BlockSpec((rows_step, 1, d), sec(SEC_MZ)),
            pl.BlockSpec((rows_step, 1, GATE_LANES), row3),
            pl.BlockSpec((1, GATE_LANES), fixed),
            pl.BlockSpec((1, ML_HD), fixed),
            pl.BlockSpec((rows_step, ML_HEADS, ML_HD), row3),
            pl.BlockSpec((rows_step, 1, ML_HEADS), row3),
        ] + [pl.BlockSpec((rows_step, 1, ML_HD, ML_HD), lambda b, h=h: (b, h, 0, 0)) for h in range(ML_HEADS)],
        out_specs=[
            pl.BlockSpec((rows_step, 1, d), row3),
            pl.BlockSpec((rows_step, ML_HEADS, ML_HD, ML_HD), state4),
            pl.BlockSpec((rows_step, ML_HEADS, ML_HD), row3),
            pl.BlockSpec((rows_step, 1, ML_HEADS), row3),
        ],
        out_shape=[
            jax.ShapeDtypeStruct((nb, 1, d), BF16),
            jax.ShapeDtypeStruct(c_state.shape, F32),
            jax.ShapeDtypeStruct(n_state.shape, F32),
            jax.ShapeDtypeStruct(m_state3.shape, F32),
        ],
        compiler_params=_params("arbitrary"),
        name="mlstm_sample",
    )(r3, r3, r3, r3, r3, g3, bg_row, mh_norm_w, n_state, m_state3, *([c_state] * ML_HEADS))


def _sgate_kernel(q_ref, kmean_ref, sel_ref):
    n_blocks = kmean_ref.shape[1]
    for b in range(q_ref.shape[0]):
        gate = jnp.sum(kmean_ref[b] * q_ref[b], axis=-1, keepdims=True)
        row = lax.broadcasted_iota(jnp.int32, gate.shape, 0).astype(F32)
        lane = lax.broadcasted_iota(jnp.int32, (ATT_HEADS, GATE_LANES), 1)
        picked = jnp.zeros((ATT_HEADS, GATE_LANES), F32)
        for j in range(MOBA_TOPK):
            best = jnp.max(gate, axis=0, keepdims=True)
            idx = jnp.min(jnp.where(gate == best, row, float(n_blocks)), axis=0, keepdims=True)
            picked = jnp.where(lane == j, idx[0], picked)
            gate = jnp.where(row == idx, NEG_INF, gate)
        sel_ref[b] = picked.astype(jnp.int32)


def _sgate(q_heads, kmean):
    nb, n_blocks = kmean.shape[:2]
    rows = _row_tile(nb, 8)
    return pl.pallas_call(
        _sgate_kernel,
        grid=(nb // rows,),
        in_specs=[pl.BlockSpec((rows, ATT_HEADS, ATT_HD), lambda b: (b, 0, 0)),
                  pl.BlockSpec((rows, n_blocks, ATT_HEADS, ATT_HD), lambda b: (b, 0, 0, 0))],
        out_specs=pl.BlockSpec((rows, ATT_HEADS, GATE_LANES), lambda b: (b, 0, 0)),
        out_shape=jax.ShapeDtypeStruct((nb, ATT_HEADS, GATE_LANES), jnp.int32),
        compiler_params=_params("arbitrary"),
        name="moba_sample_gate",
    )(q_heads, kmean)


def _sattn_copies(pt_ref, sel_ref, ck_ref, cv_ref, kbuf, vbuf, sems, b, slot):
    ppb = MOBA_BLOCK // PAGE_SIZE
    copies = []
    for h in range(ATT_HEADS):
        for j in range(MOBA_TOPK):
            block = sel_ref[b, h * MOBA_TOPK + j]
            for p in range(ppb):
                page = pt_ref[b, block * ppb + p]
                rows = pl.ds((j * ppb + p) * PAGE_SIZE, PAGE_SIZE)
                copies.append(pltpu.make_async_copy(ck_ref.at[page, :, h, :], kbuf.at[slot, h, rows, :], sems.at[0, slot]))
                copies.append(pltpu.make_async_copy(cv_ref.at[page, :, h, :], vbuf.at[slot, h, rows, :], sems.at[1, slot]))
    return copies


def _sattn_row(q, kn, vn, az, k_sel, v_sel):
    qs = q * (ATT_HD ** -0.5)
    s_new = jnp.sum(qs * kn, axis=-1, keepdims=True)
    sub = lax.broadcasted_iota(jnp.int32, (ATT_HEADS, ATT_HD), 0)
    out = jnp.zeros((ATT_HEADS, ATT_HD), F32)
    for h in range(ATT_HEADS):
        s = jnp.sum(k_sel[h] * qs[h:h + 1, :], axis=-1, keepdims=True)
        sn = s_new[h:h + 1, :]
        m = jnp.maximum(sn, jnp.max(s, axis=0, keepdims=True))
        p = jnp.exp(s - m)
        pn = jnp.exp(sn - m)
        l = pn + jnp.sum(p, axis=0, keepdims=True)
        acc = pn * vn[h:h + 1, :] + jnp.sum(p * v_sel[h], axis=0, keepdims=True)
        out = jnp.where(sub == h, acc / l, out)
    return out * _silu(az.astype(F32))


def _merge_kernel(*refs, n_side_rows, rows_step):
    if n_side_rows:
        pt_ref, sel_ref, refs = refs[0], refs[1], refs[2:]
    x_ref, am_ref, hm_ref, ga_ref, gm_ref, wa_ref, wm_ref, wo_ref = refs[:8]
    if n_side_rows:
        q_ref, kn_ref, vn_ref, az_ref, ck_ref, cv_ref, y_ref, o_ref, kbuf, vbuf, sems, mix_sc = refs[8:]
        side_tiles = []
        fetch = functools.partial(_sattn_copies, pt_ref, sel_ref, ck_ref, cv_ref, kbuf, vbuf, sems)
        step = pl.program_id(0)
        last_block = n_side_rows // rows_step - 1

        def row_of(s, j):
            return jnp.minimum(s, last_block) * rows_step + j

        def after(s, j):
            return (s, j + 1) if j + 1 < rows_step else (s + 1, 0)

        @pl.when(step == 0)
        def _():
            for cp in fetch(row_of(0, 0), 0):
                cp.start()

        for j in range(rows_step):
            cnt = step * rows_step + j
            for cp in fetch(row_of(*after(step, j)), (cnt + 1) % 2):
                cp.start()
            for cp in fetch(row_of(step, j), cnt % 2):
                cp.wait()
            side_out = _sattn_row(q_ref[j], kn_ref[j], vn_ref[j], az_ref[j], kbuf.at[cnt % 2], vbuf.at[cnt % 2])
            o_ref[j] = side_out.astype(BF16)
            side_tiles.append(side_out)
    else:
        y_ref = refs[8]

    am = jnp.concatenate([am_ref[h] for h in range(ATT_HEADS)], axis=1)
    y_att = jnp.dot(am, wa_ref[...], preferred_element_type=F32)
    y_ml = jnp.dot(hm_ref[...], wm_ref[...], preferred_element_type=F32)
    mix = _sigmoid(ga_ref[...].astype(F32)) * y_att + _sigmoid(gm_ref[...].astype(F32)) * y_ml
    if n_side_rows:
        mix_sc[...] = mix.astype(BF16)
        for tile in side_tiles:
            _pin_before_readers(mix_sc, tile)
        y_ref[...] = x_ref[...] + jnp.dot(mix_sc[...], wo_ref[...], preferred_element_type=F32)

        @pl.when(step == pl.num_programs(0) - 1)
        def _():
            for cp in fetch(row_of(step + 1, 0), ((step + 1) * rows_step) % 2):
                cp.wait()
    else:
        y_ref[...] = x_ref[...] + jnp.dot(mix.astype(BF16), wo_ref[...], preferred_element_type=F32)


def _merge(x, am, hm, r, wa, wm, wo, tm, side=None):
    t_rows, d = x.shape
    n_steps = t_rows // tm
    row = lambda i, *_: (i, 0)
    fixed = lambda i, *_: (0, 0)
    in_specs = [
        pl.BlockSpec((tm, d), row),
        pl.BlockSpec((ATT_HEADS, tm, ATT_HD), lambda i, *_: (0, i, 0)),
        pl.BlockSpec((tm, d), row),
        pl.BlockSpec((tm, d), lambda i, *_: (i, SEC_GA)),
        pl.BlockSpec((tm, d), lambda i, *_: (i, SEC_GM)),
        pl.BlockSpec((d, d), fixed),
        pl.BlockSpec((d, d), fixed),
        pl.BlockSpec((d, d), fixed),
    ]
    y_spec = pl.BlockSpec((tm, d), row)
    y_shape = jax.ShapeDtypeStruct((t_rows, d), F32)
    if side is None:
        return pl.pallas_call(
            functools.partial(_merge_kernel, n_side_rows=0, rows_step=0),
            grid=(n_steps,), in_specs=in_specs, out_specs=y_spec, out_shape=y_shape,
            compiler_params=_params("arbitrary"), name="merge",
        )(x, am, hm, r, r, wa, wm, wo)

    page_table, sel, q_heads, k_heads, v_heads, az_heads, cache_k, cache_v = side
    n_rows = q_heads.shape[0]
    rows_step = next(c for c in range(1, n_rows + 1) if n_rows % c == 0 and n_rows // c <= n_steps)
    n_blocks = n_rows // rows_step
    n_keys = MOBA_TOPK * MOBA_BLOCK
    head_spec = pl.BlockSpec((rows_step, ATT_HEADS, ATT_HD), lambda i, *_: (jnp.minimum(i, n_blocks - 1), 0, 0))
    return pl.pallas_call(
        functools.partial(_merge_kernel, n_side_rows=n_rows, rows_step=rows_step),
        grid_spec=pltpu.PrefetchScalarGridSpec(
            num_scalar_prefetch=2,
            grid=(n_steps,),
            in_specs=in_specs + [head_spec, head_spec, head_spec, head_spec,
                                 pl.BlockSpec(memory_space=pl.ANY), pl.BlockSpec(memory_space=pl.ANY)],
            out_specs=[y_spec, head_spec],
            scratch_shapes=[
                pltpu.VMEM((2, ATT_HEADS, n_keys, ATT_HD), F32),
                pltpu.VMEM((2, ATT_HEADS, n_keys, ATT_HD), F32),
                pltpu.SemaphoreType.DMA((2, 2)),
                pltpu.VMEM((tm, d), BF16),
            ],
        ),
        out_shape=[y_shape, jax.ShapeDtypeStruct((n_rows, ATT_HEADS, ATT_HD), BF16)],
        compiler_params=_params("arbitrary"),
        name="merge_with_sample_attend",
    )(page_table, sel, x, am, hm, r, r, wa, wm, wo, q_heads, k_heads, v_heads, az_heads, cache_k, cache_v)


def _row_tile(rows, target):
    tm = min(rows, target)
    while rows % tm:
        tm //= 2
    return tm


def kernel(x_prompt, x_sample, cache_k, cache_v, page_table, state_mlstm_C, state_mlstm_n, state_mlstm_m,
           norm_w, w_in, b_gates, q_norm_w, k_norm_w, mh_norm_w, w_proj_attn, w_proj_mlstm, w_out):
    batch, seq, d = x_prompt.shape
    dec_batch, dec_seq, _ = x_sample.shape
    assert dec_seq == 1 and d == ATT_HEADS * ATT_HD == ML_HEADS * ML_HD
    depth = w_in.shape[0]
    n_pages = page_table.shape[1]
    past = n_pages * PAGE_SIZE
    assert past % MOBA_BLOCK == 0 and seq % MOBA_BLOCK == 0
    t_p, t_s = batch * seq, dec_batch * dec_seq

    tm_proj = MOBA_BLOCK
    assert seq % tm_proj == 0
    chunk = _row_tile(seq, 256)
    rope_p = _rope_tables(jnp.arange(seq, dtype=jnp.int32))
    rope_s = _rope_tables(jnp.tile(past + jnp.arange(dec_seq, dtype=jnp.int32), dec_batch))

    ppb = MOBA_BLOCK // PAGE_SIZE
    steps = (t_p // tm_proj, batch * ATT_HEADS // MOBA_HEADS_STEP, batch * (seq // chunk))
    weights = (3, 4, 3)
    total_pages = dec_batch * n_pages
    unit = -(-total_pages // (sum(s * w for s, w in zip(steps, weights)) * ppb)) * ppb
    share = tuple(unit * w for w in weights)
    first = (0, steps[0] * share[0], steps[0] * share[0] + steps[1] * share[1])
    capacity = first[2] + steps[2] * share[2]
    pt_flat = jnp.pad(page_table.reshape(-1), (0, capacity - total_pages))

    y_p = x_prompt.reshape(t_p, d)
    y_s = x_sample.reshape(t_s, d)
    outs = [[] for _ in range(10)]
    for l in range(depth):
        w_wide = _cast_wide_weight(w_in, l, d)
        w_gate2 = _split_gate_weight(w_in[l][:, N_WIDE * d:])
        bg_row = jnp.pad(b_gates[l].astype(F32), (0, GATE_LANES - 2 * ML_HEADS)).reshape(1, GATE_LANES)
        bg_col = bg_row.reshape(GATE_LANES, 1)
        nw = norm_w[l].reshape(1, d)
        qnw, knw = q_norm_w[l].reshape(1, ATT_HD), k_norm_w[l].reshape(1, ATT_HD)
        mhw = mh_norm_w[l].reshape(1, ML_HD)
        wa, wm, wo = w_proj_attn[l].astype(BF16), w_proj_mlstm[l].astype(BF16), w_out[l].astype(BF16)

        q_h, k, kb_h, km, v, vt_h, az_h, r, g, gt, means0 = _inproj(
            pt_flat, y_p, nw, w_wide, w_gate2, rope_p, qnw, knw, cache_k[l], tm_proj, seq // tm_proj,
            share[0], first[0], k_split=WEIGHT_K_SPLIT_PROMPT)
        am_h, means1 = _moba(pt_flat, q_h, kb_h, vt_h, az_h, km, cache_k[l], batch, seq, share[1], first[1])
        hm, c_new, n_new, m_new, means2 = _mlstm(pt_flat, r, g, gt, bg_row, bg_col, mhw, cache_k[l], batch, seq, chunk,
                                                 share[2], first[2])
        for dst, val in zip(outs[:5], (k.reshape(batch, seq, ATT_HEADS, ATT_HD), v.reshape(batch, seq, ATT_HEADS, ATT_HD),
                                       c_new, n_new, m_new[:, :, 0])):
            dst.append(val)
        kmean = jnp.concatenate([means0, means1, means2], axis=0)[:total_pages // ppb]
        kmean = kmean.reshape(dec_batch, n_pages // ppb, ATT_HEADS, ATT_HD)
        r_p = r

        q_h, k, _, _, v, _, az_h, r, g, _ = _inproj(
            pt_flat, y_s, nw, w_wide, w_gate2, rope_s, qnw, knw, cache_k[l], t_s, 1, 0, 0, k_split=WEIGHT_K_SPLIT)
        r3 = r.reshape(t_s, 1, r.shape[-1])
        by_head = lambda t: t.reshape(t_s, ATT_HEADS, ATT_HD)
        rows_major = lambda t: jnp.swapaxes(t, 0, 1)
        q_rows = rows_major(q_h)
        sel = _sgate(q_rows, kmean)
        sel = sel[:, :, :MOBA_TOPK].reshape(t_s, ATT_HEADS * MOBA_TOPK)
        y_p, am_s = _merge(y_p, am_h, hm, r_p, wa, wm, wo, _row_tile(t_p, 512),
                           side=(page_table, sel, q_rows, by_head(k), by_head(v), rows_major(az_h),
                                 cache_k[l], cache_v[l]))
        hm, c_new, n_new, m_new = _sstep(
            r3, g.reshape(t_s, 1, GATE_LANES), bg_row, mhw,
            state_mlstm_C[l], state_mlstm_n[l], state_mlstm_m[l].reshape(dec_batch, 1, ML_HEADS), _row_tile(t_s, 4))
        y_s = _merge(y_s, rows_major(am_s), hm.reshape(t_s, d), r, wa, wm, wo, t_s)
        for dst, val in zip(outs[5:], (k.reshape(dec_batch, dec_seq, ATT_HEADS, ATT_HD),
                                       v.reshape(dec_batch, dec_seq, ATT_HEADS, ATT_HD),
                                       c_new, n_new, m_new.reshape(dec_batch, ML_HEADS))):
            dst.append(val)

    st = state_mlstm_C.dtype
    k_p, v_p, c_p, n_p, m_p, k_s, v_s, c_s, n_s, m_s = (jnp.stack(o) for o in outs)
    return (y_p.reshape(batch, seq, d), y_s.reshape(dec_batch, dec_seq, d),
            k_p, v_p, c_p.astype(st), n_p.astype(st), m_p.astype(st),
            k_s, v_s, c_s.astype(st), n_s.astype(st), m_s.astype(st))
```

```python
import functools

import jax
import jax.numpy as jnp
from jax import lax
from jax.experimental import pallas as pl
from jax.experimental.pallas import tpu as pltpu

F32 = jnp.float32
BF16 = jnp.bfloat16
HIGHEST = lax.Precision.HIGHEST
NEG_INF = float("-inf")

ATT_HEADS = 8
ATT_HD = 128
ROT_DIM = ATT_HD // 4
ROPE_THETA = 500000.0
MOBA_BLOCK = 256
MOBA_TOPK = 3
ML_HEADS = 4
ML_HD = 256
PAGE_SIZE = 128
EPS = 1e-6
N_WIDE = 11
N_ROW_SECTIONS = 7
SEC_MQ, SEC_MK, SEC_MV, SEC_MO, SEC_MZ, SEC_GA, SEC_GM = range(N_ROW_SECTIONS)
GATE_LANES = 128
N_INPROJ_OUTS = 10
LOG2_E = 1.4426950408889634
KEY_TILE = 256
MOBA_HEADS_STEP = 4
SAMPLE_K_SPLIT = 8
N_DMA_PRIORITIES = 2

VMEM_LIMIT_BYTES = 60000 * 1024

NT_DIMS = (((1,), (1,)), ((), ()))
TN_DIMS = (((0,), (0,)), ((), ()))


def _sigmoid(x):
    return 1.0 / (1.0 + jnp.exp(-x))


def _silu(x):
    return x * _sigmoid(x)


def _log_sigmoid(x):
    return jnp.minimum(x, 0.0) - jnp.log1p(jnp.exp(-jnp.abs(x)))


def _params(*semantics):
    return pltpu.CompilerParams(dimension_semantics=semantics, vmem_limit_bytes=VMEM_LIMIT_BYTES)


def _head_norm_rope(t, w, rc, rs1, rs2):
    half = ROT_DIM // 2
    outs = []
    for h in range(ATT_HEADS):
        th = t[:, h * ATT_HD:(h + 1) * ATT_HD]
        y = th * lax.rsqrt(jnp.mean(th * th, axis=-1, keepdims=True) + EPS) * w
        up = pltpu.roll(y, ATT_HD - half, axis=1)
        down = pltpu.roll(y, half, axis=1)
        outs.append(y * rc + up * rs1 + down * rs2)
    return outs


def _cast_kernel(x_ref, o_ref):
    o_ref[...] = x_ref[...].T.astype(o_ref.dtype)


def _cast_wide_weight(w_in, layer, d):
    return pl.pallas_call(
        _cast_kernel,
        grid=(N_WIDE,),
        in_specs=[pl.BlockSpec((None, d, d), lambda j: (layer, j, 0))],
        out_specs=pl.BlockSpec((None, d, d), lambda j: (j, 0, 0)),
        out_shape=jax.ShapeDtypeStruct((N_WIDE, d, d), BF16),
        compiler_params=_params("arbitrary"),
        name="cast_weight",
    )(jnp.swapaxes(w_in, 1, 2))


def _page_specs(n_step, first_page, step_of):
    def spec(p):
        return pl.BlockSpec((1, PAGE_SIZE, ATT_HEADS, ATT_HD),
                            lambda *ids: (ids[-1][first_page + step_of(*ids[:-1]) * n_step + p], 0, 0, 0))
    return [spec(p) for p in range(n_step)]


def _block_means_steps(page_refs, out_ref):
    ppb = MOBA_BLOCK // PAGE_SIZE
    for j in range(len(page_refs) // ppb):
        tot = jnp.sum(page_refs[ppb * j][0], axis=0)
        for p in range(1, ppb):
            tot = tot + jnp.sum(page_refs[ppb * j + p][0], axis=0)
        out_ref[j] = tot * (1.0 / MOBA_BLOCK)
        yield


def _block_means(page_refs, out_ref):
    for _ in _block_means_steps(page_refs, out_ref):
        pass


def _emit_round_robin(streams):
    streams = list(streams)
    while streams:
        for g in list(streams):
            if next(g, "done") == "done":
                streams.remove(g)


def _means_out(n_step, n_steps, step_of):
    ppb = MOBA_BLOCK // PAGE_SIZE
    spec = pl.BlockSpec((n_step // ppb, ATT_HEADS, ATT_HD), lambda *ids: (step_of(*ids[:-1]), 0, 0))
    return spec, jax.ShapeDtypeStruct((n_steps * n_step // ppb, ATT_HEADS, ATT_HD), F32)


def _pin_before_readers(anchor_ref, value):
    zero = pltpu.bitcast(lax.shift_right_logical(pltpu.bitcast(value, jnp.uint32), jnp.uint32(32)), F32)
    zero = jnp.concatenate([zero, zero], axis=0).astype(anchor_ref.dtype)
    anchor_ref[0:16, 0:GATE_LANES] = anchor_ref[0:16, 0:GATE_LANES] + zero


def _inproj_kernel(pt_ref, x_ref, nw_ref, wg_ref, rc_ref, rs1_ref, rs2_ref, qnw_ref, knw_ref, *refs, n_pages,
                   k_split):
    del pt_ref
    w_refs, refs = refs[:N_WIDE * k_split], refs[N_WIDE * k_split:]
    q_ref, k_ref, kb_ref, km_ref, v_ref, vt_ref, az_ref, r_ref, g_ref, gt_ref = refs[n_pages:n_pages + N_INPROJ_OUTS]
    xb_sc = refs[-1]
    d = x_ref.shape[1]
    x = x_ref[...]
    inv_rms = lax.rsqrt(jnp.mean(x * x, axis=-1, keepdims=True) + EPS)
    xw = x * nw_ref[...]
    xb = xw.astype(BF16)
    xb_sc[...] = xb

    n_gate = 2 * ML_HEADS
    x_lo = (xw - xb.astype(F32)).astype(BF16)
    both = (jnp.dot(xb_sc[...], wg_ref[...], preferred_element_type=F32)
            + jnp.dot(x_lo, wg_ref[...], preferred_element_type=F32)) * inv_rms
    lane = lax.broadcasted_iota(jnp.int32, both.shape, 1)
    g = jnp.where(lane < n_gate, both + pltpu.roll(both, GATE_LANES - n_gate, axis=1), 0.0)
    g_ref[...] = g
    gt_ref[...] = g.T

    ppb = MOBA_BLOCK // PAGE_SIZE
    page_blocks = [(refs[ppb * j:ppb * (j + 1)], refs[n_pages + N_INPROJ_OUTS].at[pl.ds(j, 1)])
                   for j in range(n_pages // ppb)]

    def section(i):
        share = -(-len(page_blocks) // N_WIDE)
        for pages, out in page_blocks[i * share:(i + 1) * share]:
            _block_means(pages, out)
            _pin_before_readers(xb_sc, out[0])
        kc = d // k_split
        acc = jnp.dot(xb_sc[:, 0:kc], w_refs[i * k_split][...], preferred_element_type=F32)
        for kk in range(1, k_split):
            acc = acc + jnp.dot(xb_sc[:, kk * kc:(kk + 1) * kc], w_refs[i * k_split + kk][...],
                                preferred_element_type=F32)
        return acc * inv_rms

    rc, rs1, rs2 = rc_ref[...], rs1_ref[...], rs2_ref[...]
    head = lambda h: slice(h * ATT_HD, (h + 1) * ATT_HD)
    for h, t in enumerate(_head_norm_rope(section(0), qnw_ref[...], rc, rs1, rs2)):
        q_ref[h] = t
    for h, t in enumerate(_head_norm_rope(section(1), knw_ref[...], rc, rs1, rs2)):
        k_ref[:, head(h)] = t
        kb_ref[h] = t.astype(BF16)
        km_ref[h, 0] = jnp.broadcast_to(jnp.sum(t, axis=0, keepdims=True) * (1.0 / t.shape[0]), (8, ATT_HD))
    v = section(2)
    v_ref[...] = v
    az = section(3)
    for h in range(ATT_HEADS):
        vt_ref[h] = v[:, head(h)].T.astype(BF16)
        az_ref[h] = az[:, head(h)].astype(BF16)
    for j in range(N_ROW_SECTIONS):
        t = section(4 + j)
        if j == SEC_MK:
            t = t * (ML_HD ** -0.5)
        r_ref[:, j * d:(j + 1) * d] = t.astype(BF16)


def _split_gate_weight(w_gate):
    hi = w_gate.astype(BF16)
    lo = (w_gate - hi.astype(F32)).astype(BF16)
    return jnp.pad(jnp.concatenate([hi, lo], axis=1), ((0, 0), (0, GATE_LANES - 2 * w_gate.shape[1])))


def _inproj(page_table_flat, x, norm_w, w_wide, w_gate2, rope, q_norm_w, k_norm_w, cache_k, tm, n_pos_blocks,
            pages_step, first_page, k_split=1):
    t_rows, d = x.shape
    rc, rs1, rs2 = rope
    n_steps = t_rows // tm
    row = lambda i, pt: (i, 0)
    fixed = lambda i, pt: (0, 0)
    pos = lambda i, pt: (i % n_pos_blocks, 0)
    once = pl.Buffered(1)
    step_of = lambda i: i
    w_blocks = w_wide.reshape(N_WIDE * k_split, d // k_split, d)
    wide_section = lambda j: pl.BlockSpec((None, d // k_split, d), lambda i, pt: (j, 0, 0), pipeline_mode=once)
    by_head = pl.BlockSpec((ATT_HEADS, tm, ATT_HD), lambda i, pt: (0, i, 0))
    out_specs = [
        by_head,
        pl.BlockSpec((tm, d), row),
        by_head,
        pl.BlockSpec((ATT_HEADS, 1, 8, ATT_HD), lambda i, pt: (0, i, 0, 0)),
        pl.BlockSpec((tm, d), row),
        pl.BlockSpec((ATT_HEADS, ATT_HD, tm), lambda i, pt: (0, 0, i)),
        by_head,
        pl.BlockSpec((tm, N_ROW_SECTIONS * d), row),
        pl.BlockSpec((tm, GATE_LANES), row),
        pl.BlockSpec((GATE_LANES, tm), lambda i, pt: (0, i)),
    ]
    out_shape = [
        jax.ShapeDtypeStruct((ATT_HEADS, t_rows, ATT_HD), F32),
        jax.ShapeDtypeStruct((t_rows, d), F32),
        jax.ShapeDtypeStruct((ATT_HEADS, t_rows, ATT_HD), BF16),
        jax.ShapeDtypeStruct((ATT_HEADS, n_steps, 8, ATT_HD), F32),
        jax.ShapeDtypeStruct((t_rows, d), F32),
        jax.ShapeDtypeStruct((ATT_HEADS, ATT_HD, t_rows), BF16),
        jax.ShapeDtypeStruct((ATT_HEADS, t_rows, ATT_HD), BF16),
        jax.ShapeDtypeStruct((t_rows, N_ROW_SECTIONS * d), BF16),
        jax.ShapeDtypeStruct((t_rows, GATE_LANES), F32),
        jax.ShapeDtypeStruct((GATE_LANES, t_rows), F32),
    ]
    assert len(out_specs) == N_INPROJ_OUTS
    if pages_step:
        spec, shape = _means_out(pages_step, n_steps, step_of)
        out_specs.append(spec)
        out_shape.append(shape)
    return pl.pallas_call(
        functools.partial(_inproj_kernel, n_pages=pages_step, k_split=k_split),
        grid_spec=pltpu.PrefetchScalarGridSpec(
            num_scalar_prefetch=1,
            grid=(n_steps,),
            in_specs=[
                pl.BlockSpec((tm, d), row),
                pl.BlockSpec((1, d), fixed),
                pl.BlockSpec((d, GATE_LANES), fixed, pipeline_mode=once),
                pl.BlockSpec((tm, ATT_HD), pos),
                pl.BlockSpec((tm, ATT_HD), pos),
                pl.BlockSpec((tm, ATT_HD), pos),
                pl.BlockSpec((1, ATT_HD), fixed),
                pl.BlockSpec((1, ATT_HD), fixed),
            ] + [wide_section(j) for j in range(N_WIDE * k_split)] + _page_specs(pages_step, first_page, step_of),
            out_specs=out_specs,
            scratch_shapes=[pltpu.VMEM((tm, d), BF16)],
        ),
        out_shape=out_shape,
        compiler_params=_params("arbitrary"),
        name="inproj",
    )(page_table_flat, x, norm_w, w_gate2, rc, rs1, rs2, q_norm_w, k_norm_w, *([w_blocks] * (N_WIDE * k_split)),
      *([cache_k] * pages_step))


def _rope_tables(pos):
    half = ROT_DIM // 2
    inv = ROPE_THETA ** (-(jnp.arange(half, dtype=F32) * 2.0) / ROT_DIM)
    ang = pos.astype(F32)[:, None] * inv[None, :]
    cos, sin = jnp.cos(ang), jnp.sin(ang)
    n = pos.shape[0]
    zeros = jnp.zeros((n, half), F32)
    tail0 = jnp.zeros((n, ATT_HD - ROT_DIM), F32)
    rc = jnp.concatenate([cos, cos, jnp.ones((n, ATT_HD - ROT_DIM), F32)], axis=-1)
    rs1 = jnp.concatenate([-sin, zeros, tail0], axis=-1)
    rs2 = jnp.concatenate([zeros, sin, tail0], axis=-1)
    return rc, rs1, rs2


def _moba_kernel(pt_ref, q_ref, kb_ref, vt_ref, az_ref, km_ref, *refs, n_pages):
    del pt_ref
    o_all, means_ref, s_all, p_all, qs_all = refs[n_pages:]
    page_refs = refs[:n_pages]
    heads_step = q_ref.shape[0]
    pages_head = n_pages // heads_step
    for hh in range(heads_step):
        _moba_head(q_ref.at[hh], kb_ref.at[hh], vt_ref.at[hh], az_ref.at[hh], km_ref.at[hh], o_all.at[hh],
                   s_all.at[hh], p_all.at[hh], qs_all.at[hh],
                   page_refs[hh * pages_head:(hh + 1) * pages_head],
                   means_ref.at[pl.ds(hh * pages_head // (MOBA_BLOCK // PAGE_SIZE), pages_head // (MOBA_BLOCK // PAGE_SIZE))])


def _moba_head(q_ref, kb_ref, vt_ref, az_ref, km_ref, o_ref, s_sc, p_sc, qs_sc, page_refs, means_ref):
    n_pages = len(page_refs)
    blk = MOBA_BLOCK
    nb = q_ref.shape[0] // blk
    nb_pad = -(-nb // 8) * 8
    rows = lambda j: slice(j * blk, (j + 1) * blk)

    sub8 = lax.broadcasted_iota(jnp.int32, (8, ATT_HD), 0)
    groups = []
    for g0 in range(0, nb, 8):
        tile = jnp.zeros((8, ATT_HD), F32)
        for j in range(g0, min(g0 + 8, nb)):
            tile = jnp.where(sub8 == j - g0, km_ref[j], tile)
        groups.append(tile)
    kmean = groups[0] if len(groups) == 1 else jnp.concatenate(groups, axis=0)

    blk_id = lax.broadcasted_iota(jnp.int32, (nb_pad, blk), 0)
    key_ix = lax.broadcasted_iota(jnp.int32, (blk, blk), 0)
    qry_ix = lax.broadcasted_iota(jnp.int32, (blk, blk), 1)
    causal = key_ix <= qry_ix

    tiles_per_blk = blk // KEY_TILE
    fold = lambda t: t.reshape(KEY_TILE // 8, 8, blk)
    state = {}

    def pass1(c):
        q = q_ref[rows(c), :]
        gate = lax.dot_general(kmean, q, NT_DIMS, preferred_element_type=F32, precision=HIGHEST)
        past = blk_id < c
        gate = jnp.where(past, gate, NEG_INF)
        beaten = jnp.zeros(gate.shape, F32)
        for m in range(c):
            gm = gate[m:m + 1, :]
            wins = jnp.where(gm > gate, 1.0, jnp.where(gm == gate, jnp.where(blk_id > m, 1.0, 0.0), 0.0))
            beaten = beaten + wins
        bias = jnp.where(past, jnp.where(beaten < MOBA_TOPK, 0.0, NEG_INF), NEG_INF)
        qs_sc[c % 2] = (q * (ATT_HD ** -0.5 * LOG2_E)).astype(BF16)
        yield
        for j in range(c + 1):
            for t in range(tiles_per_blk):
                keys = slice(j * blk + t * KEY_TILE, j * blk + (t + 1) * KEY_TILE)
                s = lax.dot_general(kb_ref[keys, :], qs_sc[c % 2], NT_DIMS, preferred_element_type=F32)
                if j == c:
                    s = jnp.where(causal[t * KEY_TILE:(t + 1) * KEY_TILE, :], s, NEG_INF)
                else:
                    s = s + bias[j:j + 1, :]
                s_sc[c % 2, keys, :] = s
                s_max = jnp.max(fold(s), axis=0)
                state[c] = s_max if c not in state else jnp.maximum(state[c], s_max)
                yield

    def pass2(c):
        m_col = jnp.max(state[c], axis=0, keepdims=True)
        n_keys = (c + 1) * blk
        l_acc = jnp.zeros((8, blk), F32)
        for t in range(n_keys // KEY_TILE):
            keys = slice(t * KEY_TILE, (t + 1) * KEY_TILE)
            p = jnp.exp2(s_sc[c % 2, keys, :] - m_col)
            l_acc = l_acc + jnp.sum(fold(p), axis=0)
            p_sc[c % 2, keys, :] = p.astype(BF16)
            yield
        l_col = jnp.sum(l_acc, axis=0, keepdims=True)
        acc = jnp.dot(vt_ref[:, :n_keys], p_sc[c % 2, :n_keys, :], preferred_element_type=F32)
        out = (acc / l_col).T
        o_ref[rows(c), :] = (out * _silu(az_ref[rows(c), :].astype(F32))).astype(BF16)
        yield

    ppb = MOBA_BLOCK // PAGE_SIZE
    rounds = max(nb - 1, 1)
    per_round = -(-n_pages // (rounds * ppb)) * ppb

    def pinned_means(lo, hi, anchor):
        for j in range(lo // ppb, hi // ppb):
            out = means_ref.at[pl.ds(j, 1)]
            _block_means(page_refs[ppb * j:ppb * (j + 1)], out)
            _pin_before_readers(anchor, out[0])
            yield

    _emit_round_robin([pass1(0)])
    for c in range(nb):
        streams = [pass2(c)]
        if c + 1 < nb:
            streams.append(pass1(c + 1))
        lo, hi = min(c * per_round, n_pages), min((c + 1) * per_round, n_pages)
        if hi > lo:
            streams.append(pinned_means(lo, hi, qs_sc.at[(c + 1) % 2 if c + 1 < nb else c % 2]))
        _emit_round_robin(streams)


def _moba(page_table_flat, q_h, kb_h, vt_h, az_h, km, cache_k, batch, seq, pages_step, first_page):
    n_heads, t_rows, _ = q_h.shape
    blk = MOBA_BLOCK
    nb = seq // blk
    hs = MOBA_HEADS_STEP
    rows_of = lambda b, h, pt: (h, b, 0)
    step_of = lambda b, h: b * (ATT_HEADS // hs) + h
    means_spec, means_shape = _means_out(pages_step, batch * ATT_HEADS // hs, step_of)
    return pl.pallas_call(
        functools.partial(_moba_kernel, n_pages=pages_step),
        grid_spec=pltpu.PrefetchScalarGridSpec(
            num_scalar_prefetch=1,
            grid=(batch, ATT_HEADS // hs),
            in_specs=[
                pl.BlockSpec((hs, seq, ATT_HD), rows_of),
                pl.BlockSpec((hs, seq, ATT_HD), rows_of),
                pl.BlockSpec((hs, ATT_HD, seq), lambda b, h, pt: (h, 0, b)),
                pl.BlockSpec((hs, seq, ATT_HD), rows_of),
                pl.BlockSpec((hs, nb, 8, ATT_HD), lambda b, h, pt: (h, b, 0, 0)),
            ] + _page_specs(pages_step, first_page, step_of),
            out_specs=[pl.BlockSpec((hs, seq, ATT_HD), rows_of), means_spec],
            scratch_shapes=[
                pltpu.VMEM((hs, 2, seq, blk), F32),
                pltpu.VMEM((hs, 2, seq, blk), BF16),
                pltpu.VMEM((hs, 2, blk, ATT_HD), BF16),
            ],
        ),
        out_shape=[jax.ShapeDtypeStruct((n_heads, t_rows, ATT_HD), BF16), means_shape],
        compiler_params=_params("arbitrary", "arbitrary"),
        name="moba_prompt",
    )(page_table_flat, q_h, kb_h, vt_h, az_h, km, *([cache_k] * pages_step))


def _mlstm_post(h, mo, mz, nw):
    h = h * _sigmoid(mo)
    h = h * lax.rsqrt(jnp.mean(h * h, axis=-1, keepdims=True) + EPS) * nw
    return h * _silu(mz)


def _dot_rows_hi_lo(mat, row, transposed):
    rows = jnp.broadcast_to(row, (8, row.shape[1]))
    hi = rows.astype(BF16)
    lo = (rows - hi.astype(F32)).astype(BF16)
    if transposed:
        return (lax.dot_general(mat, hi, NT_DIMS, preferred_element_type=F32)
                + lax.dot_general(mat, lo, NT_DIMS, preferred_element_type=F32))
    return jnp.dot(hi, mat, preferred_element_type=F32) + jnp.dot(lo, mat, preferred_element_type=F32)


def _mlstm_kernel(pt_ref, q_ref, k_ref, v_ref, mo_ref, mz_ref, g_ref, gt_ref, bgr_ref, bgc_ref, nw_ref, *refs,
                  n_pages):
    del pt_ref
    hm_ref, c_out, n_out, m_out, means_ref, c_sc, n_sc, m_sc = refs[n_pages:]
    c = pl.program_id(1)
    chunk = q_ref.shape[0]

    @pl.when(c == 0)
    def _():
        c_sc[...] = jnp.zeros(c_sc.shape, F32)
        n_sc[...] = jnp.zeros(n_sc.shape, F32)
        m_sc[...] = jnp.zeros(m_sc.shape, F32)

    _block_means(refs[:n_pages], means_ref)

    gc = g_ref[...] + bgr_ref[...]
    lane = lax.broadcasted_iota(jnp.int32, gc.shape, 1)
    gc = jnp.where(lane >= ML_HEADS, _log_sigmoid(gc), gc)
    gr = gt_ref[0:2 * ML_HEADS, :] + bgc_ref[0:2 * ML_HEADS, :]
    sub = lax.broadcasted_iota(jnp.int32, gr.shape, 0)
    gr = jnp.where(sub >= ML_HEADS, _log_sigmoid(gr), gr)

    t_ix = lax.broadcasted_iota(jnp.int32, (chunk, chunk), 0)
    s_ix = lax.broadcasted_iota(jnp.int32, (chunk, chunk), 1)
    causal = s_ix <= t_ix
    tril = jnp.where(causal, 1.0, 0.0)
    bc = jnp.dot(tril, gc, preferred_element_type=F32, precision=HIGHEST)
    br = lax.dot_general(gr, tril, NT_DIMS, preferred_element_type=F32, precision=HIGHEST)

    for h in range(ML_HEADS):
        cols = slice(h * ML_HD, (h + 1) * ML_HD)
        q, k, v = q_ref[:, cols], k_ref[:, cols], v_ref[:, cols]
        ig_r, b_r = gr[h:h + 1, :], br[ML_HEADS + h:ML_HEADS + h + 1, :]
        ig_c, b_c = gc[:, h:h + 1], bc[:, ML_HEADS + h:ML_HEADS + h + 1]
        m_prev = m_sc[h:h + 1, 0:1]
        c_prev = c_sc[h]
        n_prev = n_sc[h:h + 1, :]

        log_w = jnp.where(causal, b_c - b_r + ig_r, NEG_INF)
        log_inter = b_c + m_prev
        m_t = jnp.maximum(log_inter, jnp.max(log_w, axis=-1, keepdims=True))
        w_intra = jnp.exp(log_w - m_t)
        w_inter = jnp.exp(log_inter - m_t)
        s = lax.dot_general(q, k, NT_DIMS, preferred_element_type=F32) * w_intra
        num = (w_inter * jnp.dot(q, c_prev.astype(BF16), preferred_element_type=F32)
               + jnp.dot(s.astype(BF16), v, preferred_element_type=F32))
        qn = jnp.sum(q.astype(F32) * n_prev, axis=-1, keepdims=True)
        den = w_inter * qn + jnp.sum(s, axis=-1, keepdims=True)
        hh = num / jnp.maximum(jnp.abs(den), jnp.exp(-m_t))

        m_new = m_t[chunk - 1:chunk, :]
        b_last = b_c[chunk - 1:chunk, :]
        a_prev = jnp.exp(b_last + m_prev - m_new)
        a_c = jnp.exp(b_last - b_c + ig_c - m_new)
        a_r = jnp.exp(b_last - b_r + ig_r - m_new)
        kf = k.astype(F32)
        ka = (kf * a_c).astype(BF16)
        c_sc[h] = a_prev * c_prev + lax.dot_general(ka, v, TN_DIMS, preferred_element_type=F32)
        a_rows = jnp.broadcast_to(a_r, (8, chunk))
        n_sc[h:h + 1, :] = a_prev * n_prev + jnp.dot(
            a_rows, kf, preferred_element_type=F32, precision=HIGHEST)[0:1, :]
        m_sc[h:h + 1, :] = jnp.broadcast_to(m_new, (1, m_sc.shape[1]))

        hm_ref[:, cols] = _mlstm_post(
            hh, mo_ref[:, cols].astype(F32), mz_ref[:, cols].astype(F32), nw_ref[...]).astype(BF16)

    @pl.when(c == pl.num_programs(1) - 1)
    def _():
        c_out[0] = c_sc[...]
        n_out[0] = n_sc[...]
        m_out[0] = m_sc[...]


def _mlstm(page_table_flat, r, g, gt, bg_row, bg_col, mh_norm_w, cache_k, batch, seq, chunk, pages_step, first_page):
    t_rows = r.shape[0]
    d = ML_HEADS * ML_HD
    nc = seq // chunk
    sec = lambda j: (lambda b, c, pt: (b * nc + c, j))
    fixed = lambda b, c, pt: (0, 0)
    state = lambda b, c, pt: (b, 0, 0)
    step_of = lambda b, c: b * nc + c
    means_spec, means_shape = _means_out(pages_step, batch * nc, step_of)
    return pl.pallas_call(
        functools.partial(_mlstm_kernel, n_pages=pages_step),
        grid_spec=pltpu.PrefetchScalarGridSpec(
            num_scalar_prefetch=1,
            grid=(batch, nc),
            in_specs=[
                pl.BlockSpec((chunk, d), sec(SEC_MQ)),
                pl.BlockSpec((chunk, d), sec(SEC_MK)),
                pl.BlockSpec((chunk, d), sec(SEC_MV)),
                pl.BlockSpec((chunk, d), sec(SEC_MO)),
                pl.BlockSpec((chunk, d), sec(SEC_MZ)),
                pl.BlockSpec((chunk, GATE_LANES), lambda b, c, pt: (b * nc + c, 0)),
                pl.BlockSpec((GATE_LANES, chunk), lambda b, c, pt: (0, b * nc + c)),
                pl.BlockSpec((1, GATE_LANES), fixed),
                pl.BlockSpec((GATE_LANES, 1), fixed),
                pl.BlockSpec((1, ML_HD), fixed),
            ] + _page_specs(pages_step, first_page, step_of),
            out_specs=[
                pl.BlockSpec((chunk, d), lambda b, c, pt: (b * nc + c, 0)),
                pl.BlockSpec((1, ML_HEADS, ML_HD, ML_HD), lambda b, c, pt: (b, 0, 0, 0)),
                pl.BlockSpec((1, ML_HEADS, ML_HD), state),
                pl.BlockSpec((1, ML_HEADS, GATE_LANES), state),
                means_spec,
            ],
            scratch_shapes=[
                pltpu.VMEM((ML_HEADS, ML_HD, ML_HD), F32),
                pltpu.VMEM((ML_HEADS, ML_HD), F32),
                pltpu.VMEM((ML_HEADS, GATE_LANES), F32),
            ],
        ),
        out_shape=[
            jax.ShapeDtypeStruct((t_rows, d), BF16),
            jax.ShapeDtypeStruct((batch, ML_HEADS, ML_HD, ML_HD), F32),
            jax.ShapeDtypeStruct((batch, ML_HEADS, ML_HD), F32),
            jax.ShapeDtypeStruct((batch, ML_HEADS, GATE_LANES), F32),
            means_shape,
        ],
        compiler_params=_params("arbitrary", "arbitrary"),
        name="mlstm_prompt",
    )(page_table_flat, r, r, r, r, r, g, gt, bg_row, bg_col, mh_norm_w, *([cache_k] * pages_step))


def _sstep_kernel(q_ref, k_ref, v_ref, mo_ref, mz_ref, g_ref, bgr_ref, nw_ref, c_ref, n_ref, m_ref,
                  hm_ref, c_out, n_out, m_out):
    for b in range(q_ref.shape[0]):
        gates = g_ref[b] + bgr_ref[...]
        for h in range(ML_HEADS):
            cols = slice(h * ML_HD, (h + 1) * ML_HD)
            q = q_ref[b, :, cols].astype(F32)
            k = k_ref[b, :, cols].astype(F32)
            v = v_ref[b, :, cols].astype(F32)
            ig = gates[:, h:h + 1]
            lf = _log_sigmoid(gates[:, ML_HEADS + h:ML_HEADS + h + 1])
            m_prev = m_ref[b, :, h:h + 1]
            c_prev = c_ref[b, h]
            n_prev = n_ref[b, h:h + 1, :]

            log_inter = lf + m_prev
            m_t = jnp.maximum(log_inter, ig)
            w_intra = jnp.exp(ig - m_t)
            w_inter = jnp.exp(log_inter - m_t)
            s = jnp.sum(q * k, axis=-1, keepdims=True) * w_intra
            qc = _dot_rows_hi_lo(c_prev.astype(BF16), q, transposed=False)[0:1, :]
            num = w_inter * qc + s * v
            den = w_inter * jnp.sum(q * n_prev, axis=-1, keepdims=True) + s
            hh = num / jnp.maximum(jnp.abs(den), jnp.exp(-m_t))
            hm_ref[b, :, cols] = _mlstm_post(
                hh, mo_ref[b, :, cols].astype(F32), mz_ref[b, :, cols].astype(F32), nw_ref[...]).astype(BF16)

            sub = lax.broadcasted_iota(jnp.int32, (16, ML_HD), 0)
            k16 = jnp.where(sub == 0, jnp.broadcast_to(k, (16, ML_HD)), 0.0).astype(BF16)
            v16 = jnp.broadcast_to(v, (16, ML_HD)).astype(BF16)
            kv = lax.dot_general(k16, v16, TN_DIMS, preferred_element_type=F32)
            c_out[b, h] = w_inter * c_prev + w_intra * kv
            n_out[b, h:h + 1, :] = w_inter * n_prev + w_intra * k
            m_out[b, :, h:h + 1] = m_t


def _sstep(r3, g3, bg_row, mh_norm_w, c_state, n_state, m_state3, rows_step):
    nb = r3.shape[0]
    d = ML_HEADS * ML_HD
    sec = lambda j: (lambda b: (b, 0, j))
    fixed = lambda b: (0, 0)
    row3 = lambda b: (b, 0, 0)
    state4 = lambda b: (b, 0, 0, 0)
    return pl.pallas_call(
        _sstep_kernel,
        grid=(nb // rows_step,),
        in_specs=[
            pl.BlockSpec((rows_step, 1, d), sec(SEC_MQ)),
            pl.BlockSpec((rows_step, 1, d), sec(SEC_MK)),
            pl.BlockSpec((rows_step, 1, d), sec(SEC_MV)),
            pl.BlockSpec((rows_step, 1, d), sec(SEC_MO)),
            pl.BlockSpec((rows_step, 1, d), sec(SEC_MZ)),
            pl.BlockSpec((rows_step, 1, GATE_LANES), row3),
            pl.BlockSpec((1, GATE_LANES), fixed),
            pl.BlockSpec((1, ML_HD), fixed),
            pl.BlockSpec((rows_step, ML_HEADS, ML_HD, ML_HD), state4),
            pl.BlockSpec((rows_step, ML_HEADS, ML_HD), row3),
            pl.BlockSpec((rows_step, 1, ML_HEADS), row3),
        ],
        out_specs=[
            pl.BlockSpec((rows_step, 1, d), row3),
            pl.BlockSpec((rows_step, ML_HEADS, ML_HD, ML_HD), state4),
            pl.BlockSpec((rows_step, ML_HEADS, ML_HD), row3),
            pl.BlockSpec((rows_step, 1, ML_HEADS), row3),
        ],
        out_shape=[
            jax.ShapeDtypeStruct((nb, 1, d), BF16),
            jax.ShapeDtypeStruct(c_state.shape, F32),
            jax.ShapeDtypeStruct(n_state.shape, F32),
            jax.ShapeDtypeStruct(m_state3.shape, F32),
        ],
        compiler_params=_params("arbitrary"),
        name="mlstm_sample",
    )(r3, r3, r3, r3, r3, g3, bg_row, mh_norm_w, c_state, n_state, m_state3)


def _sgate_kernel(q_ref, kmean_ref, sel_ref):
    n_blocks = kmean_ref.shape[1]
    for b in range(q_ref.shape[0]):
        gate = jnp.sum(kmean_ref[b] * q_ref[b], axis=-1, keepdims=True)
        row = lax.broadcasted_iota(jnp.int32, gate.shape, 0).astype(F32)
        lane = lax.broadcasted_iota(jnp.int32, (ATT_HEADS, GATE_LANES), 1)
        picked = jnp.zeros((ATT_HEADS, GATE_LANES), F32)
        for j in range(MOBA_TOPK):
            best = jnp.max(gate, axis=0, keepdims=True)
            idx = jnp.min(jnp.where(gate == best, row, float(n_blocks)), axis=0, keepdims=True)
            picked = jnp.where(lane == j, idx[0], picked)
            gate = jnp.where(row == idx, NEG_INF, gate)
        sel_ref[b] = picked.astype(jnp.int32)


def _sgate(q_heads, kmean):
    nb, n_blocks = kmean.shape[:2]
    rows = _row_tile(nb, 8)
    return pl.pallas_call(
        _sgate_kernel,
        grid=(nb // rows,),
        in_specs=[pl.BlockSpec((rows, ATT_HEADS, ATT_HD), lambda b: (b, 0, 0)),
                  pl.BlockSpec((rows, n_blocks, ATT_HEADS, ATT_HD), lambda b: (b, 0, 0, 0))],
        out_specs=pl.BlockSpec((rows, ATT_HEADS, GATE_LANES), lambda b: (b, 0, 0)),
        out_shape=jax.ShapeDtypeStruct((nb, ATT_HEADS, GATE_LANES), jnp.int32),
        compiler_params=_params("arbitrary"),
        name="moba_sample_gate",
    )(q_heads, kmean)


def _sattn_copies(pt_ref, sel_ref, ck_ref, cv_ref, kbuf, vbuf, sems, b, slot):
    ppb = MOBA_BLOCK // PAGE_SIZE
    copies = []
    for h in range(ATT_HEADS):
        for j in range(MOBA_TOPK):
            block = sel_ref[b, h * MOBA_TOPK + j]
            for p in range(ppb):
                page = pt_ref[b, block * ppb + p]
                rows = pl.ds((j * ppb + p) * PAGE_SIZE, PAGE_SIZE)
                copies.append(pltpu.make_async_copy(ck_ref.at[page, :, h, :], kbuf.at[slot, h, rows, :], sems.at[0, slot]))
                copies.append(pltpu.make_async_copy(cv_ref.at[page, :, h, :], vbuf.at[slot, h, rows, :], sems.at[1, slot]))
    return copies


def _sattn_row(q, kn, vn, az, k_sel, v_sel):
    qs = q * (ATT_HD ** -0.5)
    s_new = jnp.sum(qs * kn, axis=-1, keepdims=True)
    sub = lax.broadcasted_iota(jnp.int32, (ATT_HEADS, ATT_HD), 0)
    out = jnp.zeros((ATT_HEADS, ATT_HD), F32)
    for h in range(ATT_HEADS):
        s = jnp.sum(k_sel[h] * qs[h:h + 1, :], axis=-1, keepdims=True)
        sn = s_new[h:h + 1, :]
        m = jnp.maximum(sn, jnp.max(s, axis=0, keepdims=True))
        p = jnp.exp(s - m)
        pn = jnp.exp(sn - m)
        l = pn + jnp.sum(p, axis=0, keepdims=True)
        acc = pn * vn[h:h + 1, :] + jnp.sum(p * v_sel[h], axis=0, keepdims=True)
        out = jnp.where(sub == h, acc / l, out)
    return out * _silu(az.astype(F32))


def _merge_kernel(*refs, n_side_rows, rows_step):
    if n_side_rows:
        pt_ref, sel_ref, refs = refs[0], refs[1], refs[2:]
    x_ref, am_ref, hm_ref, ga_ref, gm_ref, wa_ref, wm_ref, wo_ref = refs[:8]
    if n_side_rows:
        q_ref, kn_ref, vn_ref, az_ref, ck_ref, cv_ref, y_ref, o_ref, kbuf, vbuf, sems, mix_sc = refs[8:]
        side_tiles = []
        fetch = functools.partial(_sattn_copies, pt_ref, sel_ref, ck_ref, cv_ref, kbuf, vbuf, sems)
        step = pl.program_id(0)
        last_block = n_side_rows // rows_step - 1

        def row_of(s, j):
            return jnp.minimum(s, last_block) * rows_step + j

        def after(s, j):
            return (s, j + 1) if j + 1 < rows_step else (s + 1, 0)

        @pl.when(step == 0)
        def _():
            for i, cp in enumerate(fetch(row_of(0, 0), 0)):
                cp.start(priority=i % N_DMA_PRIORITIES)

        for j in range(rows_step):
            cnt = step * rows_step + j
            for i, cp in enumerate(fetch(row_of(*after(step, j)), (cnt + 1) % 2)):
                cp.start(priority=i % N_DMA_PRIORITIES)
            for cp in fetch(row_of(step, j), cnt % 2):
                cp.wait()
            side_out = _sattn_row(q_ref[j], kn_ref[j], vn_ref[j], az_ref[j], kbuf.at[cnt % 2], vbuf.at[cnt % 2])
            o_ref[j] = side_out.astype(BF16)
            side_tiles.append(side_out)
    else:
        y_ref = refs[8]

    am = jnp.concatenate([am_ref[h] for h in range(ATT_HEADS)], axis=1)
    y_att = jnp.dot(am, wa_ref[...], preferred_element_type=F32)
    y_ml = jnp.dot(hm_ref[...], wm_ref[...], preferred_element_type=F32)
    mix = _sigmoid(ga_ref[...].astype(F32)) * y_att + _sigmoid(gm_ref[...].astype(F32)) * y_ml
    if n_side_rows:
        mix_sc[...] = mix.astype(BF16)
        for tile in side_tiles:
            _pin_before_readers(mix_sc, tile)
        y_ref[...] = x_ref[...] + jnp.dot(mix_sc[...], wo_ref[...], preferred_element_type=F32)

        @pl.when(step == pl.num_programs(0) - 1)
        def _():
            for cp in fetch(row_of(step + 1, 0), ((step + 1) * rows_step) % 2):
                cp.wait()
    else:
        y_ref[...] = x_ref[...] + jnp.dot(mix.astype(BF16), wo_ref[...], preferred_element_type=F32)


def _merge(x, am, hm, r, wa, wm, wo, tm, side=None):
    t_rows, d = x.shape
    n_steps = t_rows // tm
    row = lambda i, *_: (i, 0)
    fixed = lambda i, *_: (0, 0)
    in_specs = [
        pl.BlockSpec((tm, d), row),
        pl.BlockSpec((ATT_HEADS, tm, ATT_HD), lambda i, *_: (0, i, 0)),
        pl.BlockSpec((tm, d), row),
        pl.BlockSpec((tm, d), lambda i, *_: (i, SEC_GA)),
        pl.BlockSpec((tm, d), lambda i, *_: (i, SEC_GM)),
        pl.BlockSpec((d, d), fixed),
        pl.BlockSpec((d, d), fixed),
        pl.BlockSpec((d, d), fixed),
    ]
    y_spec = pl.BlockSpec((tm, d), row)
    y_shape = jax.ShapeDtypeStruct((t_rows, d), F32)
    if side is None:
        return pl.pallas_call(
            functools.partial(_merge_kernel, n_side_rows=0, rows_step=0),
            grid=(n_steps,), in_specs=in_specs, out_specs=y_spec, out_shape=y_shape,
            compiler_params=_params("arbitrary"), name="merge",
        )(x, am, hm, r, r, wa, wm, wo)

    page_table, sel, q_heads, k_heads, v_heads, az_heads, cache_k, cache_v = side
    n_rows = q_heads.shape[0]
    rows_step = next(c for c in range(1, n_rows + 1) if n_rows % c == 0 and n_rows // c <= n_steps)
    n_blocks = n_rows // rows_step
    n_keys = MOBA_TOPK * MOBA_BLOCK
    head_spec = pl.BlockSpec((rows_step, ATT_HEADS, ATT_HD), lambda i, *_: (jnp.minimum(i, n_blocks - 1), 0, 0))
    return pl.pallas_call(
        functools.partial(_merge_kernel, n_side_rows=n_rows, rows_step=rows_step),
        grid_spec=pltpu.PrefetchScalarGridSpec(
            num_scalar_prefetch=2,
            grid=(n_steps,),
            in_specs=in_specs + [head_spec, head_spec, head_spec, head_spec,
                                 pl.BlockSpec(memory_space=pl.ANY), pl.BlockSpec(memory_space=pl.ANY)],
            out_specs=[y_spec, head_spec],
            scratch_shapes=[
                pltpu.VMEM((2, ATT_HEADS, n_keys, ATT_HD), F32),
                pltpu.VMEM((2, ATT_HEADS, n_keys, ATT_HD), F32),
                pltpu.SemaphoreType.DMA((2, 2)),
                pltpu.VMEM((tm, d), BF16),
            ],
        ),
        out_shape=[y_shape, jax.ShapeDtypeStruct((n_rows, ATT_HEADS, ATT_HD), BF16)],
        compiler_params=_params("arbitrary"),
        name="merge_with_sample_attend",
    )(page_table, sel, x, am, hm, r, r, wa, wm, wo, q_heads, k_heads, v_heads, az_heads, cache_k, cache_v)


def _row_tile(rows, target):
    tm = min(rows, target)
    while rows % tm:
        tm //= 2
    return tm


def kernel(x_prompt, x_sample, cache_k, cache_v, page_table, state_mlstm_C, state_mlstm_n, state_mlstm_m,
           norm_w, w_in, b_gates, q_norm_w, k_norm_w, mh_norm_w, w_proj_attn, w_proj_mlstm, w_out):
    batch, seq, d = x_prompt.shape
    dec_batch, dec_seq, _ = x_sample.shape
    assert dec_seq == 1 and d == ATT_HEADS * ATT_HD == ML_HEADS * ML_HD
    depth = w_in.shape[0]
    n_pages = page_table.shape[1]
    past = n_pages * PAGE_SIZE
    assert past % MOBA_BLOCK == 0 and seq % MOBA_BLOCK == 0
    t_p, t_s = batch * seq, dec_batch * dec_seq

    tm_proj = MOBA_BLOCK
    assert seq % tm_proj == 0
    chunk = _row_tile(seq, 256)
    rope_p = _rope_tables(jnp.arange(seq, dtype=jnp.int32))
    rope_s = _rope_tables(jnp.tile(past + jnp.arange(dec_seq, dtype=jnp.int32), dec_batch))

    ppb = MOBA_BLOCK // PAGE_SIZE
    steps = (t_p // tm_proj, batch * ATT_HEADS // MOBA_HEADS_STEP, batch * (seq // chunk))
    weights = (3, 4, 4)
    total_pages = dec_batch * n_pages
    unit = -(-total_pages // (sum(s * w for s, w in zip(steps, weights)) * ppb)) * ppb
    share = tuple(unit * w for w in weights)
    first = (0, steps[0] * share[0], steps[0] * share[0] + steps[1] * share[1])
    capacity = first[2] + steps[2] * share[2]
    pt_flat = jnp.pad(page_table.reshape(-1), (0, capacity - total_pages))

    y_p = x_prompt.reshape(t_p, d)
    y_s = x_sample.reshape(t_s, d)
    outs = [[] for _ in range(10)]
    for l in range(depth):
        w_wide = _cast_wide_weight(w_in, l, d)
        w_gate2 = _split_gate_weight(w_in[l][:, N_WIDE * d:])
        bg_row = jnp.pad(b_gates[l].astype(F32), (0, GATE_LANES - 2 * ML_HEADS)).reshape(1, GATE_LANES)
        bg_col = bg_row.reshape(GATE_LANES, 1)
        nw = norm_w[l].reshape(1, d)
        qnw, knw = q_norm_w[l].reshape(1, ATT_HD), k_norm_w[l].reshape(1, ATT_HD)
        mhw = mh_norm_w[l].reshape(1, ML_HD)
        wa, wm, wo = w_proj_attn[l].astype(BF16), w_proj_mlstm[l].astype(BF16), w_out[l].astype(BF16)

        q_h, k, kb_h, km, v, vt_h, az_h, r, g, gt, means0 = _inproj(
            pt_flat, y_p, nw, w_wide, w_gate2, rope_p, qnw, knw, cache_k[l], tm_proj, seq // tm_proj,
            share[0], first[0])
        am_h, means1 = _moba(pt_flat, q_h, kb_h, vt_h, az_h, km, cache_k[l], batch, seq, share[1], first[1])
        hm, c_new, n_new, m_new, means2 = _mlstm(pt_flat, r, g, gt, bg_row, bg_col, mhw, cache_k[l], batch, seq, chunk,
                                                 share[2], first[2])
        for dst, val in zip(outs[:5], (k.reshape(batch, seq, ATT_HEADS, ATT_HD), v.reshape(batch, seq, ATT_HEADS, ATT_HD),
                                       c_new, n_new, m_new[:, :, 0])):
            dst.append(val)
        kmean = jnp.concatenate([means0, means1, means2], axis=0)[:total_pages // ppb]
        kmean = kmean.reshape(dec_batch, n_pages // ppb, ATT_HEADS, ATT_HD)
        r_p = r

        q_h, k, _, _, v, _, az_h, r, g, _ = _inproj(
            pt_flat, y_s, nw, w_wide, w_gate2, rope_s, qnw, knw, cache_k[l], t_s, 1, 0, 0, k_split=SAMPLE_K_SPLIT)
        r3 = r.reshape(t_s, 1, r.shape[-1])
        by_head = lambda t: t.reshape(t_s, ATT_HEADS, ATT_HD)
        rows_major = lambda t: jnp.swapaxes(t, 0, 1)
        q_rows = rows_major(q_h)
        sel = _sgate(q_rows, kmean)
        sel = sel[:, :, :MOBA_TOPK].reshape(t_s, ATT_HEADS * MOBA_TOPK)
        y_p, am_s = _merge(y_p, am_h, hm, r_p, wa, wm, wo, _row_tile(t_p, 512),
                           side=(page_table, sel, q_rows, by_head(k), by_head(v), rows_major(az_h),
                                 cache_k[l], cache_v[l]))
        hm, c_new, n_new, m_new = _sstep(
            r3, g.reshape(t_s, 1, GATE_LANES), bg_row, mhw,
            state_mlstm_C[l], state_mlstm_n[l], state_mlstm_m[l].reshape(dec_batch, 1, ML_HEADS), _row_tile(t_s, 4))
        y_s = _merge(y_s, rows_major(am_s), hm.reshape(t_s, d), r, wa, wm, wo, t_s)
        for dst, val in zip(outs[5:], (k.reshape(dec_batch, dec_seq, ATT_HEADS, ATT_HD),
                                       v.reshape(dec_batch, dec_seq, ATT_HEADS, ATT_HD),
                                       c_new, n_new, m_new.reshape(dec_batch, ML_HEADS))):
            dst.append(val)

    st = state_mlstm_C.dtype
    k_p, v_p, c_p, n_p, m_p, k_s, v_s, c_s, n_s, m_s = (jnp.stack(o) for o in outs)
    return (y_p.reshape(batch, seq, d), y_s.reshape(dec_batch, dec_seq, d),
            k_p, v_p, c_p.astype(st), n_p.astype(st), m_p.astype(st),
            k_s, v_s, c_s.astype(st), n_s.astype(st), m_s.astype(st))
```
